```python
import jax, jax.numpy as jnp
from jax import lax
import numpy as np

D_MODEL = 1024
BATCH = 8
SEQ = 4096
DEPTH = 4

GRID_W = 64
CTX_LEN = 256
N_Q_HEADS = 8
N_KV_HEADS = 2
Q_PER_KV = N_Q_HEADS // N_KV_HEADS
HEAD_DIM = 64
WINDOW = 128
ATTN_BLOCK = WINDOW
ROPE_THETA = 10000.0
FNET_GROUPS = 4
FNET_GROUP_DIM = 128
ATTN_WIDTH = N_Q_HEADS * HEAD_DIM
KV_WIDTH = N_KV_HEADS * HEAD_DIM
FNET_WIDTH = FNET_GROUPS * FNET_GROUP_DIM
AB_IN_WIDTH = ATTN_WIDTH + 2 * KV_WIDTH + FNET_WIDTH
AB_OUT_WIDTH = ATTN_WIDTH + FNET_WIDTH
SGU_GROUPS = 8
SGU_GROUP_DIM = 128
SGU_WIDTH = SGU_GROUPS * SGU_GROUP_DIM
SGU_CHUNK = 128
PEER_HEADS = 8
PEER_NKEYS = 128
PEER_TOPK = 16
PEER_KEY_DIM = 256
PEER_EXPERTS = PEER_NKEYS * PEER_NKEYS
PEER_BLOCK = 128
N_MOD = 6
N_EVEN = (DEPTH + 1) // 2
N_ODD = DEPTH // 2
RMS_EPS = 1e-6
NEG_INF = -1e30

kernel_name = "hybrid_dit_swa_fnet_sgu_peer"


def rmsnorm(x, w):
    x32 = x.astype(jnp.float32)
    y = x32 * lax.rsqrt(jnp.mean(x32 * x32, axis=-1, keepdims=True) + RMS_EPS)
    return (y * w.astype(jnp.float32)).astype(x.dtype)


def modulate(h, shift, scale):
    return h * (1 + scale) + shift


def ada_params(cond, w, b):
    m = jax.nn.silu(cond) @ w + b
    return jnp.split(m, N_MOD, axis=-1)


def axial_angles(n_tokens):
    rows = n_tokens // GRID_W
    r, cl = jnp.meshgrid(jnp.arange(rows), jnp.arange(GRID_W), indexing="ij")
    row = r.reshape(-1).astype(jnp.float32)
    col = cl.reshape(-1).astype(jnp.float32)
    half = HEAD_DIM // 2
    inv = ROPE_THETA ** (-jnp.arange(0, half, 2, dtype=jnp.float32) / half)
    return row[:, None] * inv, col[:, None] * inv


def rope_half(x, ang):
    h = x.shape[-1] // 2
    a, b = x[..., :h], x[..., h:]
    shape = (ang.shape[0],) + (1,) * (x.ndim - 3) + (h,)
    cos = jnp.cos(ang).reshape(shape).astype(x.dtype)
    sin = jnp.sin(ang).reshape(shape).astype(x.dtype)
    return jnp.concatenate([a * cos - b * sin, b * cos + a * sin], axis=-1)


def rope_axial(x, ang_row, ang_col):
    h = HEAD_DIM // 2
    return jnp.concatenate([rope_half(x[..., :h], ang_row), rope_half(x[..., h:], ang_col)], axis=-1)


def banded(t, nb):
    pad = [(0, 0), (ATTN_BLOCK, ATTN_BLOCK)] + [(0, 0)] * (t.ndim - 2)
    tp = jnp.pad(t, pad).reshape((t.shape[0], nb + 2, ATTN_BLOCK) + t.shape[2:])
    return jnp.concatenate([tp[:, :-2], tp[:, 1:-1], tp[:, 2:]], axis=2)


def latent_window_attention(q, k, v, k_ctx, v_ctx, sink):
    B, S = q.shape[:2]
    nb = S // ATTN_BLOCK
    scale = HEAD_DIM ** -0.5
    qb = q.reshape(B, nb, ATTN_BLOCK, N_KV_HEADS, Q_PER_KV, HEAD_DIM)
    kb = banded(k, nb)
    vb = banded(v, nb)
    qi = jnp.arange(ATTN_BLOCK)[:, None]
    kj = jnp.arange(3 * ATTN_BLOCK)[None, :] - ATTN_BLOCK
    kpos = jnp.arange(nb)[:, None, None] * ATTN_BLOCK + kj
    mask = (jnp.abs(qi - kj) <= WINDOW) & (kpos >= 0) & (kpos < S)
    s_loc = jnp.einsum("bnqhgd,bnkhd->bnhgqk", qb, kb).astype(jnp.float32) * scale
    s_loc = jnp.where(mask[None, :, None, None], s_loc, NEG_INF)
    s_ctx = jnp.einsum("bnqhgd,bchd->bnhgqc", qb, k_ctx).astype(jnp.float32) * scale
    sk = sink.astype(jnp.float32).reshape(N_KV_HEADS, Q_PER_KV)[:, :, None, None]
    m = jnp.maximum(jnp.maximum(s_loc.max(-1, keepdims=True), s_ctx.max(-1, keepdims=True)), sk)
    p_loc = jnp.exp(s_loc - m)
    p_ctx = jnp.exp(s_ctx - m)
    inv_den = 1.0 / (p_loc.sum(-1, keepdims=True) + p_ctx.sum(-1, keepdims=True) + jnp.exp(sk - m))
    o = (jnp.einsum("bnhgqk,bnkhd->bnqhgd", (p_loc * inv_den).astype(v.dtype), vb)
         + jnp.einsum("bnhgqc,bchd->bnqhgd", (p_ctx * inv_den).astype(v.dtype), v_ctx))
    return o.reshape(B, S, ATTN_WIDTH)


def context_attention(q, k, v, sink):
    B, C = q.shape[:2]
    scale = HEAD_DIM ** -0.5
    s = jnp.einsum("bqhgd,bkhd->bhgqk", q, k).astype(jnp.float32) * scale
    sk = sink.astype(jnp.float32).reshape(N_KV_HEADS, Q_PER_KV)[:, :, None, None]
    m = jnp.maximum(s.max(-1, keepdims=True), sk)
    p = jnp.exp(s - m)
    p = p / (p.sum(-1, keepdims=True) + jnp.exp(sk - m))
    o = jnp.einsum("bhgqk,bkhd->bqhgd", p.astype(v.dtype), v)
    return o.reshape(B, C, ATTN_WIDTH)


def fourier_mix(f):
    B, n = f.shape[:2]
    fg = f.reshape(B, n, FNET_GROUPS, FNET_GROUP_DIM).astype(jnp.float32)
    y = jnp.fft.fft2(fg, axes=(1, 3), norm="ortho").real
    return y.reshape(B, n, FNET_WIDTH).astype(f.dtype)


def ab_mixer(h_l, h_c, w_in, w_out, sink, ang_row, ang_col, with_ctx_out):
    B, S = h_l.shape[:2]
    C = h_c.shape[1]
    p_l = h_l @ w_in
    q_l = p_l[..., :ATTN_WIDTH].reshape(B, S, N_KV_HEADS, Q_PER_KV, HEAD_DIM)
    k_l = p_l[..., ATTN_WIDTH:ATTN_WIDTH + KV_WIDTH].reshape(B, S, N_KV_HEADS, HEAD_DIM)
    v_l = p_l[..., ATTN_WIDTH + KV_WIDTH:ATTN_WIDTH + 2 * KV_WIDTH].reshape(B, S, N_KV_HEADS, HEAD_DIM)
    f_l = p_l[..., ATTN_WIDTH + 2 * KV_WIDTH:]
    q_l = rope_axial(q_l, ang_row, ang_col)
    k_l = rope_axial(k_l, ang_row, ang_col)
    off = ATTN_WIDTH if with_ctx_out else 0
    p_c = h_c @ (w_in if with_ctx_out else w_in[:, ATTN_WIDTH:ATTN_WIDTH + 2 * KV_WIDTH])
    k_c = p_c[..., off:off + KV_WIDTH].reshape(B, C, N_KV_HEADS, HEAD_DIM)
    v_c = p_c[..., off + KV_WIDTH:off + 2 * KV_WIDTH].reshape(B, C, N_KV_HEADS, HEAD_DIM)
    a_l = latent_window_attention(q_l, k_l, v_l, k_c, v_c, sink)
    y_l = jnp.concatenate([a_l, fourier_mix(f_l)], axis=-1) @ w_out
    y_c = None
    if with_ctx_out:
        q_c = p_c[..., :ATTN_WIDTH].reshape(B, C, N_KV_HEADS, Q_PER_KV, HEAD_DIM)
        f_c = p_c[..., ATTN_WIDTH + 2 * KV_WIDTH:]
        a_c = context_attention(q_c, k_c, v_c, sink)
        y_c = jnp.concatenate([a_c, fourier_mix(f_c)], axis=-1) @ w_out
    return y_l, y_c


def sgu_mixer(h, w_in, norm_w, w_s, b_s, w_out):
    B, n = h.shape[:2]
    nc = n // SGU_CHUNK
    z = jax.nn.gelu(h @ w_in)
    u, v = z[..., :SGU_WIDTH], z[..., SGU_WIDTH:]
    v = rmsnorm(v, norm_w).reshape(B, nc, SGU_CHUNK, SGU_GROUPS, SGU_GROUP_DIM)
    sv = jnp.einsum("gpq,bnqgc->bnpgc", w_s, v) + b_s.T[:, :, None]
    return (u * sv.reshape(B, n, SGU_WIDTH)) @ w_out


def peer_ffn(h, w_q, keys, u_tab, v_tab):
    shp = h.shape
    xb = h.reshape(-1, PEER_BLOCK, D_MODEL)

    def block(xc):
        q = (xc @ w_q).reshape(PEER_BLOCK, PEER_HEADS, 2, PEER_KEY_DIM // 2)
        s = jnp.einsum("chsd,hskd->chsk", q, keys).astype(jnp.float32)
        top_s, top_i = lax.top_k(s, PEER_TOPK)
        cand = top_s[:, :, 0, :, None] + top_s[:, :, 1, None, :]
        best_s, best = lax.top_k(cand.reshape(PEER_BLOCK, PEER_HEADS, PEER_TOPK * PEER_TOPK), PEER_TOPK)
        i1 = jnp.take_along_axis(top_i[:, :, 0], best // PEER_TOPK, axis=-1)
        i2 = jnp.take_along_axis(top_i[:, :, 1], best % PEER_TOPK, axis=-1)
        e = i1 * PEER_NKEYS + i2
        g = jax.nn.softmax(best_s, axis=-1).astype(xc.dtype)
        act = jax.nn.gelu(jnp.einsum("cd,chkd->chk", xc, u_tab[e]))
        return jnp.einsum("chk,chkd->cd", g * act, v_tab[e])

    return lax.map(block, xb).reshape(shp)


def setup_inputs(seed: int = 0) -> dict:
    key = jax.random.key(seed)
    ks = jax.random.split(key, 21)
    D = D_MODEL

    def nrm(k, shape, s):
        return jax.random.normal(k, shape, jnp.float32) * s

    return {
        "x": nrm(ks[0], (BATCH, SEQ, D), 1.0),
        "c": nrm(ks[1], (BATCH, D), 1.0),
        "ctx": nrm(ks[2], (BATCH, CTX_LEN, D), 1.0),
        "c_ctx": nrm(ks[3], (D,), 1.0),
        "ada_w": nrm(ks[4], (DEPTH, D, N_MOD * D), 0.3 * D ** -0.5),
        "ada_b": nrm(ks[5], (DEPTH, N_MOD * D), 0.02),
        "norm_mix_w": 1.0 + nrm(ks[6], (DEPTH, D), 0.02),
        "norm_ffn_w": 1.0 + nrm(ks[7], (DEPTH, D), 0.02),
        "ab_w_in": nrm(ks[8], (N_EVEN, D, AB_IN_WIDTH), D ** -0.5),
        "ab_w_out": nrm(ks[9], (N_EVEN, AB_OUT_WIDTH, D), AB_OUT_WIDTH ** -0.5),
        "attn_sink": nrm(ks[10], (N_EVEN, N_Q_HEADS), 0.5),
        "gs_w_in": nrm(ks[11], (N_ODD, D, 2 * SGU_WIDTH), D ** -0.5),
        "gs_norm_w": 1.0 + nrm(ks[12], (N_ODD, SGU_WIDTH), 0.02),
        "gs_w_s": nrm(ks[13], (N_ODD, SGU_GROUPS, SGU_CHUNK, SGU_CHUNK), SGU_CHUNK ** -0.5),
        "gs_b_s": 1.0 + nrm(ks[14], (N_ODD, SGU_GROUPS, SGU_CHUNK), 0.1),
        "gs_w_out": nrm(ks[15], (N_ODD, SGU_WIDTH, D), SGU_WIDTH ** -0.5),
        "peer_w_q": nrm(ks[16], (DEPTH, D, PEER_HEADS * PEER_KEY_DIM), D ** -0.5),
        "peer_keys": nrm(ks[17], (DEPTH, PEER_HEADS, 2, PEER_NKEYS, PEER_KEY_DIM // 2), (PEER_KEY_DIM // 2) ** -0.5),
        "peer_u": nrm(ks[18], (DEPTH, PEER_EXPERTS, D), D ** -0.5),
        "peer_v": nrm(ks[19], (DEPTH, PEER_EXPERTS, D), 1.0),
        "norm_out_w": 1.0 + nrm(ks[20], (D,), 0.02),
    }


def reference(x, c, ctx, c_ctx, ada_w, ada_b, norm_mix_w, norm_ffn_w, ab_w_in, ab_w_out, attn_sink,
              gs_w_in, gs_norm_w, gs_w_s, gs_b_s, gs_w_out, peer_w_q, peer_keys, peer_u, peer_v, norm_out_w):
    ang_row, ang_col = axial_angles(x.shape[1])
    last_attn = ((DEPTH - 1) // 2) * 2
    for i in range(DEPTH):
        j = i // 2
        is_ab = i % 2 == 0
        upd_ctx = i < last_attn
        need_ctx = upd_ctx or is_ab
        sh_m, sc_m, g_m, sh_f, sc_f, g_f = [m[:, None, :] for m in ada_params(c, ada_w[i], ada_b[i])]
        h_l = modulate(rmsnorm(x, norm_mix_w[i]), sh_m, sc_m)
        y_c = None
        if need_ctx:
            csh_m, csc_m, cg_m, csh_f, csc_f, cg_f = ada_params(c_ctx, ada_w[i], ada_b[i])
            h_c = modulate(rmsnorm(ctx, norm_mix_w[i]), csh_m, csc_m)
        if is_ab:
            y_l, y_c = ab_mixer(h_l, h_c, ab_w_in[j], ab_w_out[j], attn_sink[j], ang_row, ang_col, upd_ctx)
        else:
            y_l = sgu_mixer(h_l, gs_w_in[j], gs_norm_w[j], gs_w_s[j], gs_b_s[j], gs_w_out[j])
            if upd_ctx:
                y_c = sgu_mixer(h_c, gs_w_in[j], gs_norm_w[j], gs_w_s[j], gs_b_s[j], gs_w_out[j])
        x = x + g_m * y_l
        x = x + g_f * peer_ffn(modulate(rmsnorm(x, norm_ffn_w[i]), sh_f, sc_f),
                               peer_w_q[i], peer_keys[i], peer_u[i], peer_v[i])
        if upd_ctx:
            ctx = ctx + cg_m * y_c
            ctx = ctx + cg_f * peer_ffn(modulate(rmsnorm(ctx, norm_ffn_w[i]), csh_f, csc_f),
                                        peer_w_q[i], peer_keys[i], peer_u[i], peer_v[i])
    return rmsnorm(x, norm_out_w)
```

```python
import functools
import math

import jax
import jax.numpy as jnp
from jax import lax
from jax.experimental import pallas as pl
from jax.experimental.pallas import tpu as pltpu
from jax.experimental.pallas import tpu_sc as plsc

F32 = jnp.float32
BF16 = jnp.bfloat16
I32 = jnp.int32
U32 = jnp.uint32

GRID_W = 64
N_Q_HEADS = 8
N_KV_HEADS = 2
Q_PER_KV = N_Q_HEADS // N_KV_HEADS
HEAD_DIM = 64
WINDOW = 128
ROPE_THETA = 10000.0
ATTN_WIDTH = N_Q_HEADS * HEAD_DIM
KV_WIDTH = N_KV_HEADS * HEAD_DIM
QK_WIDTH = ATTN_WIDTH + KV_WIDTH
FNET_GROUPS = 4
FNET_GROUP_DIM = 128
FNET_WIDTH = FNET_GROUPS * FNET_GROUP_DIM
AB_IN_WIDTH = ATTN_WIDTH + 2 * KV_WIDTH + FNET_WIDTH
SGU_GROUPS = 8
SGU_GROUP_DIM = 128
SGU_WIDTH = SGU_GROUPS * SGU_GROUP_DIM
SGU_CHUNK = 128
PEER_HEADS = 8
PEER_NKEYS = 128
PEER_TOPK = 16
PEER_HALF_DIM = 128
PEER_SEL = PEER_HEADS * PEER_TOPK
N_MOD = 6
RMS_EPS = 1e-6
NEG_INF = -1e30

SC_CORES = 2
SC_SUBCORES = 16
SC_WORKERS = SC_CORES * SC_SUBCORES
SC_GATHER_ROWS = 64
PEER_TOK_BLOCK = 16
VMEM_LIMIT = 48 * 1024 * 1024


def _cparams(*sem):
    return pltpu.CompilerParams(dimension_semantics=sem, vmem_limit_bytes=VMEM_LIMIT)


def _norm_mod(x, nw, shift, scale):
    y = x * lax.rsqrt(jnp.mean(x * x, axis=-1, keepdims=True) + RMS_EPS) * nw
    return y * (1.0 + scale) + shift


def _bdot(a, b):
    return jnp.dot(a, b, preferred_element_type=F32)


def _ada_kernel(c_ref, w_ref, b_ref, o_ref):
    c = c_ref[...]
    s = (c * jax.nn.sigmoid(c)).astype(BF16)
    o_ref[0] = _bdot(s, w_ref[0].astype(BF16)) + b_ref[0]


def _ada_all(cond, ada_w, ada_b):
    depth, d, n = ada_w.shape
    r = cond.shape[0]
    tn = 1536
    return pl.pallas_call(
        _ada_kernel,
        grid=(depth, n // tn),
        in_specs=[pl.BlockSpec((r, d), lambda l, j: (0, 0)),
                  pl.BlockSpec((1, d, tn), lambda l, j: (l, 0, j)),
                  pl.BlockSpec((1, 1, tn), lambda l, j: (l, 0, j))],
        out_specs=pl.BlockSpec((1, r, tn), lambda l, j: (l, 0, j)),
        out_shape=jax.ShapeDtypeStruct((depth, r, n), F32),
        compiler_params=_cparams("parallel", "parallel"),
        name="ada",
    )(cond, ada_w, ada_b.reshape(depth, 1, n))


def _ab_in_kernel(x_ref, nw_ref, sh_ref, sc_ref, w_ref, cos_ref, sin_ref, dft_ref,
                  q_ref, k_ref, v_ref, fa_ref, fb_ref, *, rope):
    h = _norm_mod(x_ref[0], nw_ref[...], sh_ref[0], sc_ref[0]).astype(BF16)
    p = _bdot(h, w_ref[...])
    qk = p[:, :QK_WIDTH]
    if rope:
        qk = qk * cos_ref[...] + p[:, AB_IN_WIDTH:] * sin_ref[...]
    q_ref[0] = qk[:, :ATTN_WIDTH].astype(BF16)
    k_ref[0] = qk[:, ATTN_WIDTH:].astype(BF16)
    v_ref[0] = p[:, QK_WIDTH:QK_WIDTH + KV_WIDTH].astype(BF16)
    f0 = QK_WIDTH + KV_WIDTH
    for g in range(FNET_GROUPS):
        fg = p[:, f0 + g * FNET_GROUP_DIM:f0 + (g + 1) * FNET_GROUP_DIM].astype(BF16)
        ab = _bdot(fg, dft_ref[...])
        fa_ref[0, :, g * FNET_GROUP_DIM:(g + 1) * FNET_GROUP_DIM] = ab[:, :FNET_GROUP_DIM].astype(BF16)
        fb_ref[0, :, g * FNET_GROUP_DIM:(g + 1) * FNET_GROUP_DIM] = ab[:, FNET_GROUP_DIM:].astype(BF16)


def _ab_in(x, nw, shift, scale, w_ext, cos, sin, dft_c, rope):
    b, s, d = x.shape
    tm = min(512, s)
    n_ext = w_ext.shape[1]
    outs = [jax.ShapeDtypeStruct((b, s, ATTN_WIDTH), BF16), jax.ShapeDtypeStruct((b, s, KV_WIDTH), BF16),
            jax.ShapeDtypeStruct((b, s, KV_WIDTH), BF16), jax.ShapeDtypeStruct((b, s, FNET_WIDTH), BF16),
            jax.ShapeDtypeStruct((b, s, FNET_WIDTH), BF16)]
    tok = lambda w: pl.BlockSpec((1, tm, w), lambda i, j: (i, j, 0))
    return pl.pallas_call(
        functools.partial(_ab_in_kernel, rope=rope),
        grid=(b, s // tm),
        in_specs=[tok(d),
                  pl.BlockSpec((1, d), lambda i, j: (0, 0)),
                  pl.BlockSpec((1, 1, d), lambda i, j: (i, 0, 0)),
                  pl.BlockSpec((1, 1, d), lambda i, j: (i, 0, 0)),
                  pl.BlockSpec((d, n_ext), lambda i, j: (0, 0)),
                  pl.BlockSpec((tm, QK_WIDTH), lambda i, j: (j, 0)),
                  pl.BlockSpec((tm, QK_WIDTH), lambda i, j: (j, 0)),
                  pl.BlockSpec(dft_c.shape, lambda i, j: (0, 0))],
        out_specs=[tok(ATTN_WIDTH), tok(KV_WIDTH), tok(KV_WIDTH), tok(FNET_WIDTH), tok(FNET_WIDTH)],
        out_shape=outs,
        compiler_params=_cparams("parallel", "parallel"),
        name="ab_in",
    )(x, nw, shift, scale, w_ext, cos, sin, dft_c)


def _softmax_pv(qh, kk, vv, valid, sk, scale):
    s = lax.dot_general(qh, kk, (((1,), (1,)), ((), ())), preferred_element_type=F32) * scale
    if valid is not None:
        s = jnp.where(valid, s, NEG_INF)
    m = jnp.maximum(jnp.max(s, axis=-1, keepdims=True), sk)
    p = jnp.exp(s - m)
    inv_den = 1.0 / (jnp.sum(p, axis=-1, keepdims=True) + jnp.exp(sk - m))
    return _bdot((p * inv_den).astype(BF16), vv)


def _attn_local_kernel(q_ref, kp_ref, ko_ref, kn_ref, vp_ref, vo_ref, vn_ref, kc_ref, vc_ref, sink_ref, o_ref,
                       *, seq):
    j = pl.program_id(1)
    blk = q_ref.shape[1]
    n_ctx = kc_ref.shape[1]
    q = q_ref[0]
    kl = jnp.concatenate([kp_ref[0], ko_ref[0], kn_ref[0], kc_ref[0]], axis=0)
    vl = jnp.concatenate([vp_ref[0], vo_ref[0], vn_ref[0], vc_ref[0]], axis=0)
    nk = 3 * blk + n_ctx
    qi = lax.broadcasted_iota(I32, (blk, nk), 0)
    cj = lax.broadcasted_iota(I32, (blk, nk), 1)
    kj = cj - blk
    kpos = j * blk + kj
    valid = (cj >= 3 * blk) | ((jnp.abs(qi - kj) <= WINDOW) & (kpos >= 0) & (kpos < seq))
    scale = HEAD_DIM ** -0.5
    for kvh in range(N_KV_HEADS):
        kk = kl[:, kvh * HEAD_DIM:(kvh + 1) * HEAD_DIM]
        vv = vl[:, kvh * HEAD_DIM:(kvh + 1) * HEAD_DIM]
        for g in range(Q_PER_KV):
            h = kvh * Q_PER_KV + g
            o = _softmax_pv(q[:, h * HEAD_DIM:(h + 1) * HEAD_DIM], kk, vv, valid, sink_ref[0:1, h:h + 1], scale)
            o_ref[0, :, h * HEAD_DIM:(h + 1) * HEAD_DIM] = o.astype(BF16)


def _attn_local(q, k, v, kc, vc, sink):
    b, s, _ = q.shape
    c = kc.shape[1]
    blk = WINDOW
    nb = s // blk
    qspec = pl.BlockSpec((1, blk, ATTN_WIDTH), lambda i, j: (i, j, 0))
    prev = pl.BlockSpec((1, blk, KV_WIDTH), lambda i, j: (i, jnp.maximum(j - 1, 0), 0))
    own = pl.BlockSpec((1, blk, KV_WIDTH), lambda i, j: (i, j, 0))
    nxt = pl.BlockSpec((1, blk, KV_WIDTH), lambda i, j: (i, jnp.minimum(j + 1, nb - 1), 0))
    cspec = pl.BlockSpec((1, c, KV_WIDTH), lambda i, j: (i, 0, 0))
    return pl.pallas_call(
        functools.partial(_attn_local_kernel, seq=s),
        grid=(b, nb),
        in_specs=[qspec, prev, own, nxt, prev, own, nxt, cspec, cspec,
                  pl.BlockSpec((1, N_Q_HEADS), lambda i, j: (0, 0))],
        out_specs=qspec,
        out_shape=jax.ShapeDtypeStruct((b, s, ATTN_WIDTH), BF16),
        compiler_params=_cparams("parallel", "parallel"),
        name="attn_local",
    )(q, k, k, k, v, v, v, kc, vc, sink)


def _attn_ctx_kernel(q_ref, k_ref, v_ref, sink_ref, o_ref):
    q = q_ref[0]
    scale = HEAD_DIM ** -0.5
    for kvh in range(N_KV_HEADS):
        kk = k_ref[0][:, kvh * HEAD_DIM:(kvh + 1) * HEAD_DIM]
        vv = v_ref[0][:, kvh * HEAD_DIM:(kvh + 1) * HEAD_DIM]
        for g in range(Q_PER_KV):
            h = kvh * Q_PER_KV + g
            o = _softmax_pv(q[:, h * HEAD_DIM:(h + 1) * HEAD_DIM], kk, vv, None, sink_ref[0:1, h:h + 1], scale)
            o_ref[0, :, h * HEAD_DIM:(h + 1) * HEAD_DIM] = o.astype(BF16)


def _attn_ctx(q, k, v, sink):
    b, c, _ = q.shape
    return pl.pallas_call(
        _attn_ctx_kernel,
        grid=(b,),
        in_specs=[pl.BlockSpec((1, c, ATTN_WIDTH), lambda i: (i, 0, 0)),
                  pl.BlockSpec((1, c, KV_WIDTH), lambda i: (i, 0, 0)),
                  pl.BlockSpec((1, c, KV_WIDTH), lambda i: (i, 0, 0)),
                  pl.BlockSpec((1, N_Q_HEADS), lambda i: (0, 0))],
        out_specs=pl.BlockSpec((1, c, ATTN_WIDTH), lambda i: (i, 0, 0)),
        out_shape=jax.ShapeDtypeStruct((b, c, ATTN_WIDTH), BF16),
        compiler_params=_cparams("parallel"),
        name="attn_ctx",
    )(q, k, v, sink)


def _dft_pos_kernel(cs_ref, ss_ref, a_ref, b_ref, o_ref, acc_ref, *, scale):
    kk = pl.program_id(2)

    @pl.when(kk == 0)
    def _():
        acc_ref[...] = jnp.zeros_like(acc_ref)

    acc_ref[...] += _bdot(cs_ref[...], a_ref[0]) - _bdot(ss_ref[...], b_ref[0])

    @pl.when(kk == pl.num_programs(2) - 1)
    def _():
        o_ref[0] = (acc_ref[...] * scale).astype(BF16)


def _dft_pos(fa, fb, cs, ss):
    b, s, w = fa.shape
    t = min(1024, s)
    scale = 1.0 / math.sqrt(s * FNET_GROUP_DIM)
    return pl.pallas_call(
        functools.partial(_dft_pos_kernel, scale=scale),
        grid=(b, s // t, s // t),
        in_specs=[pl.BlockSpec((t, t), lambda i, m, k: (m, k)),
                  pl.BlockSpec((t, t), lambda i, m, k: (m, k)),
                  pl.BlockSpec((1, t, w), lambda i, m, k: (i, k, 0)),
                  pl.BlockSpec((1, t, w), lambda i, m, k: (i, k, 0))],
        out_specs=pl.BlockSpec((1, t, w), lambda i, m, k: (i, m, 0)),
        out_shape=jax.ShapeDtypeStruct((b, s, w), BF16),
        scratch_shapes=[pltpu.VMEM((t, w), F32)],
        compiler_params=_cparams("parallel", "parallel", "arbitrary"),
        name="dft_pos",
    )(cs, ss, fa, fb)


def _ab_out_kernel(a_ref, f_ref, w1_ref, w2_ref, x_ref, g_ref, o_ref):
    y = _bdot(a_ref[0], w1_ref[...]) + _bdot(f_ref[0], w2_ref[...])
    o_ref[0] = x_ref[0] + g_ref[0] * y


def _ab_out(a, fm, w_out, x, gate):
    b, s, d = x.shape
    tm = min(512, s)
    tok = lambda w: pl.BlockSpec((1, tm, w), lambda i, j: (i, j, 0))
    return pl.pallas_call(
        _ab_out_kernel,
        grid=(b, s // tm),
        in_specs=[tok(ATTN_WIDTH), tok(FNET_WIDTH),
                  pl.BlockSpec((ATTN_WIDTH, d), lambda i, j: (0, 0)),
                  pl.BlockSpec((FNET_WIDTH, d), lambda i, j: (0, 0)),
                  tok(d),
                  pl.BlockSpec((1, 1, d), lambda i, j: (i, 0, 0))],
        out_specs=tok(d),
        out_shape=jax.ShapeDtypeStruct((b, s, d), F32),
        compiler_params=_cparams("parallel", "parallel"),
        name="ab_out",
    )(a, fm, w_out[:ATTN_WIDTH], w_out[ATTN_WIDTH:], x, gate)


def _sgu_kernel(x_ref, nw_ref, sh_ref, sc_ref, g_ref, win_ref, gnw_ref, ws_ref, bs_ref, wout_ref, o_ref, uv_ref):
    x = x_ref[0]
    tm = x.shape[0]
    h = _norm_mod(x, nw_ref[...], sh_ref[0], sc_ref[0]).astype(BF16)
    z = jax.nn.gelu(_bdot(h, win_ref[...]))
    u = z[:, :SGU_WIDTH]
    v = z[:, SGU_WIDTH:]
    v = (v * lax.rsqrt(jnp.mean(v * v, axis=-1, keepdims=True) + RMS_EPS) * gnw_ref[...]).astype(BF16)
    for n in range(tm // SGU_CHUNK):
        r = slice(n * SGU_CHUNK, (n + 1) * SGU_CHUNK)
        for g in range(SGU_GROUPS):
            cs = slice(g * SGU_GROUP_DIM, (g + 1) * SGU_GROUP_DIM)
            sv = _bdot(ws_ref[g], v[r, cs]) + bs_ref[:, g:g + 1]
            uv_ref[r, cs] = (u[r, cs] * sv).astype(BF16)
    o_ref[0] = x + g_ref[0] * _bdot(uv_ref[...], wout_ref[...])


def _sgu(x, nw, shift, scale, gate, w_in, gnw, w_s, b_s_t, w_out):
    b, s, d = x.shape
    tm = min(256, s)
    tok = pl.BlockSpec((1, tm, d), lambda i, j: (i, j, 0))
    mod = pl.BlockSpec((1, 1, d), lambda i, j: (i, 0, 0))
    full = lambda a: pl.BlockSpec(a.shape, lambda i, j: (0,) * a.ndim)
    return pl.pallas_call(
        _sgu_kernel,
        grid=(b, s // tm),
        in_specs=[tok, full(nw), mod, mod, mod, full(w_in), full(gnw), full(w_s), full(b_s_t), full(w_out)],
        out_specs=tok,
        out_shape=jax.ShapeDtypeStruct((b, s, d), F32),
        scratch_shapes=[pltpu.VMEM((tm, SGU_WIDTH), BF16)],
        compiler_params=_cparams("parallel", "parallel"),
        name="sgu",
    )(x, nw, shift, scale, gate, w_in, gnw, w_s, b_s_t, w_out)


def _topk_rows(s, k):
    nrow = s.shape[0]
    rid = lax.broadcasted_iota(I32, s.shape, 0)
    vals, idxs = [], []
    for _ in range(k):
        m = jnp.max(s, axis=0, keepdims=True)
        am = jnp.min(jnp.where(s == m, rid, nrow), axis=0, keepdims=True)
        vals.append(m)
        idxs.append(am)
        s = jnp.where(rid == am, -jnp.inf, s)
    return jnp.concatenate(vals, axis=0), jnp.concatenate(idxs, axis=0)


def _select_rows(sel, table):
    out = jnp.zeros(sel.shape, table.dtype)
    for a in range(table.shape[0]):
        out = jnp.where(sel == a, table[a:a + 1], out)
    return out


def _peer_idx_kernel(x_ref, nw_ref, sh_ref, sc_ref, wq_ref, keys_ref, hq_ref, e_ref, g_ref, q_ref):
    hq = _norm_mod(x_ref[0], nw_ref[...], sh_ref[0], sc_ref[0])
    hq_ref[0] = hq
    q_ref[...] = _bdot(hq.astype(BF16), wq_ref[...]).astype(BF16)

    def head(h, carry):
        tops, topi = [], []
        for s in range(2):
            col = pl.multiple_of((h * 2 + s) * PEER_HALF_DIM, PEER_HALF_DIM)
            qs = q_ref[:, pl.ds(col, PEER_HALF_DIM)]
            st = lax.dot_general(keys_ref[h * 2 + s], qs, (((1,), (1,)), ((), ())), preferred_element_type=F32)
            ts, ti = _topk_rows(st, PEER_TOPK)
            tops.append(ts)
            topi.append(ti)
        cand = jnp.concatenate([tops[0][a:a + 1] + tops[1] for a in range(PEER_TOPK)], axis=0)
        best_s, best = _topk_rows(cand, PEER_TOPK)
        i1 = _select_rows(best // PEER_TOPK, topi[0])
        i2 = _select_rows(best % PEER_TOPK, topi[1])
        p = jnp.exp(best_s - best_s[0:1])
        row = pl.multiple_of(h * PEER_TOPK, PEER_TOPK)
        e_ref[0, 0, pl.ds(row, PEER_TOPK), :] = i1 * PEER_NKEYS + i2
        g_ref[0, 0, pl.ds(row, PEER_TOPK), :] = p / jnp.sum(p, axis=0, keepdims=True)
        return carry

    lax.fori_loop(0, PEER_HEADS, head, 0)


def _peer_idx(x, nw, shift, scale, w_q, keys):
    b, s, d = x.shape
    tm = min(256, s)
    nq = w_q.shape[1]
    sel = pl.BlockSpec((1, 1, PEER_SEL, tm), lambda i, j: (i, j, 0, 0))
    return pl.pallas_call(
        _peer_idx_kernel,
        grid=(b, s // tm),
        in_specs=[pl.BlockSpec((1, tm, d), lambda i, j: (i, j, 0)),
                  pl.BlockSpec((1, d), lambda i, j: (0, 0)),
                  pl.BlockSpec((1, 1, d), lambda i, j: (i, 0, 0)),
                  pl.BlockSpec((1, 1, d), lambda i, j: (i, 0, 0)),
                  pl.BlockSpec((d, nq), lambda i, j: (0, 0)),
                  pl.BlockSpec(keys.shape, lambda i, j: (0, 0, 0))],
        out_specs=[pl.BlockSpec((1, tm, d), lambda i, j: (i, j, 0)), sel, sel],
        out_shape=[jax.ShapeDtypeStruct((b, s, d), F32),
                   jax.ShapeDtypeStruct((b, s // tm, PEER_SEL, tm), I32),
                   jax.ShapeDtypeStruct((b, s // tm, PEER_SEL, tm), F32)],
        scratch_shapes=[pltpu.VMEM((tm, nq), BF16)],
        compiler_params=_cparams("parallel", "parallel"),
        name="peer_idx",
    )(x, nw, shift, scale, w_q, keys)


def _sc_gather2(tab_u, tab_v, idx):
    p = idx.shape[0]
    w = tab_u.shape[1]
    n = SC_GATHER_ROWS
    per_w = p // SC_WORKERS
    n_chunks = per_w // n
    mesh = plsc.VectorSubcoreMesh(core_axis_name="c", subcore_axis_name="s")
    rows = jax.ShapeDtypeStruct((p, w), U32)

    @functools.partial(
        pl.kernel, mesh=mesh, out_type=[rows, rows],
        scratch_types=[pltpu.VMEM((n,), I32), pltpu.VMEM((n, w), U32), pltpu.VMEM((n, w), U32),
                       pltpu.SemaphoreType.DMA, pltpu.SemaphoreType.DMA])
    def gather(u_hbm, v_hbm, idx_hbm, ou_hbm, ov_hbm, idx_v, ru, rv, sem_u, sem_v):
        wid = lax.axis_index("s") * SC_CORES + lax.axis_index("c")
        base = wid * per_w

        @pl.loop(0, n_chunks)
        def _(i):
            off = pl.multiple_of(base + i * n, n)
            pltpu.sync_copy(idx_hbm.at[pl.ds(off, n)], idx_v)
            cu = pltpu.async_copy(u_hbm.at[idx_v], ru, sem_u)
            cv = pltpu.async_copy(v_hbm.at[idx_v], rv, sem_v)
            cu.wait()
            cv.wait()
            pltpu.sync_copy(ru, ou_hbm.at[pl.ds(off, n)])
            pltpu.sync_copy(rv, ov_hbm.at[pl.ds(off, n)])

    return gather(tab_u, tab_v, idx)


def _unpack_pair(w32):
    lo = pltpu.bitcast(w32 << 16, F32)
    hi = pltpu.bitcast(w32 & jnp.uint32(0xFFFF0000), F32)
    return lo, hi


def _peer_out_kernel(ug_ref, vg_ref, hq_ref, g_ref, x_ref, gate_ref, o_ref):
    tb = PEER_TOK_BLOCK
    half = hq_ref.shape[1] // 2
    h_lo = hq_ref[:, :half]
    h_hi = hq_ref[:, half:]
    lane = lax.broadcasted_iota(I32, (tb, PEER_SEL), 1)

    def score(k, sc):
        lo, hi = _unpack_pair(ug_ref[pl.ds(pl.multiple_of(k * tb, tb), tb), :])
        col = jnp.sum(lo * h_lo + hi * h_hi, axis=1, keepdims=True)
        return jnp.where(lane == k, col, sc)

    sc = lax.fori_loop(0, PEER_SEL, score, jnp.zeros((tb, PEER_SEL), F32))
    wgt = g_ref[...] * jax.nn.gelu(sc)

    def accum(k, acc):
        lo, hi = _unpack_pair(vg_ref[pl.ds(pl.multiple_of(k * tb, tb), tb), :])
        wk = jnp.sum(jnp.where(lane == k, wgt, 0.0), axis=1, keepdims=True)
        return acc[0] + wk * lo, acc[1] + wk * hi

    z = jnp.zeros((tb, half), F32)
    a_lo, a_hi = lax.fori_loop(0, PEER_SEL, accum, (z, z))
    o_ref[...] = x_ref[...] + gate_ref[0] * jnp.concatenate([a_lo, a_hi], axis=1)


def _peer_out(ug, vg, hq, g, x, gate, tok0, seq):
    t, d = x.shape
    tb = PEER_TOK_BLOCK
    ntok = ug.shape[0] // PEER_SEL
    b0 = tok0 // tb
    per_seq = seq // tb
    rows = pl.BlockSpec((tb * PEER_SEL, ug.shape[1]), lambda i: (i, 0))
    tok = pl.BlockSpec((tb, d), lambda i: (i + b0, 0))
    return pl.pallas_call(
        _peer_out_kernel,
        grid=(ntok // tb,),
        in_specs=[rows, rows, tok,
                  pl.BlockSpec((tb, PEER_SEL), lambda i: (i + b0, 0)),
                  tok,
                  pl.BlockSpec((1, 1, d), lambda i: ((i + b0) // per_seq, 0, 0))],
        out_specs=tok,
        out_shape=jax.ShapeDtypeStruct((t, d), F32),
        input_output_aliases={4: 0},
        compiler_params=_cparams("parallel"),
        name="peer_out",
    )(ug, vg, hq, g, x, gate)


def _peer_ffn(x, nw, shift, scale, gate, w_q, keys, tab_u, tab_v, n_groups):
    b, s, d = x.shape
    hq, e_t, g_t = _peer_idx(x, nw, shift, scale, w_q, keys)
    nblk, tm = e_t.shape[1], e_t.shape[3]
    tb = PEER_TOK_BLOCK
    idx = e_t.reshape(b, nblk, PEER_SEL, tm // tb, tb).transpose(0, 1, 3, 2, 4).reshape(-1)
    g = g_t.transpose(0, 1, 3, 2).reshape(b * s, PEER_SEL)
    t = b * s
    x2 = x.reshape(t, d)
    hq2 = hq.reshape(t, d)
    gtok = t // n_groups
    for gi in range(n_groups):
        ug, vg = _sc_gather2(tab_u, tab_v, lax.slice(idx, (gi * gtok * PEER_SEL,), ((gi + 1) * gtok * PEER_SEL,)))
        x2 = _peer_out(ug, vg, hq2, g, x2, gate, gi * gtok, s)
    return x2.reshape(b, s, d)


def _rms_kernel(x_ref, w_ref, o_ref):
    x = x_ref[...]
    o_ref[...] = x * lax.rsqrt(jnp.mean(x * x, axis=-1, keepdims=True) + RMS_EPS) * w_ref[...]


def _final_norm(x, w):
    b, s, d = x.shape
    t = b * s
    tm = min(512, t)
    out = pl.pallas_call(
        _rms_kernel,
        grid=(t // tm,),
        in_specs=[pl.BlockSpec((tm, d), lambda i: (i, 0)), pl.BlockSpec((1, d), lambda i: (0, 0))],
        out_specs=pl.BlockSpec((tm, d), lambda i: (i, 0)),
        out_shape=jax.ShapeDtypeStruct((t, d), F32),
        compiler_params=_cparams("parallel"),
        name="final_norm",
    )(x.reshape(t, d), w.reshape(1, d))
    return out.reshape(b, s, d)


def _rope_tables(s):
    t = jnp.arange(s)
    row = (t // GRID_W).astype(F32)
    col = (t % GRID_W).astype(F32)
    half = HEAD_DIM // 2
    inv = ROPE_THETA ** (-jnp.arange(0, half, 2, dtype=F32) / half)
    ar = row[:, None] * inv
    ac = col[:, None] * inv
    ang = jnp.concatenate([ar, ar, ac, ac], axis=1)
    q = half // 2
    sign = jnp.concatenate([-jnp.ones(q), jnp.ones(q), -jnp.ones(q), jnp.ones(q)]).astype(F32)
    reps = QK_WIDTH // HEAD_DIM
    return jnp.tile(jnp.cos(ang), (1, reps)), jnp.tile(jnp.sin(ang) * sign, (1, reps))


def _rope_partner_cols():
    j = jnp.arange(QK_WIDTH)
    dd = j % HEAD_DIM
    q = HEAD_DIM // 4
    return j - dd + jnp.where((dd % (2 * q)) < q, dd + q, dd - q)


def _dft_tables(n, dtype):
    j = jnp.arange(n, dtype=I32)
    ang = ((j[:, None] * j[None, :]) % n).astype(F32) * (2.0 * math.pi / n)
    return jnp.cos(ang).astype(dtype), jnp.sin(ang).astype(dtype)


def _pack_table(tab):
    half = tab.shape[1] // 2
    bits = lax.bitcast_convert_type(tab.astype(BF16), jnp.uint16).astype(U32)
    return bits[:, :half] | (bits[:, half:] << 16)


def kernel(x, c, ctx, c_ctx, ada_w, ada_b, norm_mix_w, norm_ffn_w, ab_w_in, ab_w_out, attn_sink, gs_w_in, gs_norm_w,
           gs_w_s, gs_b_s, gs_w_out, peer_w_q, peer_keys, peer_u, peer_v, norm_out_w):
    bsz, seq, d = x.shape
    n_ctx = ctx.shape[1]
    depth = ada_w.shape[0]

    cond = jnp.concatenate([c, c_ctx[None], jnp.zeros((16 - bsz - 1, d), F32)], axis=0)
    mods = _ada_all(cond, ada_w, ada_b)

    cos, sin = _rope_tables(seq)
    partner = _rope_partner_cols()
    dft_cc, dft_cs = _dft_tables(FNET_GROUP_DIM, BF16)
    dft_c = jnp.concatenate([dft_cc, dft_cs], axis=1)
    pos_l = _dft_tables(seq, BF16)
    pos_c = _dft_tables(n_ctx, BF16)

    last_attn = ((depth - 1) // 2) * 2
    for i in range(depth):
        j = i // 2
        is_ab = i % 2 == 0
        upd_ctx = i < last_attn
        need_ctx = upd_ctx or is_ab
        m_l = [m[:, None, :] for m in jnp.split(mods[i, :bsz], N_MOD, axis=-1)]
        m_c = [jnp.broadcast_to(m[None], (bsz, 1, d)) for m in jnp.split(mods[i, bsz:bsz + 1], N_MOD, axis=-1)]
        nw_m = norm_mix_w[i].reshape(1, d)
        nw_f = norm_ffn_w[i].reshape(1, d)
        y_ctx = None
        if is_ab:
            w_in = ab_w_in[j]
            w_ext = jnp.concatenate([w_in, w_in[:, partner]], axis=1).astype(BF16)
            w_out = ab_w_out[j].astype(BF16)
            sink = attn_sink[j].reshape(1, N_Q_HEADS)
            q_l, k_l, v_l, fa_l, fb_l = _ab_in(x, nw_m, m_l[0], m_l[1], w_ext, cos, sin, dft_c, True)
            q_c, k_c, v_c, fa_c, fb_c = _ab_in(ctx, nw_m, m_c[0], m_c[1], w_ext, cos[:n_ctx], sin[:n_ctx], dft_c, False)
            a_l = _attn_local(q_l, k_l, v_l, k_c, v_c, sink)
            x = _ab_out(a_l, _dft_pos(fa_l, fb_l, *pos_l), w_out, x, m_l[2])
            if upd_ctx:
                a_c = _attn_ctx(q_c, k_c, v_c, sink)
                ctx = _ab_out(a_c, _dft_pos(fa_c, fb_c, *pos_c), w_out, ctx, m_c[2])
        else:
            sgu_w = (gs_w_in[j].astype(BF16), gs_norm_w[j].reshape(1, -1), gs_w_s[j].astype(BF16), gs_b_s[j].T,
                     gs_w_out[j].astype(BF16))
            x = _sgu(x, nw_m, m_l[0], m_l[1], m_l[2], *sgu_w)
            if upd_ctx:
                ctx = _sgu(ctx, nw_m, m_c[0], m_c[1], m_c[2], *sgu_w)
        w_q = peer_w_q[i].astype(BF16)
        keys = peer_keys[i].reshape(PEER_HEADS * 2, PEER_NKEYS, PEER_HALF_DIM).astype(BF16)
        tab_u = _pack_table(peer_u[i])
        tab_v = _pack_table(peer_v[i])
        x = _peer_ffn(x, nw_f, m_l[3], m_l[4], m_l[5], w_q, keys, tab_u, tab_v, 4)
        if upd_ctx:
            ctx = _peer_ffn(ctx, nw_f, m_c[3], m_c[4], m_c[5], w_q, keys, tab_u, tab_v, 1)
        del need_ctx, y_ctx
    return _final_norm(x, norm_out_w)
```

```python
import functools
import math

import jax
import jax.numpy as jnp
from jax import lax
from jax.experimental import pallas as pl
from jax.experimental.pallas import tpu as pltpu
from jax.experimental.pallas import tpu_sc as plsc

F32 = jnp.float32
BF16 = jnp.bfloat16
I32 = jnp.int32
U32 = jnp.uint32

GRID_W = 64
N_Q_HEADS = 8
N_KV_HEADS = 2
Q_PER_KV = N_Q_HEADS // N_KV_HEADS
HEAD_DIM = 64
WINDOW = 128
ROPE_THETA = 10000.0
ATTN_WIDTH = N_Q_HEADS * HEAD_DIM
KV_WIDTH = N_KV_HEADS * HEAD_DIM
QK_WIDTH = ATTN_WIDTH + KV_WIDTH
FNET_GROUPS = 4
FNET_GROUP_DIM = 128
FNET_WIDTH = FNET_GROUPS * FNET_GROUP_DIM
AB_IN_WIDTH = ATTN_WIDTH + 2 * KV_WIDTH + FNET_WIDTH
SGU_GROUPS = 8
SGU_GROUP_DIM = 128
SGU_WIDTH = SGU_GROUPS * SGU_GROUP_DIM
SGU_CHUNK = 128
PEER_HEADS = 8
PEER_NKEYS = 128
PEER_TOPK = 16
PEER_HALF_DIM = 128
PEER_SEL = PEER_HEADS * PEER_TOPK
N_MOD = 6
RMS_EPS = 1e-6
NEG_INF = -1e30

SC_CORES = 2
SC_SUBCORES = 16
SC_WORKERS = SC_CORES * SC_SUBCORES
SC_GATHER_ROWS = 64
PEER_TOK_BLOCK = 16
VMEM_LIMIT = 48 * 1024 * 1024


def _cparams(*sem):
    return pltpu.CompilerParams(dimension_semantics=sem, vmem_limit_bytes=VMEM_LIMIT)


def _norm_mod(x, nw, shift, scale):
    y = x * lax.rsqrt(jnp.mean(x * x, axis=-1, keepdims=True) + RMS_EPS) * nw
    return y * (1.0 + scale) + shift


def _bdot(a, b):
    return jnp.dot(a, b, preferred_element_type=F32)


def _ada_kernel(c_ref, w_ref, b_ref, o_ref):
    c = c_ref[...]
    s = (c * jax.nn.sigmoid(c)).astype(BF16)
    o_ref[0] = _bdot(s, w_ref[0].astype(BF16)) + b_ref[0]


def _ada_all(cond, ada_w, ada_b):
    depth, d, n = ada_w.shape
    r = cond.shape[0]
    tn = 1536
    return pl.pallas_call(
        _ada_kernel,
        grid=(depth, n // tn),
        in_specs=[pl.BlockSpec((r, d), lambda l, j: (0, 0)),
                  pl.BlockSpec((1, d, tn), lambda l, j: (l, 0, j)),
                  pl.BlockSpec((1, 1, tn), lambda l, j: (l, 0, j))],
        out_specs=pl.BlockSpec((1, r, tn), lambda l, j: (l, 0, j)),
        out_shape=jax.ShapeDtypeStruct((depth, r, n), F32),
        compiler_params=_cparams("parallel", "parallel"),
        name="ada",
    )(cond, ada_w, ada_b.reshape(depth, 1, n))


def _ab_in_kernel(x_ref, nw_ref, sh_ref, sc_ref, w_ref, cos_ref, sin_ref, dft_ref,
                  q_ref, k_ref, v_ref, fa_ref, fb_ref, *, rope):
    h = _norm_mod(x_ref[0], nw_ref[...], sh_ref[0], sc_ref[0]).astype(BF16)
    p = _bdot(h, w_ref[...])
    qk = p[:, :QK_WIDTH]
    if rope:
        qk = qk * cos_ref[...] + p[:, AB_IN_WIDTH:] * sin_ref[...]
    q_ref[0] = qk[:, :ATTN_WIDTH].astype(BF16)
    k_ref[0] = qk[:, ATTN_WIDTH:].astype(BF16)
    v_ref[0] = p[:, QK_WIDTH:QK_WIDTH + KV_WIDTH].astype(BF16)
    f0 = QK_WIDTH + KV_WIDTH
    for g in range(FNET_GROUPS):
        fg = p[:, f0 + g * FNET_GROUP_DIM:f0 + (g + 1) * FNET_GROUP_DIM].astype(BF16)
        ab = _bdot(fg, dft_ref[...])
        fa_ref[0, :, g * FNET_GROUP_DIM:(g + 1) * FNET_GROUP_DIM] = ab[:, :FNET_GROUP_DIM].astype(BF16)
        fb_ref[0, :, g * FNET_GROUP_DIM:(g + 1) * FNET_GROUP_DIM] = ab[:, FNET_GROUP_DIM:].astype(BF16)


def _ab_in(x, nw, shift, scale, w_ext, cos, sin, dft_c, rope):
    b, s, d = x.shape
    tm = min(512, s)
    n_ext = w_ext.shape[1]
    outs = [jax.ShapeDtypeStruct((b, s, ATTN_WIDTH), BF16), jax.ShapeDtypeStruct((b, s, KV_WIDTH), BF16),
            jax.ShapeDtypeStruct((b, s, KV_WIDTH), BF16), jax.ShapeDtypeStruct((b, s, FNET_WIDTH), BF16),
            jax.ShapeDtypeStruct((b, s, FNET_WIDTH), BF16)]
    tok = lambda w: pl.BlockSpec((1, tm, w), lambda i, j: (i, j, 0))
    return pl.pallas_call(
        functools.partial(_ab_in_kernel, rope=rope),
        grid=(b, s // tm),
        in_specs=[tok(d),
                  pl.BlockSpec((1, d), lambda i, j: (0, 0)),
                  pl.BlockSpec((1, 1, d), lambda i, j: (i, 0, 0)),
                  pl.BlockSpec((1, 1, d), lambda i, j: (i, 0, 0)),
                  pl.BlockSpec((d, n_ext), lambda i, j: (0, 0)),
                  pl.BlockSpec((tm, QK_WIDTH), lambda i, j: (j, 0)),
                  pl.BlockSpec((tm, QK_WIDTH), lambda i, j: (j, 0)),
                  pl.BlockSpec(dft_c.shape, lambda i, j: (0, 0))],
        out_specs=[tok(ATTN_WIDTH), tok(KV_WIDTH), tok(KV_WIDTH), tok(FNET_WIDTH), tok(FNET_WIDTH)],
        out_shape=outs,
        compiler_params=_cparams("parallel", "parallel"),
        name="ab_in",
    )(x, nw, shift, scale, w_ext, cos, sin, dft_c)


def _softmax_pv(qh, kk, vv, valid, sk, scale):
    s = lax.dot_general(qh, kk, (((1,), (1,)), ((), ())), preferred_element_type=F32) * scale
    if valid is not None:
        s = jnp.where(valid, s, NEG_INF)
    m = jnp.maximum(jnp.max(s, axis=-1, keepdims=True), sk)
    p = jnp.exp(s - m)
    inv_den = 1.0 / (jnp.sum(p, axis=-1, keepdims=True) + jnp.exp(sk - m))
    return _bdot((p * inv_den).astype(BF16), vv)


def _attn_local_kernel(q_ref, kp_ref, ko_ref, kn_ref, vp_ref, vo_ref, vn_ref, kc_ref, vc_ref, sink_ref, o_ref,
                       *, seq):
    j = pl.program_id(1)
    blk = q_ref.shape[1]
    n_ctx = kc_ref.shape[1]
    q = q_ref[0]
    kl = jnp.concatenate([kp_ref[0], ko_ref[0], kn_ref[0], kc_ref[0]], axis=0)
    vl = jnp.concatenate([vp_ref[0], vo_ref[0], vn_ref[0], vc_ref[0]], axis=0)
    nk = 3 * blk + n_ctx
    qi = lax.broadcasted_iota(I32, (blk, nk), 0)
    cj = lax.broadcasted_iota(I32, (blk, nk), 1)
    kj = cj - blk
    kpos = j * blk + kj
    valid = (cj >= 3 * blk) | ((jnp.abs(qi - kj) <= WINDOW) & (kpos >= 0) & (kpos < seq))
    scale = HEAD_DIM ** -0.5
    for kvh in range(N_KV_HEADS):
        kk = kl[:, kvh * HEAD_DIM:(kvh + 1) * HEAD_DIM]
        vv = vl[:, kvh * HEAD_DIM:(kvh + 1) * HEAD_DIM]
        for g in range(Q_PER_KV):
            h = kvh * Q_PER_KV + g
            o = _softmax_pv(q[:, h * HEAD_DIM:(h + 1) * HEAD_DIM], kk, vv, valid, sink_ref[0:1, h:h + 1], scale)
            o_ref[0, :, h * HEAD_DIM:(h + 1) * HEAD_DIM] = o.astype(BF16)


def _attn_local(q, k, v, kc, vc, sink):
    b, s, _ = q.shape
    c = kc.shape[1]
    blk = WINDOW
    nb = s // blk
    qspec = pl.BlockSpec((1, blk, ATTN_WIDTH), lambda i, j: (i, j, 0))
    prev = pl.BlockSpec((1, blk, KV_WIDTH), lambda i, j: (i, jnp.maximum(j - 1, 0), 0))
    own = pl.BlockSpec((1, blk, KV_WIDTH), lambda i, j: (i, j, 0))
    nxt = pl.BlockSpec((1, blk, KV_WIDTH), lambda i, j: (i, jnp.minimum(j + 1, nb - 1), 0))
    cspec = pl.BlockSpec((1, c, KV_WIDTH), lambda i, j: (i, 0, 0))
    return pl.pallas_call(
        functools.partial(_attn_local_kernel, seq=s),
        grid=(b, nb),
        in_specs=[qspec, prev, own, nxt, prev, own, nxt, cspec, cspec,
                  pl.BlockSpec((1, N_Q_HEADS), lambda i, j: (0, 0))],
        out_specs=qspec,
        out_shape=jax.ShapeDtypeStruct((b, s, ATTN_WIDTH), BF16),
        compiler_params=_cparams("parallel", "parallel"),
        name="attn_local",
    )(q, k, k, k, v, v, v, kc, vc, sink)


def _attn_ctx_kernel(q_ref, k_ref, v_ref, sink_ref, o_ref):
    q = q_ref[0]
    scale = HEAD_DIM ** -0.5
    for kvh in range(N_KV_HEADS):
        kk = k_ref[0][:, kvh * HEAD_DIM:(kvh + 1) * HEAD_DIM]
        vv = v_ref[0][:, kvh * HEAD_DIM:(kvh + 1) * HEAD_DIM]
        for g in range(Q_PER_KV):
            h = kvh * Q_PER_KV + g
            o = _softmax_pv(q[:, h * HEAD_DIM:(h + 1) * HEAD_DIM], kk, vv, None, sink_ref[0:1, h:h + 1], scale)
            o_ref[0, :, h * HEAD_DIM:(h + 1) * HEAD_DIM] = o.astype(BF16)


def _attn_ctx(q, k, v, sink):
    b, c, _ = q.shape
    return pl.pallas_call(
        _attn_ctx_kernel,
        grid=(b,),
        in_specs=[pl.BlockSpec((1, c, ATTN_WIDTH), lambda i: (i, 0, 0)),
                  pl.BlockSpec((1, c, KV_WIDTH), lambda i: (i, 0, 0)),
                  pl.BlockSpec((1, c, KV_WIDTH), lambda i: (i, 0, 0)),
                  pl.BlockSpec((1, N_Q_HEADS), lambda i: (0, 0))],
        out_specs=pl.BlockSpec((1, c, ATTN_WIDTH), lambda i: (i, 0, 0)),
        out_shape=jax.ShapeDtypeStruct((b, c, ATTN_WIDTH), BF16),
        compiler_params=_cparams("parallel"),
        name="attn_ctx",
    )(q, k, v, sink)


def _dft_pos_kernel(cs_ref, ss_ref, a_ref, b_ref, o_ref, acc_ref, *, scale):
    kk = pl.program_id(2)

    @pl.when(kk == 0)
    def _():
        acc_ref[...] = jnp.zeros_like(acc_ref)

    acc_ref[...] += _bdot(cs_ref[...], a_ref[0]) - _bdot(ss_ref[...], b_ref[0])

    @pl.when(kk == pl.num_programs(2) - 1)
    def _():
        o_ref[0] = (acc_ref[...] * scale).astype(BF16)


def _dft_pos(fa, fb, cs, ss):
    b, s, w = fa.shape
    t = min(1024, s)
    scale = 1.0 / math.sqrt(s * FNET_GROUP_DIM)
    return pl.pallas_call(
        functools.partial(_dft_pos_kernel, scale=scale),
        grid=(b, s // t, s // t),
        in_specs=[pl.BlockSpec((t, t), lambda i, m, k: (m, k)),
                  pl.BlockSpec((t, t), lambda i, m, k: (m, k)),
                  pl.BlockSpec((1, t, w), lambda i, m, k: (i, k, 0)),
                  pl.BlockSpec((1, t, w), lambda i, m, k: (i, k, 0))],
        out_specs=pl.BlockSpec((1, t, w), lambda i, m, k: (i, m, 0)),
        out_shape=jax.ShapeDtypeStruct((b, s, w), BF16),
        scratch_shapes=[pltpu.VMEM((t, w), F32)],
        compiler_params=_cparams("parallel", "parallel", "arbitrary"),
        name="dft_pos",
    )(cs, ss, fa, fb)


def _ab_out_kernel(a_ref, f_ref, w1_ref, w2_ref, x_ref, g_ref, o_ref):
    y = _bdot(a_ref[0], w1_ref[...]) + _bdot(f_ref[0], w2_ref[...])
    o_ref[0] = x_ref[0] + g_ref[0] * y


def _ab_out(a, fm, w_out, x, gate):
    b, s, d = x.shape
    tm = min(512, s)
    tok = lambda w: pl.BlockSpec((1, tm, w), lambda i, j: (i, j, 0))
    return pl.pallas_call(
        _ab_out_kernel,
        grid=(b, s // tm),
        in_specs=[tok(ATTN_WIDTH), tok(FNET_WIDTH),
                  pl.BlockSpec((ATTN_WIDTH, d), lambda i, j: (0, 0)),
                  pl.BlockSpec((FNET_WIDTH, d), lambda i, j: (0, 0)),
                  tok(d),
                  pl.BlockSpec((1, 1, d), lambda i, j: (i, 0, 0))],
        out_specs=tok(d),
        out_shape=jax.ShapeDtypeStruct((b, s, d), F32),
        compiler_params=_cparams("parallel", "parallel"),
        name="ab_out",
    )(a, fm, w_out[:ATTN_WIDTH], w_out[ATTN_WIDTH:], x, gate)


def _sgu_kernel(x_ref, nw_ref, sh_ref, sc_ref, g_ref, win_ref, gnw_ref, ws_ref, bs_ref, wout_ref, o_ref, uv_ref):
    x = x_ref[0]
    tm = x.shape[0]
    h = _norm_mod(x, nw_ref[...], sh_ref[0], sc_ref[0]).astype(BF16)
    z = jax.nn.gelu(_bdot(h, win_ref[...]))
    u = z[:, :SGU_WIDTH]
    v = z[:, SGU_WIDTH:]
    v = (v * lax.rsqrt(jnp.mean(v * v, axis=-1, keepdims=True) + RMS_EPS) * gnw_ref[...]).astype(BF16)
    for n in range(tm // SGU_CHUNK):
        r = slice(n * SGU_CHUNK, (n + 1) * SGU_CHUNK)
        for g in range(SGU_GROUPS):
            cs = slice(g * SGU_GROUP_DIM, (g + 1) * SGU_GROUP_DIM)
            sv = _bdot(ws_ref[g], v[r, cs]) + bs_ref[:, g:g + 1]
            uv_ref[r, cs] = (u[r, cs] * sv).astype(BF16)
    o_ref[0] = x + g_ref[0] * _bdot(uv_ref[...], wout_ref[...])


def _sgu(x, nw, shift, scale, gate, w_in, gnw, w_s, b_s_t, w_out):
    b, s, d = x.shape
    tm = min(256, s)
    tok = pl.BlockSpec((1, tm, d), lambda i, j: (i, j, 0))
    mod = pl.BlockSpec((1, 1, d), lambda i, j: (i, 0, 0))
    full = lambda a: pl.BlockSpec(a.shape, lambda i, j: (0,) * a.ndim)
    return pl.pallas_call(
        _sgu_kernel,
        grid=(b, s // tm),
        in_specs=[tok, full(nw), mod, mod, mod, full(w_in), full(gnw), full(w_s), full(b_s_t), full(w_out)],
        out_specs=tok,
        out_shape=jax.ShapeDtypeStruct((b, s, d), F32),
        scratch_shapes=[pltpu.VMEM((tm, SGU_WIDTH), BF16)],
        compiler_params=_cparams("parallel", "parallel"),
        name="sgu",
    )(x, nw, shift, scale, gate, w_in, gnw, w_s, b_s_t, w_out)


def _topk_rows(s, k):
    nrow = s.shape[0]
    rid = lax.broadcasted_iota(I32, s.shape, 0)
    vals, idxs = [], []
    for _ in range(k):
        m = jnp.max(s, axis=0, keepdims=True)
        am = jnp.min(jnp.where(s == m, rid, nrow), axis=0, keepdims=True)
        vals.append(m)
        idxs.append(am)
        s = jnp.where(rid == am, -jnp.inf, s)
    return jnp.concatenate(vals, axis=0), jnp.concatenate(idxs, axis=0)


def _select_rows(sel, table):
    out = jnp.zeros(sel.shape, table.dtype)
    for a in range(table.shape[0]):
        out = jnp.where(sel == a, table[a:a + 1], out)
    return out


def _peer_idx_kernel(x_ref, nw_ref, sh_ref, sc_ref, wq_ref, keys_ref, hq_ref, e_ref, g_ref, q_ref):
    hq = _norm_mod(x_ref[0], nw_ref[...], sh_ref[0], sc_ref[0])
    hq_ref[0] = hq
    q_ref[...] = _bdot(hq.astype(BF16), wq_ref[...]).astype(BF16)

    def head(h, carry):
        tops, topi = [], []
        for s in range(2):
            col = pl.multiple_of((h * 2 + s) * PEER_HALF_DIM, PEER_HALF_DIM)
            qs = q_ref[:, pl.ds(col, PEER_HALF_DIM)]
            st = lax.dot_general(keys_ref[h * 2 + s], qs, (((1,), (1,)), ((), ())), preferred_element_type=F32)
            ts, ti = _topk_rows(st, PEER_TOPK)
            tops.append(ts)
            topi.append(ti)
        cand = jnp.concatenate([tops[0][a:a + 1] + tops[1] for a in range(PEER_TOPK)], axis=0)
        best_s, best = _topk_rows(cand, PEER_TOPK)
        i1 = _select_rows(best // PEER_TOPK, topi[0])
        i2 = _select_rows(best % PEER_TOPK, topi[1])
        p = jnp.exp(best_s - best_s[0:1])
        row = pl.multiple_of(h * PEER_TOPK, PEER_TOPK)
        e_ref[0, 0, pl.ds(row, PEER_TOPK), :] = i1 * PEER_NKEYS + i2
        g_ref[0, 0, pl.ds(row, PEER_TOPK), :] = p / jnp.sum(p, axis=0, keepdims=True)
        return carry

    lax.fori_loop(0, PEER_HEADS, head, 0)


def _peer_idx(x, nw, shift, scale, w_q, keys):
    b, s, d = x.shape
    tm = min(256, s)
    nq = w_q.shape[1]
    sel = pl.BlockSpec((1, 1, PEER_SEL, tm), lambda i, j: (i, j, 0, 0))
    return pl.pallas_call(
        _peer_idx_kernel,
        grid=(b, s // tm),
        in_specs=[pl.BlockSpec((1, tm, d), lambda i, j: (i, j, 0)),
                  pl.BlockSpec((1, d), lambda i, j: (0, 0)),
                  pl.BlockSpec((1, 1, d), lambda i, j: (i, 0, 0)),
                  pl.BlockSpec((1, 1, d), lambda i, j: (i, 0, 0)),
                  pl.BlockSpec((d, nq), lambda i, j: (0, 0)),
                  pl.BlockSpec(keys.shape, lambda i, j: (0, 0, 0))],
        out_specs=[pl.BlockSpec((1, tm, d), lambda i, j: (i, j, 0)), sel, sel],
        out_shape=[jax.ShapeDtypeStruct((b, s, d), F32),
                   jax.ShapeDtypeStruct((b, s // tm, PEER_SEL, tm), I32),
                   jax.ShapeDtypeStruct((b, s // tm, PEER_SEL, tm), F32)],
        scratch_shapes=[pltpu.VMEM((tm, nq), BF16)],
        compiler_params=_cparams("parallel", "parallel"),
        name="peer_idx",
    )(x, nw, shift, scale, w_q, keys)


def _sc_gather2(tab_u, tab_v, idx):
    p = idx.shape[0]
    w = tab_u.shape[1]
    n = SC_GATHER_ROWS
    per_w = p // SC_WORKERS
    n_chunks = per_w // n
    mesh = plsc.VectorSubcoreMesh(core_axis_name="c", subcore_axis_name="s")
    rows = jax.ShapeDtypeStruct((p, w), U32)

    @functools.partial(
        pl.kernel, mesh=mesh, out_type=[rows, rows],
        scratch_types=[pltpu.VMEM((n,), I32), pltpu.VMEM((n, w), U32), pltpu.VMEM((n, w), U32),
                       pltpu.SemaphoreType.DMA, pltpu.SemaphoreType.DMA])
    def gather(u_hbm, v_hbm, idx_hbm, ou_hbm, ov_hbm, idx_v, ru, rv, sem_u, sem_v):
        wid = lax.axis_index("s") * SC_CORES + lax.axis_index("c")
        base = wid * per_w

        @pl.loop(0, n_chunks)
        def _(i):
            off = pl.multiple_of(base + i * n, n)
            pltpu.sync_copy(idx_hbm.at[pl.ds(off, n)], idx_v)
            cu = pltpu.async_copy(u_hbm.at[idx_v], ru, sem_u)
            cv = pltpu.async_copy(v_hbm.at[idx_v], rv, sem_v)
            cu.wait()
            cv.wait()
            pltpu.sync_copy(ru, ou_hbm.at[pl.ds(off, n)])
            pltpu.sync_copy(rv, ov_hbm.at[pl.ds(off, n)])

    return gather(tab_u, tab_v, idx)


def _unpack_pair(w32):
    lo = pltpu.bitcast(w32 << 16, F32)
    hi = pltpu.bitcast(w32 & jnp.uint32(0xFFFF0000), F32)
    return lo, hi


def _rowsum_all_lanes(x):
    hi = x.astype(BF16)
    r1 = x - hi.astype(F32)
    mid = r1.astype(BF16)
    lo = (r1 - mid.astype(F32)).astype(BF16)
    pieces = jnp.concatenate([hi, mid, lo], axis=1)
    return _bdot(pieces, jnp.ones((pieces.shape[1], x.shape[1]), BF16))


def _peer_out_kernel(ug_ref, vg_ref, hq_ref, g_ref, x_ref, gate_ref, o_ref, part_ref, wb_ref):
    tb = PEER_TOK_BLOCK
    half = hq_ref.shape[1] // 2
    nl = 128
    h_lo = hq_ref[:, :half]
    h_hi = hq_ref[:, half:]

    def score(k, carry):
        lo, hi = _unpack_pair(ug_ref[pl.ds(pl.multiple_of(k * tb, tb), tb), :])
        p = lo * h_lo + hi * h_hi
        part_ref[k] = sum(p[:, i * nl:(i + 1) * nl] for i in range(half // nl))
        return carry

    lax.fori_loop(0, PEER_SEL, score, 0, unroll=8)
    scb = _rowsum_all_lanes(part_ref[...].reshape(PEER_SEL * tb, nl)).reshape(PEER_SEL, tb, nl)
    diag = (lax.broadcasted_iota(I32, (PEER_SEL, tb, PEER_SEL), 0)
            == lax.broadcasted_iota(I32, (PEER_SEL, tb, PEER_SEL), 2))
    sc = jnp.sum(jnp.where(diag, scb, 0.0), axis=0)
    wgt = g_ref[...] * jax.nn.gelu(sc)
    z = jnp.where(diag, jnp.broadcast_to(wgt[None], (PEER_SEL, tb, PEER_SEL)), 0.0)
    wb_ref[...] = _rowsum_all_lanes(z.reshape(PEER_SEL * tb, PEER_SEL))

    def accum(k, acc):
        r = pl.ds(pl.multiple_of(k * tb, tb), tb)
        lo, hi = _unpack_pair(vg_ref[r, :])
        wk = wb_ref[r, :]
        wk = jnp.concatenate([wk] * (half // nl), axis=1)
        return acc[0] + wk * lo, acc[1] + wk * hi

    z0 = jnp.zeros((tb, half), F32)
    a_lo, a_hi = lax.fori_loop(0, PEER_SEL, accum, (z0, z0), unroll=8)
    o_ref[...] = x_ref[...] + gate_ref[0] * jnp.concatenate([a_lo, a_hi], axis=1)


def _peer_out(ug, vg, hq, g, x, gate, tok0, seq):
    t, d = x.shape
    tb = PEER_TOK_BLOCK
    ntok = ug.shape[0] // PEER_SEL
    b0 = tok0 // tb
    per_seq = seq // tb
    rows = pl.BlockSpec((tb * PEER_SEL, ug.shape[1]), lambda i: (i, 0))
    tok = pl.BlockSpec((tb, d), lambda i: (i + b0, 0))
    return pl.pallas_call(
        _peer_out_kernel,
        grid=(ntok // tb,),
        in_specs=[rows, rows, tok,
                  pl.BlockSpec((tb, PEER_SEL), lambda i: (i + b0, 0)),
                  tok,
                  pl.BlockSpec((1, 1, d), lambda i: ((i + b0) // per_seq, 0, 0))],
        out_specs=tok,
        out_shape=jax.ShapeDtypeStruct((t, d), F32),
        input_output_aliases={4: 0},
        scratch_shapes=[pltpu.VMEM((PEER_SEL, tb, 128), F32), pltpu.VMEM((PEER_SEL * tb, 128), F32)],
        compiler_params=_cparams("parallel"),
        name="peer_out",
    )(ug, vg, hq, g, x, gate)


def _peer_ffn(x, nw, shift, scale, gate, w_q, keys, tab_u, tab_v, n_groups):
    b, s, d = x.shape
    hq, e_t, g_t = _peer_idx(x, nw, shift, scale, w_q, keys)
    nblk, tm = e_t.shape[1], e_t.shape[3]
    tb = PEER_TOK_BLOCK
    idx = e_t.reshape(b, nblk, PEER_SEL, tm // tb, tb).transpose(0, 1, 3, 2, 4).reshape(-1)
    g = g_t.transpose(0, 1, 3, 2).reshape(b * s, PEER_SEL)
    t = b * s
    x2 = x.reshape(t, d)
    hq2 = hq.reshape(t, d)
    gtok = t // n_groups
    for gi in range(n_groups):
        ug, vg = _sc_gather2(tab_u, tab_v, lax.slice(idx, (gi * gtok * PEER_SEL,), ((gi + 1) * gtok * PEER_SEL,)))
        x2 = _peer_out(ug, vg, hq2, g, x2, gate, gi * gtok, s)
    return x2.reshape(b, s, d)


def _rms_kernel(x_ref, w_ref, o_ref):
    x = x_ref[...]
    o_ref[...] = x * lax.rsqrt(jnp.mean(x * x, axis=-1, keepdims=True) + RMS_EPS) * w_ref[...]


def _final_norm(x, w):
    b, s, d = x.shape
    t = b * s
    tm = min(512, t)
    out = pl.pallas_call(
        _rms_kernel,
        grid=(t // tm,),
        in_specs=[pl.BlockSpec((tm, d), lambda i: (i, 0)), pl.BlockSpec((1, d), lambda i: (0, 0))],
        out_specs=pl.BlockSpec((tm, d), lambda i: (i, 0)),
        out_shape=jax.ShapeDtypeStruct((t, d), F32),
        compiler_params=_cparams("parallel"),
        name="final_norm",
    )(x.reshape(t, d), w.reshape(1, d))
    return out.reshape(b, s, d)


def _rope_tables(s):
    t = jnp.arange(s)
    row = (t // GRID_W).astype(F32)
    col = (t % GRID_W).astype(F32)
    half = HEAD_DIM // 2
    inv = ROPE_THETA ** (-jnp.arange(0, half, 2, dtype=F32) / half)
    ar = row[:, None] * inv
    ac = col[:, None] * inv
    ang = jnp.concatenate([ar, ar, ac, ac], axis=1)
    q = half // 2
    sign = jnp.concatenate([-jnp.ones(q), jnp.ones(q), -jnp.ones(q), jnp.ones(q)]).astype(F32)
    reps = QK_WIDTH // HEAD_DIM
    return jnp.tile(jnp.cos(ang), (1, reps)), jnp.tile(jnp.sin(ang) * sign, (1, reps))


def _rope_partner_cols():
    j = jnp.arange(QK_WIDTH)
    dd = j % HEAD_DIM
    q = HEAD_DIM // 4
    return j - dd + jnp.where((dd % (2 * q)) < q, dd + q, dd - q)


def _dft_tables(n, dtype):
    j = jnp.arange(n, dtype=I32)
    ang = ((j[:, None] * j[None, :]) % n).astype(F32) * (2.0 * math.pi / n)
    return jnp.cos(ang).astype(dtype), jnp.sin(ang).astype(dtype)


def _pack_table(tab):
    half = tab.shape[1] // 2
    bits = lax.bitcast_convert_type(tab.astype(BF16), jnp.uint16).astype(U32)
    return bits[:, :half] | (bits[:, half:] << 16)


def kernel(x, c, ctx, c_ctx, ada_w, ada_b, norm_mix_w, norm_ffn_w, ab_w_in, ab_w_out, attn_sink, gs_w_in, gs_norm_w,
           gs_w_s, gs_b_s, gs_w_out, peer_w_q, peer_keys, peer_u, peer_v, norm_out_w):
    bsz, seq, d = x.shape
    n_ctx = ctx.shape[1]
    depth = ada_w.shape[0]

    cond = jnp.concatenate([c, c_ctx[None], jnp.zeros((16 - bsz - 1, d), F32)], axis=0)
    mods = _ada_all(cond, ada_w, ada_b)

    cos, sin = _rope_tables(seq)
    partner = _rope_partner_cols()
    dft_cc, dft_cs = _dft_tables(FNET_GROUP_DIM, BF16)
    dft_c = jnp.concatenate([dft_cc, dft_cs], axis=1)
    pos_l = _dft_tables(seq, BF16)
    pos_c = _dft_tables(n_ctx, BF16)

    last_attn = ((depth - 1) // 2) * 2
    for i in range(depth):
        j = i // 2
        is_ab = i % 2 == 0
        upd_ctx = i < last_attn
        need_ctx = upd_ctx or is_ab
        m_l = [m[:, None, :] for m in jnp.split(mods[i, :bsz], N_MOD, axis=-1)]
        m_c = [jnp.broadcast_to(m[None], (bsz, 1, d)) for m in jnp.split(mods[i, bsz:bsz + 1], N_MOD, axis=-1)]
        nw_m = norm_mix_w[i].reshape(1, d)
        nw_f = norm_ffn_w[i].reshape(1, d)
        y_ctx = None
        if is_ab:
            w_in = ab_w_in[j]
            w_ext = jnp.concatenate([w_in, w_in[:, partner]], axis=1).astype(BF16)
            w_out = ab_w_out[j].astype(BF16)
            sink = attn_sink[j].reshape(1, N_Q_HEADS)
            q_l, k_l, v_l, fa_l, fb_l = _ab_in(x, nw_m, m_l[0], m_l[1], w_ext, cos, sin, dft_c, True)
            q_c, k_c, v_c, fa_c, fb_c = _ab_in(ctx, nw_m, m_c[0], m_c[1], w_ext, cos[:n_ctx], sin[:n_ctx], dft_c, False)
            a_l = _attn_local(q_l, k_l, v_l, k_c, v_c, sink)
            x = _ab_out(a_l, _dft_pos(fa_l, fb_l, *pos_l), w_out, x, m_l[2])
            if upd_ctx:
                a_c = _attn_ctx(q_c, k_c, v_c, sink)
                ctx = _ab_out(a_c, _dft_pos(fa_c, fb_c, *pos_c), w_out, ctx, m_c[2])
        else:
            sgu_w = (gs_w_in[j].astype(BF16), gs_norm_w[j].reshape(1, -1), gs_w_s[j].astype(BF16), gs_b_s[j].T,
                     gs_w_out[j].astype(BF16))
            x = _sgu(x, nw_m, m_l[0], m_l[1], m_l[2], *sgu_w)
            if upd_ctx:
                ctx = _sgu(ctx, nw_m, m_c[0], m_c[1], m_c[2], *sgu_w)
        w_q = peer_w_q[i].astype(BF16)
        keys = peer_keys[i].reshape(PEER_HEADS * 2, PEER_NKEYS, PEER_HALF_DIM).astype(BF16)
        tab_u = _pack_table(peer_u[i])
        tab_v = _pack_table(peer_v[i])
        x = _peer_ffn(x, nw_f, m_l[3], m_l[4], m_l[5], w_q, keys, tab_u, tab_v, 4)
        if upd_ctx:
            ctx = _peer_ffn(ctx, nw_f, m_c[3], m_c[4], m_c[5], w_q, keys, tab_u, tab_v, 1)
        del need_ctx, y_ctx
    return _final_norm(x, norm_out_w)
```

```python
import functools
import math

import jax
import jax.numpy as jnp
from jax import lax
from jax.experimental import pallas as pl
from jax.experimental.pallas import tpu as pltpu
from jax.experimental.pallas import tpu_sc as plsc

F32 = jnp.float32
BF16 = jnp.bfloat16
I32 = jnp.int32
U32 = jnp.uint32

GRID_W = 64
N_Q_HEADS = 8
N_KV_HEADS = 2
Q_PER_KV = N_Q_HEADS // N_KV_HEADS
HEAD_DIM = 64
WINDOW = 128
ROPE_THETA = 10000.0
ATTN_WIDTH = N_Q_HEADS * HEAD_DIM
KV_WIDTH = N_KV_HEADS * HEAD_DIM
QK_WIDTH = ATTN_WIDTH + KV_WIDTH
FNET_GROUPS = 4
FNET_GROUP_DIM = 128
FNET_WIDTH = FNET_GROUPS * FNET_GROUP_DIM
AB_IN_WIDTH = ATTN_WIDTH + 2 * KV_WIDTH + FNET_WIDTH
SGU_GROUPS = 8
SGU_GROUP_DIM = 128
SGU_WIDTH = SGU_GROUPS * SGU_GROUP_DIM
SGU_CHUNK = 128
PEER_HEADS = 8
PEER_NKEYS = 128
PEER_TOPK = 16
PEER_HALF_DIM = 128
PEER_SEL = PEER_HEADS * PEER_TOPK
N_MOD = 6
RMS_EPS = 1e-6
NEG_INF = -1e30

SC_CORES = 2
SC_SUBCORES = 16
SC_WORKERS = SC_CORES * SC_SUBCORES
SC_LANES = 16
SC_GATHER_ROWS = 64
PEER_TOK_BLOCK = 16
VMEM_LIMIT = 48 * 1024 * 1024


def _cparams(*sem):
    return pltpu.CompilerParams(dimension_semantics=sem, vmem_limit_bytes=VMEM_LIMIT)


def _norm_mod(x, nw, shift, scale):
    y = x * lax.rsqrt(jnp.mean(x * x, axis=-1, keepdims=True) + RMS_EPS) * nw
    return y * (1.0 + scale) + shift


def _bdot(a, b):
    return jnp.dot(a, b, preferred_element_type=F32)


def _ada_kernel(c_ref, w_ref, b_ref, o_ref):
    c = c_ref[...]
    s = (c * jax.nn.sigmoid(c)).astype(BF16)
    o_ref[0] = _bdot(s, w_ref[0].astype(BF16)) + b_ref[0]


def _ada_all(cond, ada_w, ada_b):
    depth, d, n = ada_w.shape
    r = cond.shape[0]
    tn = 1536
    return pl.pallas_call(
        _ada_kernel,
        grid=(depth, n // tn),
        in_specs=[pl.BlockSpec((r, d), lambda l, j: (0, 0)),
                  pl.BlockSpec((1, d, tn), lambda l, j: (l, 0, j)),
                  pl.BlockSpec((1, 1, tn), lambda l, j: (l, 0, j))],
        out_specs=pl.BlockSpec((1, r, tn), lambda l, j: (l, 0, j)),
        out_shape=jax.ShapeDtypeStruct((depth, r, n), F32),
        compiler_params=_cparams("parallel", "parallel"),
        name="ada",
    )(cond, ada_w, ada_b.reshape(depth, 1, n))


def _ab_in_kernel(x_ref, nw_ref, sh_ref, sc_ref, w_ref, cos_ref, sin_ref, dft_ref,
                  q_ref, k_ref, v_ref, fa_ref, fb_ref, *, rope):
    h = _norm_mod(x_ref[0], nw_ref[...], sh_ref[0], sc_ref[0]).astype(BF16)
    p = _bdot(h, w_ref[...])
    qk = p[:, :QK_WIDTH]
    if rope:
        qk = qk * cos_ref[...] + p[:, AB_IN_WIDTH:] * sin_ref[...]
    q_ref[0] = qk[:, :ATTN_WIDTH].astype(BF16)
    k_ref[0] = qk[:, ATTN_WIDTH:].astype(BF16)
    v_ref[0] = p[:, QK_WIDTH:QK_WIDTH + KV_WIDTH].astype(BF16)
    f0 = QK_WIDTH + KV_WIDTH
    for g in range(FNET_GROUPS):
        fg = p[:, f0 + g * FNET_GROUP_DIM:f0 + (g + 1) * FNET_GROUP_DIM].astype(BF16)
        ab = _bdot(fg, dft_ref[...])
        fa_ref[0, :, g * FNET_GROUP_DIM:(g + 1) * FNET_GROUP_DIM] = ab[:, :FNET_GROUP_DIM].astype(BF16)
        fb_ref[0, :, g * FNET_GROUP_DIM:(g + 1) * FNET_GROUP_DIM] = ab[:, FNET_GROUP_DIM:].astype(BF16)


def _ab_in(x, nw, shift, scale, w_ext, cos, sin, dft_c, rope):
    b, s, d = x.shape
    tm = min(512, s)
    n_ext = w_ext.shape[1]
    outs = [jax.ShapeDtypeStruct((b, s, ATTN_WIDTH), BF16), jax.ShapeDtypeStruct((b, s, KV_WIDTH), BF16),
            jax.ShapeDtypeStruct((b, s, KV_WIDTH), BF16), jax.ShapeDtypeStruct((b, s, FNET_WIDTH), BF16),
            jax.ShapeDtypeStruct((b, s, FNET_WIDTH), BF16)]
    tok = lambda w: pl.BlockSpec((1, tm, w), lambda i, j: (i, j, 0))
    return pl.pallas_call(
        functools.partial(_ab_in_kernel, rope=rope),
        grid=(b, s // tm),
        in_specs=[tok(d),
                  pl.BlockSpec((1, d), lambda i, j: (0, 0)),
                  pl.BlockSpec((1, 1, d), lambda i, j: (i, 0, 0)),
                  pl.BlockSpec((1, 1, d), lambda i, j: (i, 0, 0)),
                  pl.BlockSpec((d, n_ext), lambda i, j: (0, 0)),
                  pl.BlockSpec((tm, QK_WIDTH), lambda i, j: (j, 0)),
                  pl.BlockSpec((tm, QK_WIDTH), lambda i, j: (j, 0)),
                  pl.BlockSpec(dft_c.shape, lambda i, j: (0, 0))],
        out_specs=[tok(ATTN_WIDTH), tok(KV_WIDTH), tok(KV_WIDTH), tok(FNET_WIDTH), tok(FNET_WIDTH)],
        out_shape=outs,
        compiler_params=_cparams("parallel", "parallel"),
        name="ab_in",
    )(x, nw, shift, scale, w_ext, cos, sin, dft_c)


def _softmax_pv(qh, kk, vv, valid, sk, scale):
    s = lax.dot_general(qh, kk, (((1,), (1,)), ((), ())), preferred_element_type=F32) * scale
    if valid is not None:
        s = jnp.where(valid, s, NEG_INF)
    m = jnp.maximum(jnp.max(s, axis=-1, keepdims=True), sk)
    p = jnp.exp(s - m)
    inv_den = 1.0 / (jnp.sum(p, axis=-1, keepdims=True) + jnp.exp(sk - m))
    return _bdot((p * inv_den).astype(BF16), vv)


def _attn_local_kernel(q_ref, kp_ref, ko_ref, kn_ref, vp_ref, vo_ref, vn_ref, kc_ref, vc_ref, sink_ref, o_ref,
                       *, seq):
    j = pl.program_id(1)
    blk = q_ref.shape[1]
    n_ctx = kc_ref.shape[1]
    q = q_ref[0]
    kl = jnp.concatenate([kp_ref[0], ko_ref[0], kn_ref[0], kc_ref[0]], axis=0)
    vl = jnp.concatenate([vp_ref[0], vo_ref[0], vn_ref[0], vc_ref[0]], axis=0)
    nk = 3 * blk + n_ctx
    qi = lax.broadcasted_iota(I32, (blk, nk), 0)
    cj = lax.broadcasted_iota(I32, (blk, nk), 1)
    kj = cj - blk
    kpos = j * blk + kj
    valid = (cj >= 3 * blk) | ((jnp.abs(qi - kj) <= WINDOW) & (kpos >= 0) & (kpos < seq))
    scale = HEAD_DIM ** -0.5
    for kvh in range(N_KV_HEADS):
        kk = kl[:, kvh * HEAD_DIM:(kvh + 1) * HEAD_DIM]
        vv = vl[:, kvh * HEAD_DIM:(kvh + 1) * HEAD_DIM]
        for g in range(Q_PER_KV):
            h = kvh * Q_PER_KV + g
            o = _softmax_pv(q[:, h * HEAD_DIM:(h + 1) * HEAD_DIM], kk, vv, valid, sink_ref[0:1, h:h + 1], scale)
            o_ref[0, :, h * HEAD_DIM:(h + 1) * HEAD_DIM] = o.astype(BF16)


def _attn_local(q, k, v, kc, vc, sink):
    b, s, _ = q.shape
    c = kc.shape[1]
    blk = WINDOW
    nb = s // blk
    qspec = pl.BlockSpec((1, blk, ATTN_WIDTH), lambda i, j: (i, j, 0))
    prev = pl.BlockSpec((1, blk, KV_WIDTH), lambda i, j: (i, jnp.maximum(j - 1, 0), 0))
    own = pl.BlockSpec((1, blk, KV_WIDTH), lambda i, j: (i, j, 0))
    nxt = pl.BlockSpec((1, blk, KV_WIDTH), lambda i, j: (i, jnp.minimum(j + 1, nb - 1), 0))
    cspec = pl.BlockSpec((1, c, KV_WIDTH), lambda i, j: (i, 0, 0))
    return pl.pallas_call(
        functools.partial(_attn_local_kernel, seq=s),
        grid=(b, nb),
        in_specs=[qspec, prev, own, nxt, prev, own, nxt, cspec, cspec,
                  pl.BlockSpec((1, N_Q_HEADS), lambda i, j: (0, 0))],
        out_specs=qspec,
        out_shape=jax.ShapeDtypeStruct((b, s, ATTN_WIDTH), BF16),
        compiler_params=_cparams("parallel", "parallel"),
        name="attn_local",
    )(q, k, k, k, v, v, v, kc, vc, sink)


def _attn_ctx_kernel(q_ref, k_ref, v_ref, sink_ref, o_ref):
    q = q_ref[0]
    scale = HEAD_DIM ** -0.5
    for kvh in range(N_KV_HEADS):
        kk = k_ref[0][:, kvh * HEAD_DIM:(kvh + 1) * HEAD_DIM]
        vv = v_ref[0][:, kvh * HEAD_DIM:(kvh + 1) * HEAD_DIM]
        for g in range(Q_PER_KV):
            h = kvh * Q_PER_KV + g
            o = _softmax_pv(q[:, h * HEAD_DIM:(h + 1) * HEAD_DIM], kk, vv, None, sink_ref[0:1, h:h + 1], scale)
            o_ref[0, :, h * HEAD_DIM:(h + 1) * HEAD_DIM] = o.astype(BF16)


def _attn_ctx(q, k, v, sink):
    b, c, _ = q.shape
    return pl.pallas_call(
        _attn_ctx_kernel,
        grid=(b,),
        in_specs=[pl.BlockSpec((1, c, ATTN_WIDTH), lambda i: (i, 0, 0)),
                  pl.BlockSpec((1, c, KV_WIDTH), lambda i: (i, 0, 0)),
                  pl.BlockSpec((1, c, KV_WIDTH), lambda i: (i, 0, 0)),
                  pl.BlockSpec((1, N_Q_HEADS), lambda i: (0, 0))],
        out_specs=pl.BlockSpec((1, c, ATTN_WIDTH), lambda i: (i, 0, 0)),
        out_shape=jax.ShapeDtypeStruct((b, c, ATTN_WIDTH), BF16),
        compiler_params=_cparams("parallel"),
        name="attn_ctx",
    )(q, k, v, sink)


def _dft_pos_kernel(cs_ref, ss_ref, a_ref, b_ref, o_ref, acc_ref, *, scale):
    kk = pl.program_id(2)

    @pl.when(kk == 0)
    def _():
        acc_ref[...] = jnp.zeros_like(acc_ref)

    acc_ref[...] += _bdot(cs_ref[...], a_ref[0]) - _bdot(ss_ref[...], b_ref[0])

    @pl.when(kk == pl.num_programs(2) - 1)
    def _():
        o_ref[0] = (acc_ref[...] * scale).astype(BF16)


def _dft_pos(fa, fb, cs, ss):
    b, s, w = fa.shape
    t = min(1024, s)
    scale = 1.0 / math.sqrt(s * FNET_GROUP_DIM)
    return pl.pallas_call(
        functools.partial(_dft_pos_kernel, scale=scale),
        grid=(b, s // t, s // t),
        in_specs=[pl.BlockSpec((t, t), lambda i, m, k: (m, k)),
                  pl.BlockSpec((t, t), lambda i, m, k: (m, k)),
                  pl.BlockSpec((1, t, w), lambda i, m, k: (i, k, 0)),
                  pl.BlockSpec((1, t, w), lambda i, m, k: (i, k, 0))],
        out_specs=pl.BlockSpec((1, t, w), lambda i, m, k: (i, m, 0)),
        out_shape=jax.ShapeDtypeStruct((b, s, w), BF16),
        scratch_shapes=[pltpu.VMEM((t, w), F32)],
        compiler_params=_cparams("parallel", "parallel", "arbitrary"),
        name="dft_pos",
    )(cs, ss, fa, fb)


def _ab_out_kernel(a_ref, f_ref, w1_ref, w2_ref, x_ref, g_ref, o_ref):
    y = _bdot(a_ref[0], w1_ref[...]) + _bdot(f_ref[0], w2_ref[...])
    o_ref[0] = x_ref[0] + g_ref[0] * y


def _ab_out(a, fm, w_out, x, gate):
    b, s, d = x.shape
    tm = min(512, s)
    tok = lambda w: pl.BlockSpec((1, tm, w), lambda i, j: (i, j, 0))
    return pl.pallas_call(
        _ab_out_kernel,
        grid=(b, s // tm),
        in_specs=[tok(ATTN_WIDTH), tok(FNET_WIDTH),
                  pl.BlockSpec((ATTN_WIDTH, d), lambda i, j: (0, 0)),
                  pl.BlockSpec((FNET_WIDTH, d), lambda i, j: (0, 0)),
                  tok(d),
                  pl.BlockSpec((1, 1, d), lambda i, j: (i, 0, 0))],
        out_specs=tok(d),
        out_shape=jax.ShapeDtypeStruct((b, s, d), F32),
        compiler_params=_cparams("parallel", "parallel"),
        name="ab_out",
    )(a, fm, w_out[:ATTN_WIDTH], w_out[ATTN_WIDTH:], x, gate)


def _sgu_kernel(x_ref, nw_ref, sh_ref, sc_ref, g_ref, win_ref, gnw_ref, ws_ref, bs_ref, wout_ref, o_ref, uv_ref):
    x = x_ref[0]
    tm = x.shape[0]
    h = _norm_mod(x, nw_ref[...], sh_ref[0], sc_ref[0]).astype(BF16)
    z = jax.nn.gelu(_bdot(h, win_ref[...]))
    u = z[:, :SGU_WIDTH]
    v = z[:, SGU_WIDTH:]
    v = (v * lax.rsqrt(jnp.mean(v * v, axis=-1, keepdims=True) + RMS_EPS) * gnw_ref[...]).astype(BF16)
    for n in range(tm // SGU_CHUNK):
        r = slice(n * SGU_CHUNK, (n + 1) * SGU_CHUNK)
        for g in range(SGU_GROUPS):
            cs = slice(g * SGU_GROUP_DIM, (g + 1) * SGU_GROUP_DIM)
            sv = _bdot(ws_ref[g], v[r, cs]) + bs_ref[:, g:g + 1]
            uv_ref[r, cs] = (u[r, cs] * sv).astype(BF16)
    o_ref[0] = x + g_ref[0] * _bdot(uv_ref[...], wout_ref[...])


def _sgu(x, nw, shift, scale, gate, w_in, gnw, w_s, b_s_t, w_out):
    b, s, d = x.shape
    tm = min(256, s)
    tok = pl.BlockSpec((1, tm, d), lambda i, j: (i, j, 0))
    mod = pl.BlockSpec((1, 1, d), lambda i, j: (i, 0, 0))
    full = lambda a: pl.BlockSpec(a.shape, lambda i, j: (0,) * a.ndim)
    return pl.pallas_call(
        _sgu_kernel,
        grid=(b, s // tm),
        in_specs=[tok, full(nw), mod, mod, mod, full(w_in), full(gnw), full(w_s), full(b_s_t), full(w_out)],
        out_specs=tok,
        out_shape=jax.ShapeDtypeStruct((b, s, d), F32),
        scratch_shapes=[pltpu.VMEM((tm, SGU_WIDTH), BF16)],
        compiler_params=_cparams("parallel", "parallel"),
        name="sgu",
    )(x, nw, shift, scale, gate, w_in, gnw, w_s, b_s_t, w_out)


def _topk_rows(s, rid, k):
    big = jnp.iinfo(jnp.int32).max
    vals, idxs = [], []
    for _ in range(k):
        m = jnp.max(s, axis=0, keepdims=True)
        am = jnp.min(jnp.where(s == m, rid, big), axis=0, keepdims=True)
        vals.append(m)
        idxs.append(am)
        s = jnp.where(rid == am, -jnp.inf, s)
    return jnp.concatenate(vals, axis=0), jnp.concatenate(idxs, axis=0)


_PAIR_PIECES = ((0, 1, 0, 16), (1, 1, 0, 8), (2, 1, 0, 8), (3, 1, 0, 8),
                (8, 8, 0, 1), (0, 8, 0, 1), (0, 8, 1, 1), (0, 8, 2, 1))
_PAIR_ROW_PIECES = 4


def _pair_candidates(t0, t1):
    k = PEER_TOPK
    n = t0.shape[1]
    vals, ids = [], []
    for a0, na, b0, nb in _PAIR_PIECES:
        io = lax.broadcasted_iota(I32, (max(na, nb), n), 0)
        a = a0 + io if na > 1 else jnp.full_like(io, a0)
        b = b0 + io if nb > 1 else jnp.full_like(io, b0)
        ok = (a + 1) * (b + 1) <= k
        if na > 1:
            ok = ok & (a >= _PAIR_ROW_PIECES)
        vals.append(jnp.where(ok, t0[a0:a0 + na] + t1[b0:b0 + nb], -jnp.inf))
        ids.append(a * k + b)
    return jnp.concatenate(vals, axis=0), jnp.concatenate(ids, axis=0)


def _select_rows(sel, table):
    out = jnp.zeros(sel.shape, table.dtype)
    for a in range(table.shape[0]):
        out = jnp.where(sel == a, table[a:a + 1], out)
    return out


def _peer_idx_kernel(x_ref, nw_ref, sh_ref, sc_ref, wq_ref, keys_ref, hq_ref, e_ref, g_ref, q_ref):
    hq = _norm_mod(x_ref[...], nw_ref[...], sh_ref[0], sc_ref[0])
    hq_ref[...] = hq
    q_ref[...] = _bdot(hq.astype(BF16), wq_ref[...]).astype(BF16)

    def head(h, carry):
        tops, topi = [], []
        for s in range(2):
            col = pl.multiple_of((h * 2 + s) * PEER_HALF_DIM, PEER_HALF_DIM)
            qs = q_ref[:, pl.ds(col, PEER_HALF_DIM)]
            st = lax.dot_general(keys_ref[h * 2 + s], qs, (((1,), (1,)), ((), ())), preferred_element_type=F32)
            ts, ti = _topk_rows(st, lax.broadcasted_iota(I32, st.shape, 0), PEER_TOPK)
            tops.append(ts)
            topi.append(ti)
        best_s, best = _topk_rows(*_pair_candidates(tops[0], tops[1]), PEER_TOPK)
        i1 = _select_rows(best // PEER_TOPK, topi[0])
        i2 = _select_rows(best % PEER_TOPK, topi[1])
        p = jnp.exp(best_s - best_s[0:1])
        row = pl.multiple_of(h * PEER_TOPK, PEER_TOPK)
        e_ref[0, pl.ds(row, PEER_TOPK), :] = i1 * PEER_NKEYS + i2
        g_ref[0, pl.ds(row, PEER_TOPK), :] = p / jnp.sum(p, axis=0, keepdims=True)
        return carry

    lax.fori_loop(0, PEER_HEADS, head, 0)


def _peer_idx(x, nw, shift, scale, w_q, keys, tok0, ntok, seq):
    d = x.shape[1]
    tm = min(256, ntok, seq)
    nq = w_q.shape[1]
    t0 = tok0 // tm
    per_seq = seq // tm
    mod = pl.BlockSpec((1, 1, d), lambda i: ((i + t0) // per_seq, 0, 0))
    sel = pl.BlockSpec((1, PEER_SEL, tm), lambda i: (i, 0, 0))
    return pl.pallas_call(
        _peer_idx_kernel,
        grid=(ntok // tm,),
        in_specs=[pl.BlockSpec((tm, d), lambda i: (i + t0, 0)),
                  pl.BlockSpec((1, d), lambda i: (0, 0)),
                  mod, mod,
                  pl.BlockSpec((d, nq), lambda i: (0, 0)),
                  pl.BlockSpec(keys.shape, lambda i: (0, 0, 0))],
        out_specs=[pl.BlockSpec((tm, d), lambda i: (i, 0)), sel, sel],
        out_shape=[jax.ShapeDtypeStruct((ntok, d), F32),
                   jax.ShapeDtypeStruct((ntok // tm, PEER_SEL, tm), I32),
                   jax.ShapeDtypeStruct((ntok // tm, PEER_SEL, tm), F32)],
        scratch_shapes=[pltpu.VMEM((tm, nq), BF16)],
        compiler_params=_cparams("parallel"),
        name="peer_idx",
    )(x, nw, shift, scale, w_q, keys)


def _sc_worker_id():
    return lax.axis_index("s") * SC_CORES + lax.axis_index("c")


def _sc_gather(tab, idx):
    p = idx.shape[0]
    w = tab.shape[1]
    n = SC_GATHER_ROWS
    per_w = p // SC_WORKERS
    mesh = plsc.VectorSubcoreMesh(core_axis_name="c", subcore_axis_name="s")

    @functools.partial(
        pl.kernel, mesh=mesh, out_type=jax.ShapeDtypeStruct((p, w), U32),
        scratch_types=[pltpu.VMEM((n,), I32), pltpu.VMEM((n, w), U32), pltpu.SemaphoreType.DMA])
    def gather(tab_hbm, idx_hbm, out_hbm, idx_v, rows, sem):
        base = _sc_worker_id() * per_w

        @pl.loop(0, per_w // n)
        def _(i):
            off = pl.multiple_of(base + i * n, n)
            pltpu.sync_copy(idx_hbm.at[pl.ds(off, n)], idx_v)
            pltpu.async_copy(tab_hbm.at[idx_v], rows, sem).wait()
            pltpu.sync_copy(rows, out_hbm.at[pl.ds(off, n)])

    return gather(tab, idx)


def _sc_weighted_rows(tab, idx, wgt):
    k = PEER_SEL
    n_tok = idx.shape[0] // k
    w = tab.shape[1]
    lanes = SC_LANES
    nvec = w // lanes
    slab = 4
    ch = SC_GATHER_ROWS
    per_w = n_tok // SC_WORKERS
    mesh = plsc.VectorSubcoreMesh(core_axis_name="c", subcore_axis_name="s")

    @functools.partial(
        pl.kernel, mesh=mesh, out_type=jax.ShapeDtypeStruct((n_tok, 2 * w), F32),
        scratch_types=[pltpu.VMEM((k,), I32), pltpu.VMEM((k * lanes,), F32), pltpu.VMEM((ch, w), U32),
                       pltpu.VMEM((2 * w,), F32), pltpu.SemaphoreType.DMA],
        compiler_params=pltpu.CompilerParams(needs_layout_passes=False))
    def weighted(tab_hbm, idx_hbm, w_hbm, out_hbm, idx_v, w_v, rows, acc_v, sem):
        base = _sc_worker_id() * per_w

        @pl.loop(0, per_w)
        def _(tt):
            tok = base + tt
            off = pl.multiple_of(tok * k, k)
            pltpu.sync_copy(idx_hbm.at[pl.ds(off, k)], idx_v)
            pltpu.sync_copy(w_hbm.at[pl.ds(pl.multiple_of(off * lanes, k * lanes), k * lanes)], w_v)
            for c in range(k // ch):
                pltpu.async_copy(tab_hbm.at[idx_v.at[pl.ds(c * ch, ch)]], rows, sem).wait()

                @pl.loop(0, nvec // slab)
                def _(sb, c=c):
                    col = pl.multiple_of(sb * (slab * lanes), slab * lanes)
                    if c == 0:
                        acc = [jnp.zeros((lanes,), F32) for _ in range(2 * slab)]
                    else:
                        acc = [acc_v[pl.ds(part * w + col + j * lanes, lanes)]
                               for j in range(slab) for part in range(2)]
                    for r in range(ch):
                        wv = w_v[pl.ds((c * ch + r) * lanes, lanes)]
                        for j in range(slab):
                            x = rows[r, pl.ds(col + j * lanes, lanes)]
                            acc[2 * j] += wv * lax.bitcast_convert_type(x << 16, F32)
                            acc[2 * j + 1] += wv * lax.bitcast_convert_type(x & jnp.uint32(0xFFFF0000), F32)
                    for j in range(slab):
                        acc_v[pl.ds(col + j * lanes, lanes)] = acc[2 * j]
                        acc_v[pl.ds(w + col + j * lanes, lanes)] = acc[2 * j + 1]
            pltpu.sync_copy(acc_v, out_hbm.at[tok])

    return weighted(tab, idx, wgt)


def _unpack_pair(w32):
    lo = pltpu.bitcast(w32 << 16, F32)
    hi = pltpu.bitcast(w32 & jnp.uint32(0xFFFF0000), F32)
    return lo, hi


def _split3(x):
    hi = x.astype(BF16)
    r1 = x - hi.astype(F32)
    mid = r1.astype(BF16)
    lo = (r1 - mid.astype(F32)).astype(BF16)
    return jnp.concatenate([hi, mid, lo], axis=1)


def _rowsum_all_lanes(x):
    pieces = _split3(x)
    return _bdot(pieces, jnp.ones((pieces.shape[1], x.shape[1]), BF16))


def _peer_score_kernel(ug_ref, hq_ref, g_ref, rep_ref, w_ref, part_ref):
    tb = PEER_TOK_BLOCK
    half = hq_ref.shape[1] // 2
    nl = 128
    h_lo = hq_ref[:, :half]
    h_hi = hq_ref[:, half:]

    def score(k, carry):
        lo, hi = _unpack_pair(ug_ref[pl.ds(pl.multiple_of(k * tb, tb), tb), :])
        p = lo * h_lo + hi * h_hi
        part_ref[k] = sum(p[:, i * nl:(i + 1) * nl] for i in range(half // nl))
        return carry

    lax.fori_loop(0, PEER_SEL, score, 0, unroll=8)
    scb = _rowsum_all_lanes(part_ref[...].reshape(PEER_SEL * tb, nl)).reshape(PEER_SEL, tb, nl)
    diag = (lax.broadcasted_iota(I32, (PEER_SEL, tb, PEER_SEL), 0)
            == lax.broadcasted_iota(I32, (PEER_SEL, tb, PEER_SEL), 2))
    wgt = g_ref[...] * jax.nn.gelu(jnp.sum(jnp.where(diag, scb, 0.0), axis=0))
    w_ref[...] = _bdot(_split3(wgt), rep_ref[...])


def _peer_score(ug, hq, g):
    ntok, d = hq.shape
    tb = PEER_TOK_BLOCK
    rep = jnp.tile(jnp.repeat(jnp.eye(PEER_SEL, dtype=BF16), SC_LANES, axis=1), (3, 1))
    return pl.pallas_call(
        _peer_score_kernel,
        grid=(ntok // tb,),
        in_specs=[pl.BlockSpec((tb * PEER_SEL, ug.shape[1]), lambda i: (i, 0)),
                  pl.BlockSpec((tb, d), lambda i: (i, 0)),
                  pl.BlockSpec((tb, PEER_SEL), lambda i: (i, 0)),
                  pl.BlockSpec(rep.shape, lambda i: (0, 0))],
        out_specs=pl.BlockSpec((tb, PEER_SEL * SC_LANES), lambda i: (i, 0)),
        out_shape=jax.ShapeDtypeStruct((ntok, PEER_SEL * SC_LANES), F32),
        scratch_shapes=[pltpu.VMEM((PEER_SEL, tb, 128), F32)],
        compiler_params=_cparams("parallel"),
        name="peer_score",
    )(ug, hq, g, rep)


def _resid_kernel(x_ref, y_ref, gate_ref, o_ref):
    o_ref[...] = x_ref[...] + gate_ref[0] * y_ref[...]


def _resid(x, y, gate, tok0, seq):
    t, d = x.shape
    ntok = y.shape[0]
    tm = min(512, ntok, seq)
    t0 = tok0 // tm
    per_seq = seq // tm
    tok = pl.BlockSpec((tm, d), lambda i: (i + t0, 0))
    return pl.pallas_call(
        _resid_kernel,
        grid=(ntok // tm,),
        in_specs=[tok, pl.BlockSpec((tm, d), lambda i: (i, 0)),
                  pl.BlockSpec((1, 1, d), lambda i: ((i + t0) // per_seq, 0, 0))],
        out_specs=tok,
        out_shape=jax.ShapeDtypeStruct((t, d), F32),
        input_output_aliases={0: 0},
        compiler_params=_cparams("parallel"),
        name="peer_resid",
    )(x, y, gate)


def _peer_groups(t):
    return max(1, min(4, t // 2048))


def _peer_ffn(x, nw, shift, scale, gate, w_q, keys, tab_u, tab_v):
    b, s, d = x.shape
    t = b * s
    x2 = x.reshape(t, d)
    n_groups = _peer_groups(t)
    gtok = t // n_groups
    tb = PEER_TOK_BLOCK
    staged = []
    for gi in range(n_groups):
        hq, e_t, g_t = _peer_idx(x2, nw, shift, scale, w_q, keys, gi * gtok, gtok, s)
        nblk, _, tm = e_t.shape
        idx_blk = e_t.reshape(nblk, PEER_SEL, tm // tb, tb).transpose(0, 2, 1, 3).reshape(-1)
        idx_tok = e_t.transpose(0, 2, 1).reshape(-1)
        g = g_t.transpose(0, 2, 1).reshape(gtok, PEER_SEL)
        staged.append((hq, idx_tok, g, _sc_gather(tab_u, idx_blk)))
    for gi, (hq, idx_tok, g, ug) in enumerate(staged):
        wgt = _peer_score(ug, hq, g)
        y = _sc_weighted_rows(tab_v, idx_tok, wgt.reshape(-1))
        x2 = _resid(x2, y, gate, gi * gtok, s)
    return x2.reshape(b, s, d)


def _rms_kernel(x_ref, w_ref, o_ref):
    x = x_ref[...]
    o_ref[...] = x * lax.rsqrt(jnp.mean(x * x, axis=-1, keepdims=True) + RMS_EPS) * w_ref[...]


def _final_norm(x, w):
    b, s, d = x.shape
    t = b * s
    tm = min(512, t)
    out = pl.pallas_call(
        _rms_kernel,
        grid=(t // tm,),
        in_specs=[pl.BlockSpec((tm, d), lambda i: (i, 0)), pl.BlockSpec((1, d), lambda i: (0, 0))],
        out_specs=pl.BlockSpec((tm, d), lambda i: (i, 0)),
        out_shape=jax.ShapeDtypeStruct((t, d), F32),
        compiler_params=_cparams("parallel"),
        name="final_norm",
    )(x.reshape(t, d), w.reshape(1, d))
    return out.reshape(b, s, d)


def _rope_tables(s):
    t = jnp.arange(s)
    row = (t // GRID_W).astype(F32)
    col = (t % GRID_W).astype(F32)
    half = HEAD_DIM // 2
    inv = ROPE_THETA ** (-jnp.arange(0, half, 2, dtype=F32) / half)
    ar = row[:, None] * inv
    ac = col[:, None] * inv
    ang = jnp.concatenate([ar, ar, ac, ac], axis=1)
    q = half // 2
    sign = jnp.concatenate([-jnp.ones(q), jnp.ones(q), -jnp.ones(q), jnp.ones(q)]).astype(F32)
    reps = QK_WIDTH // HEAD_DIM
    return jnp.tile(jnp.cos(ang), (1, reps)), jnp.tile(jnp.sin(ang) * sign, (1, reps))


def _rope_partner_cols():
    j = jnp.arange(QK_WIDTH)
    dd = j % HEAD_DIM
    q = HEAD_DIM // 4
    return j - dd + jnp.where((dd % (2 * q)) < q, dd + q, dd - q)


def _dft_tables(n, dtype):
    j = jnp.arange(n, dtype=I32)
    ang = ((j[:, None] * j[None, :]) % n).astype(F32) * (2.0 * math.pi / n)
    return jnp.cos(ang).astype(dtype), jnp.sin(ang).astype(dtype)


def _pack_table(tab):
    half = tab.shape[1] // 2
    bits = lax.bitcast_convert_type(tab.astype(BF16), jnp.uint16).astype(U32)
    return bits[:, :half] | (bits[:, half:] << 16)


def kernel(x, c, ctx, c_ctx, ada_w, ada_b, norm_mix_w, norm_ffn_w, ab_w_in, ab_w_out, attn_sink, gs_w_in, gs_norm_w,
           gs_w_s, gs_b_s, gs_w_out, peer_w_q, peer_keys, peer_u, peer_v, norm_out_w):
    bsz, seq, d = x.shape
    n_ctx = ctx.shape[1]
    depth = ada_w.shape[0]

    cond = jnp.concatenate([c, c_ctx[None], jnp.zeros((16 - bsz - 1, d), F32)], axis=0)
    mods = _ada_all(cond, ada_w, ada_b)

    cos, sin = _rope_tables(seq)
    partner = _rope_partner_cols()
    dft_cc, dft_cs = _dft_tables(FNET_GROUP_DIM, BF16)
    dft_c = jnp.concatenate([dft_cc, dft_cs], axis=1)
    pos_l = _dft_tables(seq, BF16)
    pos_c = _dft_tables(n_ctx, BF16)

    last_attn = ((depth - 1) // 2) * 2
    for i in range(depth):
        j = i // 2
        is_ab = i % 2 == 0
        upd_ctx = i < last_attn
        need_ctx = upd_ctx or is_ab
        m_l = [m[:, None, :] for m in jnp.split(mods[i, :bsz], N_MOD, axis=-1)]
        m_c = [jnp.broadcast_to(m[None], (bsz, 1, d)) for m in jnp.split(mods[i, bsz:bsz + 1], N_MOD, axis=-1)]
        nw_m = norm_mix_w[i].reshape(1, d)
        nw_f = norm_ffn_w[i].reshape(1, d)
        y_ctx = None
        if is_ab:
            w_in = ab_w_in[j]
            w_ext = jnp.concatenate([w_in, w_in[:, partner]], axis=1).astype(BF16)
            w_out = ab_w_out[j].astype(BF16)
            sink = attn_sink[j].reshape(1, N_Q_HEADS)
            q_l, k_l, v_l, fa_l, fb_l = _ab_in(x, nw_m, m_l[0], m_l[1], w_ext, cos, sin, dft_c, True)
            q_c, k_c, v_c, fa_c, fb_c = _ab_in(ctx, nw_m, m_c[0], m_c[1], w_ext, cos[:n_ctx], sin[:n_ctx], dft_c, False)
            a_l = _attn_local(q_l, k_l, v_l, k_c, v_c, sink)
            x = _ab_out(a_l, _dft_pos(fa_l, fb_l, *pos_l), w_out, x, m_l[2])
            if upd_ctx:
                a_c = _attn_ctx(q_c, k_c, v_c, sink)
                ctx = _ab_out(a_c, _dft_pos(fa_c, fb_c, *pos_c), w_out, ctx, m_c[2])
        else:
            sgu_w = (gs_w_in[j].astype(BF16), gs_norm_w[j].reshape(1, -1), gs_w_s[j].astype(BF16), gs_b_s[j].T,
                     gs_w_out[j].astype(BF16))
            x = _sgu(x, nw_m, m_l[0], m_l[1], m_l[2], *sgu_w)
            if upd_ctx:
                ctx = _sgu(ctx, nw_m, m_c[0], m_c[1], m_c[2], *sgu_w)
        w_q = peer_w_q[i].astype(BF16)
        keys = peer_keys[i].reshape(PEER_HEADS * 2, PEER_NKEYS, PEER_HALF_DIM).astype(BF16)
        tab_u = _pack_table(peer_u[i])
        tab_v = _pack_table(peer_v[i])
        x = _peer_ffn(x, nw_f, m_l[3], m_l[4], m_l[5], w_q, keys, tab_u, tab_v)
        if upd_ctx:
            ctx = _peer_ffn(ctx, nw_f, m_c[3], m_c[4], m_c[5], w_q, keys, tab_u, tab_v)
        del need_ctx, y_ctx
    return _final_norm(x, norm_out_w)
```

```python
import functools
import math

import jax
import jax.numpy as jnp
from jax import lax
from jax.experimental import pallas as pl
from jax.experimental.pallas import tpu as pltpu
from jax.experimental.pallas import tpu_sc as plsc

F32 = jnp.float32
BF16 = jnp.bfloat16
I32 = jnp.int32
U32 = jnp.uint32

GRID_W = 64
N_Q_HEADS = 8
N_KV_HEADS = 2
Q_PER_KV = N_Q_HEADS // N_KV_HEADS
HEAD_DIM = 64
WINDOW = 128
ROPE_THETA = 10000.0
ATTN_WIDTH = N_Q_HEADS * HEAD_DIM
KV_WIDTH = N_KV_HEADS * HEAD_DIM
QK_WIDTH = ATTN_WIDTH + KV_WIDTH
FNET_GROUPS = 4
FNET_GROUP_DIM = 128
FNET_WIDTH = FNET_GROUPS * FNET_GROUP_DIM
AB_IN_WIDTH = ATTN_WIDTH + 2 * KV_WIDTH + FNET_WIDTH
SGU_GROUPS = 8
SGU_GROUP_DIM = 128
SGU_WIDTH = SGU_GROUPS * SGU_GROUP_DIM
SGU_CHUNK = 128
PEER_HEADS = 8
PEER_NKEYS = 128
PEER_TOPK = 16
PEER_HALF_DIM = 128
PEER_SEL = PEER_HEADS * PEER_TOPK
N_MOD = 6
RMS_EPS = 1e-6
NEG_INF = -1e30

SC_CORES = 2
SC_SUBCORES = 16
SC_WORKERS = SC_CORES * SC_SUBCORES
SC_LANES = 16
SC_GATHER_ROWS = 64
PEER_TOK_BLOCK = 16
VMEM_LIMIT = 48 * 1024 * 1024


def _cparams(*sem):
    return pltpu.CompilerParams(dimension_semantics=sem, vmem_limit_bytes=VMEM_LIMIT)


def _norm_mod(x, nw, shift, scale):
    y = x * lax.rsqrt(jnp.mean(x * x, axis=-1, keepdims=True) + RMS_EPS) * nw
    return y * (1.0 + scale) + shift


def _bdot(a, b):
    return jnp.dot(a, b, preferred_element_type=F32)


def _ada_kernel(c_ref, w_ref, b_ref, o_ref):
    c = c_ref[...]
    s = (c * jax.nn.sigmoid(c)).astype(BF16)
    o_ref[0] = _bdot(s, w_ref[0].astype(BF16)) + b_ref[0]


def _ada_all(cond, ada_w, ada_b):
    depth, d, n = ada_w.shape
    r = cond.shape[0]
    tn = 1536
    return pl.pallas_call(
        _ada_kernel,
        grid=(depth, n // tn),
        in_specs=[pl.BlockSpec((r, d), lambda l, j: (0, 0)),
                  pl.BlockSpec((1, d, tn), lambda l, j: (l, 0, j)),
                  pl.BlockSpec((1, 1, tn), lambda l, j: (l, 0, j))],
        out_specs=pl.BlockSpec((1, r, tn), lambda l, j: (l, 0, j)),
        out_shape=jax.ShapeDtypeStruct((depth, r, n), F32),
        compiler_params=_cparams("parallel", "parallel"),
        name="ada",
    )(cond, ada_w, ada_b.reshape(depth, 1, n))


def _ab_in_kernel(x_ref, nw_ref, sh_ref, sc_ref, w_ref, cos_ref, sin_ref, dft_ref,
                  q_ref, k_ref, v_ref, fa_ref, fb_ref, *, rope):
    h = _norm_mod(x_ref[0], nw_ref[...], sh_ref[0], sc_ref[0]).astype(BF16)
    p = _bdot(h, w_ref[...])
    qk = p[:, :QK_WIDTH]
    if rope:
        qk = qk * cos_ref[...] + p[:, AB_IN_WIDTH:] * sin_ref[...]
    q_ref[0] = qk[:, :ATTN_WIDTH].astype(BF16)
    k_ref[0] = qk[:, ATTN_WIDTH:].astype(BF16)
    v_ref[0] = p[:, QK_WIDTH:QK_WIDTH + KV_WIDTH].astype(BF16)
    f0 = QK_WIDTH + KV_WIDTH
    for g in range(FNET_GROUPS):
        fg = p[:, f0 + g * FNET_GROUP_DIM:f0 + (g + 1) * FNET_GROUP_DIM].astype(BF16)
        ab = _bdot(fg, dft_ref[...])
        fa_ref[0, :, g * FNET_GROUP_DIM:(g + 1) * FNET_GROUP_DIM] = ab[:, :FNET_GROUP_DIM].astype(BF16)
        fb_ref[0, :, g * FNET_GROUP_DIM:(g + 1) * FNET_GROUP_DIM] = ab[:, FNET_GROUP_DIM:].astype(BF16)


def _ab_in(x, nw, shift, scale, w_ext, cos, sin, dft_c, rope):
    b, s, d = x.shape
    tm = min(512, s)
    n_ext = w_ext.shape[1]
    outs = [jax.ShapeDtypeStruct((b, s, ATTN_WIDTH), BF16), jax.ShapeDtypeStruct((b, s, KV_WIDTH), BF16),
            jax.ShapeDtypeStruct((b, s, KV_WIDTH), BF16), jax.ShapeDtypeStruct((b, s, FNET_WIDTH), BF16),
            jax.ShapeDtypeStruct((b, s, FNET_WIDTH), BF16)]
    tok = lambda w: pl.BlockSpec((1, tm, w), lambda i, j: (i, j, 0))
    return pl.pallas_call(
        functools.partial(_ab_in_kernel, rope=rope),
        grid=(b, s // tm),
        in_specs=[tok(d),
                  pl.BlockSpec((1, d), lambda i, j: (0, 0)),
                  pl.BlockSpec((1, 1, d), lambda i, j: (i, 0, 0)),
                  pl.BlockSpec((1, 1, d), lambda i, j: (i, 0, 0)),
                  pl.BlockSpec((d, n_ext), lambda i, j: (0, 0)),
                  pl.BlockSpec((tm, QK_WIDTH), lambda i, j: (j, 0)),
                  pl.BlockSpec((tm, QK_WIDTH), lambda i, j: (j, 0)),
                  pl.BlockSpec(dft_c.shape, lambda i, j: (0, 0))],
        out_specs=[tok(ATTN_WIDTH), tok(KV_WIDTH), tok(KV_WIDTH), tok(FNET_WIDTH), tok(FNET_WIDTH)],
        out_shape=outs,
        compiler_params=_cparams("parallel", "parallel"),
        name="ab_in",
    )(x, nw, shift, scale, w_ext, cos, sin, dft_c)


def _softmax_pv(qh, kk, vv, valid, sk, scale):
    s = lax.dot_general(qh, kk, (((1,), (1,)), ((), ())), preferred_element_type=F32) * scale
    if valid is not None:
        s = jnp.where(valid, s, NEG_INF)
    m = jnp.maximum(jnp.max(s, axis=-1, keepdims=True), sk)
    p = jnp.exp(s - m)
    inv_den = 1.0 / (jnp.sum(p, axis=-1, keepdims=True) + jnp.exp(sk - m))
    return _bdot((p * inv_den).astype(BF16), vv)


def _attn_local_kernel(q_ref, kp_ref, ko_ref, kn_ref, vp_ref, vo_ref, vn_ref, kc_ref, vc_ref, sink_ref, o_ref,
                       *, seq):
    j = pl.program_id(1)
    blk = q_ref.shape[1]
    n_ctx = kc_ref.shape[1]
    q = q_ref[0]
    kl = jnp.concatenate([kp_ref[0], ko_ref[0], kn_ref[0], kc_ref[0]], axis=0)
    vl = jnp.concatenate([vp_ref[0], vo_ref[0], vn_ref[0], vc_ref[0]], axis=0)
    nk = 3 * blk + n_ctx
    qi = lax.broadcasted_iota(I32, (blk, nk), 0)
    cj = lax.broadcasted_iota(I32, (blk, nk), 1)
    kj = cj - blk
    kpos = j * blk + kj
    valid = (cj >= 3 * blk) | ((jnp.abs(qi - kj) <= WINDOW) & (kpos >= 0) & (kpos < seq))
    scale = HEAD_DIM ** -0.5
    for kvh in range(N_KV_HEADS):
        kk = kl[:, kvh * HEAD_DIM:(kvh + 1) * HEAD_DIM]
        vv = vl[:, kvh * HEAD_DIM:(kvh + 1) * HEAD_DIM]
        for g in range(Q_PER_KV):
            h = kvh * Q_PER_KV + g
            o = _softmax_pv(q[:, h * HEAD_DIM:(h + 1) * HEAD_DIM], kk, vv, valid, sink_ref[0:1, h:h + 1], scale)
            o_ref[0, :, h * HEAD_DIM:(h + 1) * HEAD_DIM] = o.astype(BF16)


def _attn_local(q, k, v, kc, vc, sink):
    b, s, _ = q.shape
    c = kc.shape[1]
    blk = WINDOW
    nb = s // blk
    qspec = pl.BlockSpec((1, blk, ATTN_WIDTH), lambda i, j: (i, j, 0))
    prev = pl.BlockSpec((1, blk, KV_WIDTH), lambda i, j: (i, jnp.maximum(j - 1, 0), 0))
    own = pl.BlockSpec((1, blk, KV_WIDTH), lambda i, j: (i, j, 0))
    nxt = pl.BlockSpec((1, blk, KV_WIDTH), lambda i, j: (i, jnp.minimum(j + 1, nb - 1), 0))
    cspec = pl.BlockSpec((1, c, KV_WIDTH), lambda i, j: (i, 0, 0))
    return pl.pallas_call(
        functools.partial(_attn_local_kernel, seq=s),
        grid=(b, nb),
        in_specs=[qspec, prev, own, nxt, prev, own, nxt, cspec, cspec,
                  pl.BlockSpec((1, N_Q_HEADS), lambda i, j: (0, 0))],
        out_specs=qspec,
        out_shape=jax.ShapeDtypeStruct((b, s, ATTN_WIDTH), BF16),
        compiler_params=_cparams("parallel", "parallel"),
        name="attn_local",
    )(q, k, k, k, v, v, v, kc, vc, sink)


def _attn_ctx_kernel(q_ref, k_ref, v_ref, sink_ref, o_ref):
    q = q_ref[0]
    scale = HEAD_DIM ** -0.5
    for kvh in range(N_KV_HEADS):
        kk = k_ref[0][:, kvh * HEAD_DIM:(kvh + 1) * HEAD_DIM]
        vv = v_ref[0][:, kvh * HEAD_DIM:(kvh + 1) * HEAD_DIM]
        for g in range(Q_PER_KV):
            h = kvh * Q_PER_KV + g
            o = _softmax_pv(q[:, h * HEAD_DIM:(h + 1) * HEAD_DIM], kk, vv, None, sink_ref[0:1, h:h + 1], scale)
            o_ref[0, :, h * HEAD_DIM:(h + 1) * HEAD_DIM] = o.astype(BF16)


def _attn_ctx(q, k, v, sink):
    b, c, _ = q.shape
    return pl.pallas_call(
        _attn_ctx_kernel,
        grid=(b,),
        in_specs=[pl.BlockSpec((1, c, ATTN_WIDTH), lambda i: (i, 0, 0)),
                  pl.BlockSpec((1, c, KV_WIDTH), lambda i: (i, 0, 0)),
                  pl.BlockSpec((1, c, KV_WIDTH), lambda i: (i, 0, 0)),
                  pl.BlockSpec((1, N_Q_HEADS), lambda i: (0, 0))],
        out_specs=pl.BlockSpec((1, c, ATTN_WIDTH), lambda i: (i, 0, 0)),
        out_shape=jax.ShapeDtypeStruct((b, c, ATTN_WIDTH), BF16),
        compiler_params=_cparams("parallel"),
        name="attn_ctx",
    )(q, k, v, sink)


def _dft_pos_kernel(cs_ref, ss_ref, a_ref, b_ref, o_ref, acc_ref, *, scale):
    kk = pl.program_id(2)

    @pl.when(kk == 0)
    def _():
        acc_ref[...] = jnp.zeros_like(acc_ref)

    acc_ref[...] += _bdot(cs_ref[...], a_ref[0]) - _bdot(ss_ref[...], b_ref[0])

    @pl.when(kk == pl.num_programs(2) - 1)
    def _():
        o_ref[0] = (acc_ref[...] * scale).astype(BF16)


def _dft_pos(fa, fb, cs, ss):
    b, s, w = fa.shape
    t = min(1024, s)
    scale = 1.0 / math.sqrt(s * FNET_GROUP_DIM)
    return pl.pallas_call(
        functools.partial(_dft_pos_kernel, scale=scale),
        grid=(b, s // t, s // t),
        in_specs=[pl.BlockSpec((t, t), lambda i, m, k: (m, k)),
                  pl.BlockSpec((t, t), lambda i, m, k: (m, k)),
                  pl.BlockSpec((1, t, w), lambda i, m, k: (i, k, 0)),
                  pl.BlockSpec((1, t, w), lambda i, m, k: (i, k, 0))],
        out_specs=pl.BlockSpec((1, t, w), lambda i, m, k: (i, m, 0)),
        out_shape=jax.ShapeDtypeStruct((b, s, w), BF16),
        scratch_shapes=[pltpu.VMEM((t, w), F32)],
        compiler_params=_cparams("parallel", "parallel", "arbitrary"),
        name="dft_pos",
    )(cs, ss, fa, fb)


def _ab_out_kernel(a_ref, f_ref, w1_ref, w2_ref, x_ref, g_ref, o_ref):
    y = _bdot(a_ref[0], w1_ref[...]) + _bdot(f_ref[0], w2_ref[...])
    o_ref[0] = x_ref[0] + g_ref[0] * y


def _ab_out(a, fm, w_out, x, gate):
    b, s, d = x.shape
    tm = min(512, s)
    tok = lambda w: pl.BlockSpec((1, tm, w), lambda i, j: (i, j, 0))
    return pl.pallas_call(
        _ab_out_kernel,
        grid=(b, s // tm),
        in_specs=[tok(ATTN_WIDTH), tok(FNET_WIDTH),
                  pl.BlockSpec((ATTN_WIDTH, d), lambda i, j: (0, 0)),
                  pl.BlockSpec((FNET_WIDTH, d), lambda i, j: (0, 0)),
                  tok(d),
                  pl.BlockSpec((1, 1, d), lambda i, j: (i, 0, 0))],
        out_specs=tok(d),
        out_shape=jax.ShapeDtypeStruct((b, s, d), F32),
        compiler_params=_cparams("parallel", "parallel"),
        name="ab_out",
    )(a, fm, w_out[:ATTN_WIDTH], w_out[ATTN_WIDTH:], x, gate)


def _sgu_kernel(x_ref, nw_ref, sh_ref, sc_ref, g_ref, win_ref, gnw_ref, ws_ref, bs_ref, wout_ref, o_ref, uv_ref):
    x = x_ref[0]
    tm = x.shape[0]
    h = _norm_mod(x, nw_ref[...], sh_ref[0], sc_ref[0]).astype(BF16)
    z = jax.nn.gelu(_bdot(h, win_ref[...]))
    u = z[:, :SGU_WIDTH]
    v = z[:, SGU_WIDTH:]
    v = (v * lax.rsqrt(jnp.mean(v * v, axis=-1, keepdims=True) + RMS_EPS) * gnw_ref[...]).astype(BF16)
    for n in range(tm // SGU_CHUNK):
        r = slice(n * SGU_CHUNK, (n + 1) * SGU_CHUNK)
        for g in range(SGU_GROUPS):
            cs = slice(g * SGU_GROUP_DIM, (g + 1) * SGU_GROUP_DIM)
            sv = _bdot(ws_ref[g], v[r, cs]) + bs_ref[:, g:g + 1]
            uv_ref[r, cs] = (u[r, cs] * sv).astype(BF16)
    o_ref[0] = x + g_ref[0] * _bdot(uv_ref[...], wout_ref[...])


def _sgu(x, nw, shift, scale, gate, w_in, gnw, w_s, b_s_t, w_out):
    b, s, d = x.shape
    tm = min(256, s)
    tok = pl.BlockSpec((1, tm, d), lambda i, j: (i, j, 0))
    mod = pl.BlockSpec((1, 1, d), lambda i, j: (i, 0, 0))
    full = lambda a: pl.BlockSpec(a.shape, lambda i, j: (0,) * a.ndim)
    return pl.pallas_call(
        _sgu_kernel,
        grid=(b, s // tm),
        in_specs=[tok, full(nw), mod, mod, mod, full(w_in), full(gnw), full(w_s), full(b_s_t), full(w_out)],
        out_specs=tok,
        out_shape=jax.ShapeDtypeStruct((b, s, d), F32),
        scratch_shapes=[pltpu.VMEM((tm, SGU_WIDTH), BF16)],
        compiler_params=_cparams("parallel", "parallel"),
        name="sgu",
    )(x, nw, shift, scale, gate, w_in, gnw, w_s, b_s_t, w_out)


def _topk_rows(s, rid, k):
    big = jnp.iinfo(jnp.int32).max
    vals, idxs = [], []
    for _ in range(k):
        m = jnp.max(s, axis=0, keepdims=True)
        am = jnp.min(jnp.where(s == m, rid, big), axis=0, keepdims=True)
        vals.append(m)
        idxs.append(am)
        s = jnp.where(rid == am, -jnp.inf, s)
    return jnp.concatenate(vals, axis=0), jnp.concatenate(idxs, axis=0)


_PAIR_PIECES = ((0, 1, 0, 16), (1, 1, 0, 8), (2, 1, 0, 8), (3, 1, 0, 8),
                (8, 8, 0, 1), (0, 8, 0, 1), (0, 8, 1, 1), (0, 8, 2, 1))
_PAIR_ROW_PIECES = 4


def _pair_candidates(t0, t1):
    k = PEER_TOPK
    n = t0.shape[1]
    vals, ids = [], []
    for a0, na, b0, nb in _PAIR_PIECES:
        io = lax.broadcasted_iota(I32, (max(na, nb), n), 0)
        a = a0 + io if na > 1 else jnp.full_like(io, a0)
        b = b0 + io if nb > 1 else jnp.full_like(io, b0)
        ok = (a + 1) * (b + 1) <= k
        if na > 1:
            ok = ok & (a >= _PAIR_ROW_PIECES)
        vals.append(jnp.where(ok, t0[a0:a0 + na] + t1[b0:b0 + nb], -jnp.inf))
        ids.append(a * k + b)
    return jnp.concatenate(vals, axis=0), jnp.concatenate(ids, axis=0)


def _select_rows(sel, table):
    out = jnp.zeros(sel.shape, table.dtype)
    for a in range(table.shape[0]):
        out = jnp.where(sel == a, table[a:a + 1], out)
    return out


def _peer_idx_kernel(x_ref, nw_ref, sh_ref, sc_ref, wq_ref, keys_ref, hq_ref, e_ref, g_ref, q_ref):
    hq = _norm_mod(x_ref[...], nw_ref[...], sh_ref[0], sc_ref[0])
    hq_ref[...] = hq
    q_ref[...] = _bdot(hq.astype(BF16), wq_ref[...]).astype(BF16)

    def head(h, carry):
        tops, topi = [], []
        for s in range(2):
            col = pl.multiple_of((h * 2 + s) * PEER_HALF_DIM, PEER_HALF_DIM)
            qs = q_ref[:, pl.ds(col, PEER_HALF_DIM)]
            st = lax.dot_general(keys_ref[h * 2 + s], qs, (((1,), (1,)), ((), ())), preferred_element_type=F32)
            ts, ti = _topk_rows(st, lax.broadcasted_iota(I32, st.shape, 0), PEER_TOPK)
            tops.append(ts)
            topi.append(ti)
        best_s, best = _topk_rows(*_pair_candidates(tops[0], tops[1]), PEER_TOPK)
        i1 = _select_rows(best // PEER_TOPK, topi[0])
        i2 = _select_rows(best % PEER_TOPK, topi[1])
        p = jnp.exp(best_s - best_s[0:1])
        row = pl.multiple_of(h * PEER_TOPK, PEER_TOPK)
        e_ref[0, pl.ds(row, PEER_TOPK), :] = i1 * PEER_NKEYS + i2
        g_ref[0, pl.ds(row, PEER_TOPK), :] = p / jnp.sum(p, axis=0, keepdims=True)
        return carry

    lax.fori_loop(0, PEER_HEADS, head, 0)


def _peer_idx(x, nw, shift, scale, w_q, keys, tok0, ntok, seq):
    d = x.shape[1]
    tm = min(256, ntok, seq)
    nq = w_q.shape[1]
    t0 = tok0 // tm
    per_seq = seq // tm
    mod = pl.BlockSpec((1, 1, d), lambda i: ((i + t0) // per_seq, 0, 0))
    sel = pl.BlockSpec((1, PEER_SEL, tm), lambda i: (i, 0, 0))
    return pl.pallas_call(
        _peer_idx_kernel,
        grid=(ntok // tm,),
        in_specs=[pl.BlockSpec((tm, d), lambda i: (i + t0, 0)),
                  pl.BlockSpec((1, d), lambda i: (0, 0)),
                  mod, mod,
                  pl.BlockSpec((d, nq), lambda i: (0, 0)),
                  pl.BlockSpec(keys.shape, lambda i: (0, 0, 0))],
        out_specs=[pl.BlockSpec((tm, d), lambda i: (i, 0)), sel, sel],
        out_shape=[jax.ShapeDtypeStruct((ntok, d), F32),
                   jax.ShapeDtypeStruct((ntok // tm, PEER_SEL, tm), I32),
                   jax.ShapeDtypeStruct((ntok // tm, PEER_SEL, tm), F32)],
        scratch_shapes=[pltpu.VMEM((tm, nq), BF16)],
        compiler_params=_cparams("parallel"),
        name="peer_idx",
    )(x, nw, shift, scale, w_q, keys)


def _sc_worker_id():
    return lax.axis_index("s") * SC_CORES + lax.axis_index("c")


def _sc_gather(tab, idx):
    p = idx.shape[0]
    w = tab.shape[1]
    n = SC_GATHER_ROWS
    per_w = p // SC_WORKERS
    mesh = plsc.VectorSubcoreMesh(core_axis_name="c", subcore_axis_name="s")

    n_chunks = per_w // n
    assert n_chunks % 2 == 0

    @functools.partial(
        pl.kernel, mesh=mesh, out_type=jax.ShapeDtypeStruct((p, w), U32),
        scratch_types=[pltpu.VMEM((per_w,), I32), pltpu.VMEM((n, w), U32), pltpu.VMEM((n, w), U32)]
        + [pltpu.SemaphoreType.DMA] * 4)
    def gather(tab_hbm, idx_hbm, out_hbm, idx_v, rows0, rows1, gsem0, gsem1, ssem0, ssem1):
        base = pl.multiple_of(_sc_worker_id() * per_w, n)
        bufs = ((rows0, gsem0, ssem0), (rows1, gsem1, ssem1))
        pltpu.sync_copy(idx_hbm.at[pl.ds(base, per_w)], idx_v)

        def fetch(i, b):
            sel = idx_v.at[pl.ds(pl.multiple_of(i * n, n), n)]
            return pltpu.make_async_copy(tab_hbm.at[sel], bufs[b][0], bufs[b][1])

        def store(i, b):
            dst = out_hbm.at[pl.ds(pl.multiple_of(base + i * n, n), n)]
            return pltpu.make_async_copy(bufs[b][0], dst, bufs[b][2])

        fetch(0, 0).start()

        @pl.loop(0, n_chunks, step=2)
        def _(i0):
            for b in range(2):
                i = i0 + b
                fetch(i, b).wait()
                store(i, b).start()

                @pl.when(i >= 1)
                def _():
                    store(i - 1, 1 - b).wait()

                @pl.when(i + 1 < n_chunks)
                def _():
                    fetch(i + 1, 1 - b).start()

        store(n_chunks - 1, 1).wait()

    return gather(tab, idx)


def _sc_weighted_rows(tab, idx, wgt):
    k = PEER_SEL
    n_tok = idx.shape[0] // k
    w = tab.shape[1]
    lanes = SC_LANES
    nvec = w // lanes
    slab = 4
    ch = SC_GATHER_ROWS
    per_w = n_tok // SC_WORKERS
    mesh = plsc.VectorSubcoreMesh(core_axis_name="c", subcore_axis_name="s")

    assert k == 2 * ch and per_w % 2 == 0

    @functools.partial(
        pl.kernel, mesh=mesh, out_type=jax.ShapeDtypeStruct((n_tok, 2 * w), F32),
        scratch_types=[pltpu.VMEM((per_w * k,), I32),
                       pltpu.VMEM((k * lanes,), F32), pltpu.VMEM((k * lanes,), F32),
                       pltpu.VMEM((ch, w), U32), pltpu.VMEM((ch, w), U32),
                       pltpu.VMEM((2 * w,), F32), pltpu.VMEM((2 * w,), F32)]
        + [pltpu.SemaphoreType.DMA] * 6,
        compiler_params=pltpu.CompilerParams(needs_layout_passes=False))
    def weighted(tab_hbm, idx_hbm, w_hbm, out_hbm, idx_v, w_v0, w_v1, rows0, rows1, acc_v0, acc_v1,
                 gsem0, gsem1, wsem0, wsem1, osem0, osem1):
        base = _sc_worker_id() * per_w
        rbuf = ((rows0, gsem0), (rows1, gsem1))
        wbuf = ((w_v0, wsem0), (w_v1, wsem1))
        abuf = ((acc_v0, osem0), (acc_v1, osem1))
        pltpu.sync_copy(idx_hbm.at[pl.ds(pl.multiple_of(base * k, k), per_w * k)], idx_v)

        def fetch_rows(t, c):
            sel = idx_v.at[pl.ds(pl.multiple_of(t * k + c * ch, ch), ch)]
            return pltpu.make_async_copy(tab_hbm.at[sel], rbuf[c][0], rbuf[c][1])

        def fetch_w(t, p):
            src = w_hbm.at[pl.ds(pl.multiple_of((base + t) * (k * lanes), k * lanes), k * lanes)]
            return pltpu.make_async_copy(src, wbuf[p][0], wbuf[p][1])

        def store_out(t, p):
            return pltpu.make_async_copy(abuf[p][0], out_hbm.at[base + t], abuf[p][1])

        def accumulate(c, w_v, rows, acc_v):
            @pl.loop(0, nvec // slab)
            def _(sb):
                col = pl.multiple_of(sb * (slab * lanes), slab * lanes)
                if c == 0:
                    acc = [jnp.zeros((lanes,), F32) for _ in range(2 * slab)]
                else:
                    acc = [acc_v[pl.ds(part * w + col + j * lanes, lanes)]
                           for j in range(slab) for part in range(2)]
                for r in range(ch):
                    wv = w_v[pl.ds((c * ch + r) * lanes, lanes)]
                    for j in range(slab):
                        x = rows[r, pl.ds(col + j * lanes, lanes)]
                        acc[2 * j] += wv * lax.bitcast_convert_type(x << 16, F32)
                        acc[2 * j + 1] += wv * lax.bitcast_convert_type(x & jnp.uint32(0xFFFF0000), F32)
                for j in range(slab):
                    acc_v[pl.ds(col + j * lanes, lanes)] = acc[2 * j]
                    acc_v[pl.ds(w + col + j * lanes, lanes)] = acc[2 * j + 1]

        fetch_rows(0, 0).start()
        fetch_rows(0, 1).start()
        fetch_w(0, 0).start()

        @pl.loop(0, per_w, step=2)
        def _(t0):
            for p in range(2):
                t = t0 + p
                fetch_w(t, p).wait()

                @pl.when(t + 1 < per_w)
                def _():
                    fetch_w(t + 1, 1 - p).start()

                @pl.when(t >= 2)
                def _():
                    store_out(t - 2, p).wait()

                for c in range(2):
                    fetch_rows(t, c).wait()
                    accumulate(c, wbuf[p][0], rbuf[c][0], abuf[p][0])

                    @pl.when(t + 1 < per_w)
                    def _():
                        fetch_rows(t + 1, c).start()

                store_out(t, p).start()

        store_out(per_w - 2, 0).wait()
        store_out(per_w - 1, 1).wait()

    return weighted(tab, idx, wgt)


def _unpack_pair(w32):
    lo = pltpu.bitcast(w32 << 16, F32)
    hi = pltpu.bitcast(w32 & jnp.uint32(0xFFFF0000), F32)
    return lo, hi


def _split3(x):
    hi = x.astype(BF16)
    r1 = x - hi.astype(F32)
    mid = r1.astype(BF16)
    lo = (r1 - mid.astype(F32)).astype(BF16)
    return jnp.concatenate([hi, mid, lo], axis=1)


def _rowsum_all_lanes(x):
    pieces = _split3(x)
    return _bdot(pieces, jnp.ones((pieces.shape[1], x.shape[1]), BF16))


def _peer_score_kernel(ug_ref, hq_ref, g_ref, rep_ref, w_ref, part_ref):
    tb = PEER_TOK_BLOCK
    half = hq_ref.shape[1] // 2
    nl = 128
    h_lo = hq_ref[:, :half]
    h_hi = hq_ref[:, half:]

    def score(k, carry):
        lo, hi = _unpack_pair(ug_ref[pl.ds(pl.multiple_of(k * tb, tb), tb), :])
        p = lo * h_lo + hi * h_hi
        part_ref[k] = sum(p[:, i * nl:(i + 1) * nl] for i in range(half // nl))
        return carry

    lax.fori_loop(0, PEER_SEL, score, 0, unroll=8)
    scb = _rowsum_all_lanes(part_ref[...].reshape(PEER_SEL * tb, nl)).reshape(PEER_SEL, tb, nl)
    diag = (lax.broadcasted_iota(I32, (PEER_SEL, tb, PEER_SEL), 0)
            == lax.broadcasted_iota(I32, (PEER_SEL, tb, PEER_SEL), 2))
    wgt = g_ref[...] * jax.nn.gelu(jnp.sum(jnp.where(diag, scb, 0.0), axis=0))
    w_ref[...] = _bdot(_split3(wgt), rep_ref[...])


def _peer_score(ug, hq, g):
    ntok, d = hq.shape
    tb = PEER_TOK_BLOCK
    rep = jnp.tile(jnp.repeat(jnp.eye(PEER_SEL, dtype=BF16), SC_LANES, axis=1), (3, 1))
    return pl.pallas_call(
        _peer_score_kernel,
        grid=(ntok // tb,),
        in_specs=[pl.BlockSpec((tb * PEER_SEL, ug.shape[1]), lambda i: (i, 0)),
                  pl.BlockSpec((tb, d), lambda i: (i, 0)),
                  pl.BlockSpec((tb, PEER_SEL), lambda i: (i, 0)),
                  pl.BlockSpec(rep.shape, lambda i: (0, 0))],
        out_specs=pl.BlockSpec((tb, PEER_SEL * SC_LANES), lambda i: (i, 0)),
        out_shape=jax.ShapeDtypeStruct((ntok, PEER_SEL * SC_LANES), F32),
        scratch_shapes=[pltpu.VMEM((PEER_SEL, tb, 128), F32)],
        compiler_params=_cparams("parallel"),
        name="peer_score",
    )(ug, hq, g, rep)


def _resid_kernel(x_ref, y_ref, gate_ref, o_ref):
    o_ref[...] = x_ref[...] + gate_ref[0] * y_ref[...]


def _resid(x, y, gate, tok0, seq):
    t, d = x.shape
    ntok = y.shape[0]
    tm = min(512, ntok, seq)
    t0 = tok0 // tm
    per_seq = seq // tm
    tok = pl.BlockSpec((tm, d), lambda i: (i + t0, 0))
    return pl.pallas_call(
        _resid_kernel,
        grid=(ntok // tm,),
        in_specs=[tok, pl.BlockSpec((tm, d), lambda i: (i, 0)),
                  pl.BlockSpec((1, 1, d), lambda i: ((i + t0) // per_seq, 0, 0))],
        out_specs=tok,
        out_shape=jax.ShapeDtypeStruct((t, d), F32),
        input_output_aliases={0: 0},
        compiler_params=_cparams("parallel"),
        name="peer_resid",
    )(x, y, gate)


def _peer_groups(t):
    return max(1, min(4, t // 2048))


def _peer_ffn(x, nw, shift, scale, gate, w_q, keys, tab_u, tab_v):
    b, s, d = x.shape
    t = b * s
    x2 = x.reshape(t, d)
    n_groups = _peer_groups(t)
    gtok = t // n_groups
    tb = PEER_TOK_BLOCK
    staged = []
    for gi in range(n_groups):
        hq, e_t, g_t = _peer_idx(x2, nw, shift, scale, w_q, keys, gi * gtok, gtok, s)
        nblk, _, tm = e_t.shape
        idx_blk = e_t.reshape(nblk, PEER_SEL, tm // tb, tb).transpose(0, 2, 1, 3).reshape(-1)
        idx_tok = e_t.transpose(0, 2, 1).reshape(-1)
        g = g_t.transpose(0, 2, 1).reshape(gtok, PEER_SEL)
        staged.append((hq, idx_tok, g, _sc_gather(tab_u, idx_blk)))
    ys = [_sc_weighted_rows(tab_v, idx_tok, _peer_score(ug, hq, g).reshape(-1)) for hq, idx_tok, g, ug in staged]
    for gi, y in enumerate(ys):
        x2 = _resid(x2, y, gate, gi * gtok, s)
    return x2.reshape(b, s, d)


def _rms_kernel(x_ref, w_ref, o_ref):
    x = x_ref[...]
    o_ref[...] = x * lax.rsqrt(jnp.mean(x * x, axis=-1, keepdims=True) + RMS_EPS) * w_ref[...]


def _final_norm(x, w):
    b, s, d = x.shape
    t = b * s
    tm = min(512, t)
    out = pl.pallas_call(
        _rms_kernel,
        grid=(t // tm,),
        in_specs=[pl.BlockSpec((tm, d), lambda i: (i, 0)), pl.BlockSpec((1, d), lambda i: (0, 0))],
        out_specs=pl.BlockSpec((tm, d), lambda i: (i, 0)),
        out_shape=jax.ShapeDtypeStruct((t, d), F32),
        compiler_params=_cparams("parallel"),
        name="final_norm",
    )(x.reshape(t, d), w.reshape(1, d))
    return out.reshape(b, s, d)


def _rope_tables(s):
    t = jnp.arange(s)
    row = (t // GRID_W).astype(F32)
    col = (t % GRID_W).astype(F32)
    half = HEAD_DIM // 2
    inv = ROPE_THETA ** (-jnp.arange(0, half, 2, dtype=F32) / half)
    ar = row[:, None] * inv
    ac = col[:, None] * inv
    ang = jnp.concatenate([ar, ar, ac, ac], axis=1)
    q = half // 2
    sign = jnp.concatenate([-jnp.ones(q), jnp.ones(q), -jnp.ones(q), jnp.ones(q)]).astype(F32)
    reps = QK_WIDTH // HEAD_DIM
    return jnp.tile(jnp.cos(ang), (1, reps)), jnp.tile(jnp.sin(ang) * sign, (1, reps))


def _rope_partner_cols():
    j = jnp.arange(QK_WIDTH)
    dd = j % HEAD_DIM
    q = HEAD_DIM // 4
    return j - dd + jnp.where((dd % (2 * q)) < q, dd + q, dd - q)


def _dft_tables(n, dtype):
    j = jnp.arange(n, dtype=I32)
    ang = ((j[:, None] * j[None, :]) % n).astype(F32) * (2.0 * math.pi / n)
    return jnp.cos(ang).astype(dtype), jnp.sin(ang).astype(dtype)


def _pack_table(tab):
    half = tab.shape[1] // 2
    bits = lax.bitcast_convert_type(tab.astype(BF16), jnp.uint16).astype(U32)
    return bits[:, :half] | (bits[:, half:] << 16)


def kernel(x, c, ctx, c_ctx, ada_w, ada_b, norm_mix_w, norm_ffn_w, ab_w_in, ab_w_out, attn_sink, gs_w_in, gs_norm_w,
           gs_w_s, gs_b_s, gs_w_out, peer_w_q, peer_keys, peer_u, peer_v, norm_out_w):
    bsz, seq, d = x.shape
    n_ctx = ctx.shape[1]
    depth = ada_w.shape[0]

    cond = jnp.concatenate([c, c_ctx[None], jnp.zeros((16 - bsz - 1, d), F32)], axis=0)
    mods = _ada_all(cond, ada_w, ada_b)

    cos, sin = _rope_tables(seq)
    partner = _rope_partner_cols()
    dft_cc, dft_cs = _dft_tables(FNET_GROUP_DIM, BF16)
    dft_c = jnp.concatenate([dft_cc, dft_cs], axis=1)
    pos_l = _dft_tables(seq, BF16)
    pos_c = _dft_tables(n_ctx, BF16)

    last_attn = ((depth - 1) // 2) * 2
    for i in range(depth):
        j = i // 2
        is_ab = i % 2 == 0
        upd_ctx = i < last_attn
        need_ctx = upd_ctx or is_ab
        m_l = [m[:, None, :] for m in jnp.split(mods[i, :bsz], N_MOD, axis=-1)]
        m_c = [jnp.broadcast_to(m[None], (bsz, 1, d)) for m in jnp.split(mods[i, bsz:bsz + 1], N_MOD, axis=-1)]
        nw_m = norm_mix_w[i].reshape(1, d)
        nw_f = norm_ffn_w[i].reshape(1, d)
        y_ctx = None
        if is_ab:
            w_in = ab_w_in[j]
            w_ext = jnp.concatenate([w_in, w_in[:, partner]], axis=1).astype(BF16)
            w_out = ab_w_out[j].astype(BF16)
            sink = attn_sink[j].reshape(1, N_Q_HEADS)
            q_l, k_l, v_l, fa_l, fb_l = _ab_in(x, nw_m, m_l[0], m_l[1], w_ext, cos, sin, dft_c, True)
            q_c, k_c, v_c, fa_c, fb_c = _ab_in(ctx, nw_m, m_c[0], m_c[1], w_ext, cos[:n_ctx], sin[:n_ctx], dft_c, False)
            a_l = _attn_local(q_l, k_l, v_l, k_c, v_c, sink)
            x = _ab_out(a_l, _dft_pos(fa_l, fb_l, *pos_l), w_out, x, m_l[2])
            if upd_ctx:
                a_c = _attn_ctx(q_c, k_c, v_c, sink)
                ctx = _ab_out(a_c, _dft_pos(fa_c, fb_c, *pos_c), w_out, ctx, m_c[2])
        else:
            sgu_w = (gs_w_in[j].astype(BF16), gs_norm_w[j].reshape(1, -1), gs_w_s[j].astype(BF16), gs_b_s[j].T,
                     gs_w_out[j].astype(BF16))
            x = _sgu(x, nw_m, m_l[0], m_l[1], m_l[2], *sgu_w)
            if upd_ctx:
                ctx = _sgu(ctx, nw_m, m_c[0], m_c[1], m_c[2], *sgu_w)
        w_q = peer_w_q[i].astype(BF16)
        keys = peer_keys[i].reshape(PEER_HEADS * 2, PEER_NKEYS, PEER_HALF_DIM).astype(BF16)
        tab_u = _pack_table(peer_u[i])
        tab_v = _pack_table(peer_v[i])
        x = _peer_ffn(x, nw_f, m_l[3], m_l[4], m_l[5], w_q, keys, tab_u, tab_v)
        if upd_ctx:
            ctx = _peer_ffn(ctx, nw_f, m_c[3], m_c[4], m_c[5], w_q, keys, tab_u, tab_v)
        del need_ctx, y_ctx
    return _final_norm(x, norm_out_w)
```

```python
import functools
import math

import jax
import jax.numpy as jnp
from jax import lax
from jax.experimental import pallas as pl
from jax.experimental.pallas import tpu as pltpu
from jax.experimental.pallas import tpu_sc as plsc

F32 = jnp.float32
BF16 = jnp.bfloat16
I32 = jnp.int32
U32 = jnp.uint32

GRID_W = 64
N_Q_HEADS = 8
N_KV_HEADS = 2
Q_PER_KV = N_Q_HEADS // N_KV_HEADS
HEAD_DIM = 64
WINDOW = 128
ROPE_THETA = 10000.0
ATTN_WIDTH = N_Q_HEADS * HEAD_DIM
KV_WIDTH = N_KV_HEADS * HEAD_DIM
QK_WIDTH = ATTN_WIDTH + KV_WIDTH
FNET_GROUPS = 4
FNET_GROUP_DIM = 128
FNET_WIDTH = FNET_GROUPS * FNET_GROUP_DIM
AB_IN_WIDTH = ATTN_WIDTH + 2 * KV_WIDTH + FNET_WIDTH
SGU_GROUPS = 8
SGU_GROUP_DIM = 128
SGU_WIDTH = SGU_GROUPS * SGU_GROUP_DIM
SGU_CHUNK = 128
PEER_HEADS = 8
PEER_NKEYS = 128
PEER_TOPK = 16
PEER_HALF_DIM = 128
PEER_SEL = PEER_HEADS * PEER_TOPK
N_MOD = 6
RMS_EPS = 1e-6
NEG_INF = -1e30

SC_CORES = 2
SC_SUBCORES = 16
SC_WORKERS = SC_CORES * SC_SUBCORES
SC_LANES = 16
SC_GATHER_ROWS = 64
VMEM_LIMIT = 48 * 1024 * 1024


def _cparams(*sem):
    return pltpu.CompilerParams(dimension_semantics=sem, vmem_limit_bytes=VMEM_LIMIT)


def _norm_mod(x, nw, shift, scale):
    y = x * lax.rsqrt(jnp.mean(x * x, axis=-1, keepdims=True) + RMS_EPS) * nw
    return y * (1.0 + scale) + shift


def _bdot(a, b):
    return jnp.dot(a, b, preferred_element_type=F32)


def _ada_kernel(c_ref, w_ref, b_ref, o_ref):
    c = c_ref[...]
    s = (c * jax.nn.sigmoid(c)).astype(BF16)
    o_ref[0] = _bdot(s, w_ref[0].astype(BF16)) + b_ref[0]


def _ada_all(cond, ada_w, ada_b):
    depth, d, n = ada_w.shape
    r = cond.shape[0]
    tn = 1536
    return pl.pallas_call(
        _ada_kernel,
        grid=(depth, n // tn),
        in_specs=[pl.BlockSpec((r, d), lambda l, j: (0, 0)),
                  pl.BlockSpec((1, d, tn), lambda l, j: (l, 0, j)),
                  pl.BlockSpec((1, 1, tn), lambda l, j: (l, 0, j))],
        out_specs=pl.BlockSpec((1, r, tn), lambda l, j: (l, 0, j)),
        out_shape=jax.ShapeDtypeStruct((depth, r, n), F32),
        compiler_params=_cparams("parallel", "parallel"),
        name="ada",
    )(cond, ada_w, ada_b.reshape(depth, 1, n))


def _ab_in_kernel(x_ref, nw_ref, sh_ref, sc_ref, w_ref, cos_ref, sin_ref, dft_ref,
                  q_ref, k_ref, v_ref, fa_ref, fb_ref, *, rope):
    h = _norm_mod(x_ref[0], nw_ref[...], sh_ref[0], sc_ref[0]).astype(BF16)
    p = _bdot(h, w_ref[...])
    qk = p[:, :QK_WIDTH]
    if rope:
        qk = qk * cos_ref[...] + p[:, AB_IN_WIDTH:] * sin_ref[...]
    q_ref[0] = qk[:, :ATTN_WIDTH].astype(BF16)
    k_ref[0] = qk[:, ATTN_WIDTH:].astype(BF16)
    v_ref[0] = p[:, QK_WIDTH:QK_WIDTH + KV_WIDTH].astype(BF16)
    f0 = QK_WIDTH + KV_WIDTH
    for g in range(FNET_GROUPS):
        fg = p[:, f0 + g * FNET_GROUP_DIM:f0 + (g + 1) * FNET_GROUP_DIM].astype(BF16)
        ab = _bdot(fg, dft_ref[...])
        fa_ref[0, :, g * FNET_GROUP_DIM:(g + 1) * FNET_GROUP_DIM] = ab[:, :FNET_GROUP_DIM].astype(BF16)
        fb_ref[0, :, g * FNET_GROUP_DIM:(g + 1) * FNET_GROUP_DIM] = ab[:, FNET_GROUP_DIM:].astype(BF16)


def _ab_in(x, nw, shift, scale, w_ext, cos, sin, dft_c, rope):
    b, s, d = x.shape
    tm = min(512, s)
    n_ext = w_ext.shape[1]
    outs = [jax.ShapeDtypeStruct((b, s, ATTN_WIDTH), BF16), jax.ShapeDtypeStruct((b, s, KV_WIDTH), BF16),
            jax.ShapeDtypeStruct((b, s, KV_WIDTH), BF16), jax.ShapeDtypeStruct((b, s, FNET_WIDTH), BF16),
            jax.ShapeDtypeStruct((b, s, FNET_WIDTH), BF16)]
    tok = lambda w: pl.BlockSpec((1, tm, w), lambda i, j: (i, j, 0))
    return pl.pallas_call(
        functools.partial(_ab_in_kernel, rope=rope),
        grid=(b, s // tm),
        in_specs=[tok(d),
                  pl.BlockSpec((1, d), lambda i, j: (0, 0)),
                  pl.BlockSpec((1, 1, d), lambda i, j: (i, 0, 0)),
                  pl.BlockSpec((1, 1, d), lambda i, j: (i, 0, 0)),
                  pl.BlockSpec((d, n_ext), lambda i, j: (0, 0)),
                  pl.BlockSpec((tm, QK_WIDTH), lambda i, j: (j, 0)),
                  pl.BlockSpec((tm, QK_WIDTH), lambda i, j: (j, 0)),
                  pl.BlockSpec(dft_c.shape, lambda i, j: (0, 0))],
        out_specs=[tok(ATTN_WIDTH), tok(KV_WIDTH), tok(KV_WIDTH), tok(FNET_WIDTH), tok(FNET_WIDTH)],
        out_shape=outs,
        compiler_params=_cparams("parallel", "parallel"),
        name="ab_in",
    )(x, nw, shift, scale, w_ext, cos, sin, dft_c)


def _softmax_pv(qh, kk, vv, valid, sk, scale):
    s = lax.dot_general(qh, kk, (((1,), (1,)), ((), ())), preferred_element_type=F32) * scale
    if valid is not None:
        s = jnp.where(valid, s, NEG_INF)
    m = jnp.maximum(jnp.max(s, axis=-1, keepdims=True), sk)
    p = jnp.exp(s - m)
    inv_den = 1.0 / (jnp.sum(p, axis=-1, keepdims=True) + jnp.exp(sk - m))
    return _bdot((p * inv_den).astype(BF16), vv)


def _attn_local_kernel(q_ref, kp_ref, ko_ref, kn_ref, vp_ref, vo_ref, vn_ref, kc_ref, vc_ref, sink_ref, o_ref,
                       *, seq):
    j = pl.program_id(1)
    blk = q_ref.shape[1]
    n_ctx = kc_ref.shape[1]
    q = q_ref[0]
    kl = jnp.concatenate([kp_ref[0], ko_ref[0], kn_ref[0], kc_ref[0]], axis=0)
    vl = jnp.concatenate([vp_ref[0], vo_ref[0], vn_ref[0], vc_ref[0]], axis=0)
    nk = 3 * blk + n_ctx
    qi = lax.broadcasted_iota(I32, (blk, nk), 0)
    cj = lax.broadcasted_iota(I32, (blk, nk), 1)
    kj = cj - blk
    kpos = j * blk + kj
    valid = (cj >= 3 * blk) | ((jnp.abs(qi - kj) <= WINDOW) & (kpos >= 0) & (kpos < seq))
    scale = HEAD_DIM ** -0.5
    for kvh in range(N_KV_HEADS):
        kk = kl[:, kvh * HEAD_DIM:(kvh + 1) * HEAD_DIM]
        vv = vl[:, kvh * HEAD_DIM:(kvh + 1) * HEAD_DIM]
        for g in range(Q_PER_KV):
            h = kvh * Q_PER_KV + g
            o = _softmax_pv(q[:, h * HEAD_DIM:(h + 1) * HEAD_DIM], kk, vv, valid, sink_ref[0:1, h:h + 1], scale)
            o_ref[0, :, h * HEAD_DIM:(h + 1) * HEAD_DIM] = o.astype(BF16)


def _attn_local(q, k, v, kc, vc, sink):
    b, s, _ = q.shape
    c = kc.shape[1]
    blk = WINDOW
    nb = s // blk
    qspec = pl.BlockSpec((1, blk, ATTN_WIDTH), lambda i, j: (i, j, 0))
    prev = pl.BlockSpec((1, blk, KV_WIDTH), lambda i, j: (i, jnp.maximum(j - 1, 0), 0))
    own = pl.BlockSpec((1, blk, KV_WIDTH), lambda i, j: (i, j, 0))
    nxt = pl.BlockSpec((1, blk, KV_WIDTH), lambda i, j: (i, jnp.minimum(j + 1, nb - 1), 0))
    cspec = pl.BlockSpec((1, c, KV_WIDTH), lambda i, j: (i, 0, 0))
    return pl.pallas_call(
        functools.partial(_attn_local_kernel, seq=s),
        grid=(b, nb),
        in_specs=[qspec, prev, own, nxt, prev, own, nxt, cspec, cspec,
                  pl.BlockSpec((1, N_Q_HEADS), lambda i, j: (0, 0))],
        out_specs=qspec,
        out_shape=jax.ShapeDtypeStruct((b, s, ATTN_WIDTH), BF16),
        compiler_params=_cparams("parallel", "parallel"),
        name="attn_local",
    )(q, k, k, k, v, v, v, kc, vc, sink)


def _attn_ctx_kernel(q_ref, k_ref, v_ref, sink_ref, o_ref):
    q = q_ref[0]
    scale = HEAD_DIM ** -0.5
    for kvh in range(N_KV_HEADS):
        kk = k_ref[0][:, kvh * HEAD_DIM:(kvh + 1) * HEAD_DIM]
        vv = v_ref[0][:, kvh * HEAD_DIM:(kvh + 1) * HEAD_DIM]
        for g in range(Q_PER_KV):
            h = kvh * Q_PER_KV + g
            o = _softmax_pv(q[:, h * HEAD_DIM:(h + 1) * HEAD_DIM], kk, vv, None, sink_ref[0:1, h:h + 1], scale)
            o_ref[0, :, h * HEAD_DIM:(h + 1) * HEAD_DIM] = o.astype(BF16)


def _attn_ctx(q, k, v, sink):
    b, c, _ = q.shape
    return pl.pallas_call(
        _attn_ctx_kernel,
        grid=(b,),
        in_specs=[pl.BlockSpec((1, c, ATTN_WIDTH), lambda i: (i, 0, 0)),
                  pl.BlockSpec((1, c, KV_WIDTH), lambda i: (i, 0, 0)),
                  pl.BlockSpec((1, c, KV_WIDTH), lambda i: (i, 0, 0)),
                  pl.BlockSpec((1, N_Q_HEADS), lambda i: (0, 0))],
        out_specs=pl.BlockSpec((1, c, ATTN_WIDTH), lambda i: (i, 0, 0)),
        out_shape=jax.ShapeDtypeStruct((b, c, ATTN_WIDTH), BF16),
        compiler_params=_cparams("parallel"),
        name="attn_ctx",
    )(q, k, v, sink)


def _dft_pos_kernel(cs_ref, ss_ref, a_ref, b_ref, o_ref, acc_ref, *, scale):
    kk = pl.program_id(2)

    @pl.when(kk == 0)
    def _():
        acc_ref[...] = jnp.zeros_like(acc_ref)

    acc_ref[...] += _bdot(cs_ref[...], a_ref[0]) - _bdot(ss_ref[...], b_ref[0])

    @pl.when(kk == pl.num_programs(2) - 1)
    def _():
        o_ref[0] = (acc_ref[...] * scale).astype(BF16)


def _dft_pos(fa, fb, cs, ss):
    b, s, w = fa.shape
    t = min(1024, s)
    scale = 1.0 / math.sqrt(s * FNET_GROUP_DIM)
    return pl.pallas_call(
        functools.partial(_dft_pos_kernel, scale=scale),
        grid=(b, s // t, s // t),
        in_specs=[pl.BlockSpec((t, t), lambda i, m, k: (m, k)),
                  pl.BlockSpec((t, t), lambda i, m, k: (m, k)),
                  pl.BlockSpec((1, t, w), lambda i, m, k: (i, k, 0)),
                  pl.BlockSpec((1, t, w), lambda i, m, k: (i, k, 0))],
        out_specs=pl.BlockSpec((1, t, w), lambda i, m, k: (i, m, 0)),
        out_shape=jax.ShapeDtypeStruct((b, s, w), BF16),
        scratch_shapes=[pltpu.VMEM((t, w), F32)],
        compiler_params=_cparams("parallel", "parallel", "arbitrary"),
        name="dft_pos",
    )(cs, ss, fa, fb)


def _ab_out_kernel(a_ref, f_ref, w1_ref, w2_ref, x_ref, g_ref, o_ref):
    y = _bdot(a_ref[0], w1_ref[...]) + _bdot(f_ref[0], w2_ref[...])
    o_ref[0] = x_ref[0] + g_ref[0] * y


def _ab_out(a, fm, w_out, x, gate):
    b, s, d = x.shape
    tm = min(512, s)
    tok = lambda w: pl.BlockSpec((1, tm, w), lambda i, j: (i, j, 0))
    return pl.pallas_call(
        _ab_out_kernel,
        grid=(b, s // tm),
        in_specs=[tok(ATTN_WIDTH), tok(FNET_WIDTH),
                  pl.BlockSpec((ATTN_WIDTH, d), lambda i, j: (0, 0)),
                  pl.BlockSpec((FNET_WIDTH, d), lambda i, j: (0, 0)),
                  tok(d),
                  pl.BlockSpec((1, 1, d), lambda i, j: (i, 0, 0))],
        out_specs=tok(d),
        out_shape=jax.ShapeDtypeStruct((b, s, d), F32),
        compiler_params=_cparams("parallel", "parallel"),
        name="ab_out",
    )(a, fm, w_out[:ATTN_WIDTH], w_out[ATTN_WIDTH:], x, gate)


def _sgu_kernel(x_ref, nw_ref, sh_ref, sc_ref, g_ref, win_ref, gnw_ref, ws_ref, bs_ref, wout_ref, o_ref, uv_ref):
    x = x_ref[0]
    tm = x.shape[0]
    h = _norm_mod(x, nw_ref[...], sh_ref[0], sc_ref[0]).astype(BF16)
    z = jax.nn.gelu(_bdot(h, win_ref[...]))
    u = z[:, :SGU_WIDTH]
    v = z[:, SGU_WIDTH:]
    v = (v * lax.rsqrt(jnp.mean(v * v, axis=-1, keepdims=True) + RMS_EPS) * gnw_ref[...]).astype(BF16)
    for n in range(tm // SGU_CHUNK):
        r = slice(n * SGU_CHUNK, (n + 1) * SGU_CHUNK)
        for g in range(SGU_GROUPS):
            cs = slice(g * SGU_GROUP_DIM, (g + 1) * SGU_GROUP_DIM)
            sv = _bdot(ws_ref[g], v[r, cs]) + bs_ref[:, g:g + 1]
            uv_ref[r, cs] = (u[r, cs] * sv).astype(BF16)
    o_ref[0] = x + g_ref[0] * _bdot(uv_ref[...], wout_ref[...])


def _sgu(x, nw, shift, scale, gate, w_in, gnw, w_s, b_s_t, w_out):
    b, s, d = x.shape
    tm = min(256, s)
    tok = pl.BlockSpec((1, tm, d), lambda i, j: (i, j, 0))
    mod = pl.BlockSpec((1, 1, d), lambda i, j: (i, 0, 0))
    full = lambda a: pl.BlockSpec(a.shape, lambda i, j: (0,) * a.ndim)
    return pl.pallas_call(
        _sgu_kernel,
        grid=(b, s // tm),
        in_specs=[tok, full(nw), mod, mod, mod, full(w_in), full(gnw), full(w_s), full(b_s_t), full(w_out)],
        out_specs=tok,
        out_shape=jax.ShapeDtypeStruct((b, s, d), F32),
        scratch_shapes=[pltpu.VMEM((tm, SGU_WIDTH), BF16)],
        compiler_params=_cparams("parallel", "parallel"),
        name="sgu",
    )(x, nw, shift, scale, gate, w_in, gnw, w_s, b_s_t, w_out)


def _topk_rows(s, rid, k):
    big = jnp.iinfo(jnp.int32).max
    vals, idxs = [], []
    for _ in range(k):
        m = jnp.max(s, axis=0, keepdims=True)
        am = jnp.min(jnp.where(s == m, rid, big), axis=0, keepdims=True)
        vals.append(m)
        idxs.append(am)
        s = jnp.where(rid == am, -jnp.inf, s)
    return jnp.concatenate(vals, axis=0), jnp.concatenate(idxs, axis=0)


_PAIR_PIECES = ((0, 1, 0, 16), (1, 1, 0, 8), (2, 1, 0, 8), (3, 1, 0, 8),
                (8, 8, 0, 1), (0, 8, 0, 1), (0, 8, 1, 1), (0, 8, 2, 1))
_PAIR_ROW_PIECES = 4


def _pair_candidates(t0, t1):
    k = PEER_TOPK
    n = t0.shape[1]
    vals, ids = [], []
    for a0, na, b0, nb in _PAIR_PIECES:
        io = lax.broadcasted_iota(I32, (max(na, nb), n), 0)
        a = a0 + io if na > 1 else jnp.full_like(io, a0)
        b = b0 + io if nb > 1 else jnp.full_like(io, b0)
        ok = (a + 1) * (b + 1) <= k
        if na > 1:
            ok = ok & (a >= _PAIR_ROW_PIECES)
        vals.append(jnp.where(ok, t0[a0:a0 + na] + t1[b0:b0 + nb], -jnp.inf))
        ids.append(a * k + b)
    return jnp.concatenate(vals, axis=0), jnp.concatenate(ids, axis=0)


def _select_rows(sel, table):
    out = jnp.zeros(sel.shape, table.dtype)
    for a in range(table.shape[0]):
        out = jnp.where(sel == a, table[a:a + 1], out)
    return out


def _peer_idx_kernel(x_ref, nw_ref, sh_ref, sc_ref, wq_ref, keys_ref, hq_ref, e_ref, g_ref, q_ref):
    hq = _norm_mod(x_ref[...], nw_ref[...], sh_ref[0], sc_ref[0])
    hq_ref[...] = hq
    q_ref[...] = _bdot(hq.astype(BF16), wq_ref[...]).astype(BF16)

    def head(h, carry):
        tops, topi = [], []
        for s in range(2):
            col = pl.multiple_of((h * 2 + s) * PEER_HALF_DIM, PEER_HALF_DIM)
            qs = q_ref[:, pl.ds(col, PEER_HALF_DIM)]
            st = lax.dot_general(keys_ref[h * 2 + s], qs, (((1,), (1,)), ((), ())), preferred_element_type=F32)
            ts, ti = _topk_rows(st, lax.broadcasted_iota(I32, st.shape, 0), PEER_TOPK)
            tops.append(ts)
            topi.append(ti)
        best_s, best = _topk_rows(*_pair_candidates(tops[0], tops[1]), PEER_TOPK)
        i1 = _select_rows(best // PEER_TOPK, topi[0])
        i2 = _select_rows(best % PEER_TOPK, topi[1])
        p = jnp.exp(best_s - best_s[0:1])
        row = pl.multiple_of(h * PEER_TOPK, PEER_TOPK)
        e_ref[0, pl.ds(row, PEER_TOPK), :] = i1 * PEER_NKEYS + i2
        g_ref[0, pl.ds(row, PEER_TOPK), :] = p / jnp.sum(p, axis=0, keepdims=True)
        return carry

    lax.fori_loop(0, PEER_HEADS, head, 0)


def _peer_idx(x, nw, shift, scale, w_q, keys, tok0, ntok, seq):
    d = x.shape[1]
    tm = min(256, ntok, seq)
    nq = w_q.shape[1]
    t0 = tok0 // tm
    per_seq = seq // tm
    mod = pl.BlockSpec((1, 1, d), lambda i: ((i + t0) // per_seq, 0, 0))
    sel = pl.BlockSpec((1, PEER_SEL, tm), lambda i: (i, 0, 0))
    return pl.pallas_call(
        _peer_idx_kernel,
        grid=(ntok // tm,),
        in_specs=[pl.BlockSpec((tm, d), lambda i: (i + t0, 0)),
                  pl.BlockSpec((1, d), lambda i: (0, 0)),
                  mod, mod,
                  pl.BlockSpec((d, nq), lambda i: (0, 0)),
                  pl.BlockSpec(keys.shape, lambda i: (0, 0, 0))],
        out_specs=[pl.BlockSpec((tm, d), lambda i: (i, 0)), sel, sel],
        out_shape=[jax.ShapeDtypeStruct((ntok, d), F32),
                   jax.ShapeDtypeStruct((ntok // tm, PEER_SEL, tm), I32),
                   jax.ShapeDtypeStruct((ntok // tm, PEER_SEL, tm), F32)],
        scratch_shapes=[pltpu.VMEM((tm, nq), BF16)],
        compiler_params=_cparams("parallel"),
        name="peer_idx",
    )(x, nw, shift, scale, w_q, keys)


def _sc_worker_id():
    return lax.axis_index("s") * SC_CORES + lax.axis_index("c")


SC_SLAB = 4


def _sc_unpack(x):
    return lax.bitcast_convert_type(x << 16, F32), lax.bitcast_convert_type(x & jnp.uint32(0xFFFF0000), F32)


def _sc_per_token(tab, idx, aux, out_width, chunk_fn):
    k = PEER_SEL
    n_tok, a = aux.shape
    w = tab.shape[1]
    ch = SC_GATHER_ROWS
    per_w = n_tok // SC_WORKERS
    assert k == 2 * ch and per_w % 2 == 0
    mesh = plsc.VectorSubcoreMesh(core_axis_name="c", subcore_axis_name="s")

    @functools.partial(
        pl.kernel, mesh=mesh, out_type=jax.ShapeDtypeStruct((n_tok, out_width), F32),
        scratch_types=[pltpu.VMEM((per_w * k,), I32),
                       pltpu.VMEM((a,), F32), pltpu.VMEM((a,), F32),
                       pltpu.VMEM((ch, w), U32), pltpu.VMEM((ch, w), U32),
                       pltpu.VMEM((out_width,), F32), pltpu.VMEM((out_width,), F32)]
        + [pltpu.SemaphoreType.DMA] * 6,
        compiler_params=pltpu.CompilerParams(needs_layout_passes=False))
    def per_token(tab_hbm, idx_hbm, aux_hbm, out_hbm, idx_v, aux_v0, aux_v1, rows0, rows1, out_v0, out_v1,
                  gsem0, gsem1, asem0, asem1, osem0, osem1):
        base = _sc_worker_id() * per_w
        rbuf = ((rows0, gsem0), (rows1, gsem1))
        abuf = ((aux_v0, asem0), (aux_v1, asem1))
        obuf = ((out_v0, osem0), (out_v1, osem1))
        pltpu.sync_copy(idx_hbm.at[pl.ds(pl.multiple_of(base * k, k), per_w * k)], idx_v)

        def fetch_rows(t, c):
            sel = idx_v.at[pl.ds(pl.multiple_of(t * k + c * ch, ch), ch)]
            return pltpu.make_async_copy(tab_hbm.at[sel], rbuf[c][0], rbuf[c][1])

        def fetch_aux(t, p):
            return pltpu.make_async_copy(aux_hbm.at[base + t], abuf[p][0], abuf[p][1])

        def store_out(t, p):
            return pltpu.make_async_copy(obuf[p][0], out_hbm.at[base + t], obuf[p][1])

        fetch_rows(0, 0).start()
        fetch_rows(0, 1).start()
        fetch_aux(0, 0).start()

        @pl.loop(0, per_w, step=2)
        def _(t0):
            for p in range(2):
                t = t0 + p
                fetch_aux(t, p).wait()

                @pl.when(t + 1 < per_w)
                def _():
                    fetch_aux(t + 1, 1 - p).start()

                @pl.when(t >= 2)
                def _():
                    store_out(t - 2, p).wait()

                for c in range(2):
                    fetch_rows(t, c).wait()
                    chunk_fn(c, abuf[p][0], rbuf[c][0], obuf[p][0])

                    @pl.when(t + 1 < per_w)
                    def _():
                        fetch_rows(t + 1, c).start()

                store_out(t, p).start()

        store_out(per_w - 2, 0).wait()
        store_out(per_w - 1, 1).wait()

    return per_token(tab, idx, aux)


def _sc_row_dots(tab, idx, hq):
    w = tab.shape[1]
    lanes, slab, ch = SC_LANES, SC_SLAB, SC_GATHER_ROWS

    def chunk(c, h_v, rows, part_v):
        for r in range(ch):
            part_v[pl.ds((c * ch + r) * lanes, lanes)] = jnp.zeros((lanes,), F32)

        @pl.loop(0, w // lanes // slab)
        def _(sb):
            col = pl.multiple_of(sb * (slab * lanes), slab * lanes)
            h_lo = [h_v[pl.ds(col + j * lanes, lanes)] for j in range(slab)]
            h_hi = [h_v[pl.ds(w + col + j * lanes, lanes)] for j in range(slab)]
            for r in range(ch):
                s = None
                for j in range(slab):
                    lo, hi = _sc_unpack(rows[r, pl.ds(col + j * lanes, lanes)])
                    term = h_lo[j] * lo + h_hi[j] * hi
                    s = term if s is None else s + term
                plsc.addupdate(part_v.at[pl.ds((c * ch + r) * lanes, lanes)], s)

    return _sc_per_token(tab, idx, hq, PEER_SEL * lanes, chunk)


def _sc_weighted_rows(tab, idx, w_rep):
    w = tab.shape[1]
    lanes, slab, ch = SC_LANES, SC_SLAB, SC_GATHER_ROWS

    def chunk(c, w_v, rows, acc_v):
        @pl.loop(0, w // lanes // slab)
        def _(sb):
            col = pl.multiple_of(sb * (slab * lanes), slab * lanes)
            if c == 0:
                acc = [jnp.zeros((lanes,), F32) for _ in range(2 * slab)]
            else:
                acc = [acc_v[pl.ds(part * w + col + j * lanes, lanes)] for j in range(slab) for part in range(2)]
            for r in range(ch):
                wv = w_v[pl.ds((c * ch + r) * lanes, lanes)]
                for j in range(slab):
                    lo, hi = _sc_unpack(rows[r, pl.ds(col + j * lanes, lanes)])
                    acc[2 * j] += wv * lo
                    acc[2 * j + 1] += wv * hi
            for j in range(slab):
                acc_v[pl.ds(col + j * lanes, lanes)] = acc[2 * j]
                acc_v[pl.ds(w + col + j * lanes, lanes)] = acc[2 * j + 1]

    return _sc_per_token(tab, idx, w_rep, 2 * w, chunk)


def _split3(x):
    hi = x.astype(BF16)
    r1 = x - hi.astype(F32)
    mid = r1.astype(BF16)
    lo = (r1 - mid.astype(F32)).astype(BF16)
    return jnp.concatenate([hi, mid, lo], axis=1)


def _peer_weights_kernel(part_ref, g_ref, fold_ref, rep_ref, w_ref):
    score = _bdot(_split3(part_ref[...]), fold_ref[...])
    wgt = g_ref[...] * jax.nn.gelu(score)
    w_ref[...] = _bdot(_split3(wgt), rep_ref[...])


def _peer_weights(part, g):
    ntok, wide = part.shape
    tm = min(256, ntok)
    rep1 = jnp.repeat(jnp.eye(PEER_SEL, dtype=BF16), SC_LANES, axis=1)
    rep = jnp.tile(rep1, (3, 1))
    fold = jnp.tile(rep1.T, (3, 1))
    row = lambda w: pl.BlockSpec((tm, w), lambda i: (i, 0))
    full = lambda a: pl.BlockSpec(a.shape, lambda i: (0, 0))
    return pl.pallas_call(
        _peer_weights_kernel,
        grid=(ntok // tm,),
        in_specs=[row(wide), row(PEER_SEL), full(fold), full(rep)],
        out_specs=row(wide),
        out_shape=jax.ShapeDtypeStruct((ntok, wide), F32),
        compiler_params=_cparams("parallel"),
        name="peer_weights",
    )(part, g, fold, rep)


def _resid_kernel(x_ref, y_ref, gate_ref, o_ref):
    o_ref[...] = x_ref[...] + gate_ref[0] * y_ref[...]


def _resid(x, y, gate, tok0, seq):
    t, d = x.shape
    ntok = y.shape[0]
    tm = min(512, ntok, seq)
    t0 = tok0 // tm
    per_seq = seq // tm
    tok = pl.BlockSpec((tm, d), lambda i: (i + t0, 0))
    return pl.pallas_call(
        _resid_kernel,
        grid=(ntok // tm,),
        in_specs=[tok, pl.BlockSpec((tm, d), lambda i: (i, 0)),
                  pl.BlockSpec((1, 1, d), lambda i: ((i + t0) // per_seq, 0, 0))],
        out_specs=tok,
        out_shape=jax.ShapeDtypeStruct((t, d), F32),
        input_output_aliases={0: 0},
        compiler_params=_cparams("parallel"),
        name="peer_resid",
    )(x, y, gate)


def _peer_groups(t):
    return max(1, min(4, t // 2048))


def _peer_ffn(x, nw, shift, scale, gate, w_q, keys, tab_u, tab_v):
    b, s, d = x.shape
    t = b * s
    x2 = x.reshape(t, d)
    n_groups = _peer_groups(t)
    gtok = t // n_groups
    staged = []
    for gi in range(n_groups):
        hq, e_t, g_t = _peer_idx(x2, nw, shift, scale, w_q, keys, gi * gtok, gtok, s)
        idx = e_t.transpose(0, 2, 1).reshape(-1)
        g = g_t.transpose(0, 2, 1).reshape(gtok, PEER_SEL)
        staged.append((idx, g, _sc_row_dots(tab_u, idx, hq)))
    ys = [_sc_weighted_rows(tab_v, idx, _peer_weights(part, g)) for idx, g, part in staged]
    for gi, y in enumerate(ys):
        x2 = _resid(x2, y, gate, gi * gtok, s)
    return x2.reshape(b, s, d)


def _rms_kernel(x_ref, w_ref, o_ref):
    x = x_ref[...]
    o_ref[...] = x * lax.rsqrt(jnp.mean(x * x, axis=-1, keepdims=True) + RMS_EPS) * w_ref[...]


def _final_norm(x, w):
    b, s, d = x.shape
    t = b * s
    tm = min(512, t)
    out = pl.pallas_call(
        _rms_kernel,
        grid=(t // tm,),
        in_specs=[pl.BlockSpec((tm, d), lambda i: (i, 0)), pl.BlockSpec((1, d), lambda i: (0, 0))],
        out_specs=pl.BlockSpec((tm, d), lambda i: (i, 0)),
        out_shape=jax.ShapeDtypeStruct((t, d), F32),
        compiler_params=_cparams("parallel"),
        name="final_norm",
    )(x.reshape(t, d), w.reshape(1, d))
    return out.reshape(b, s, d)


def _rope_tables(s):
    t = jnp.arange(s)
    row = (t // GRID_W).astype(F32)
    col = (t % GRID_W).astype(F32)
    half = HEAD_DIM // 2
    inv = ROPE_THETA ** (-jnp.arange(0, half, 2, dtype=F32) / half)
    ar = row[:, None] * inv
    ac = col[:, None] * inv
    ang = jnp.concatenate([ar, ar, ac, ac], axis=1)
    q = half // 2
    sign = jnp.concatenate([-jnp.ones(q), jnp.ones(q), -jnp.ones(q), jnp.ones(q)]).astype(F32)
    reps = QK_WIDTH // HEAD_DIM
    return jnp.tile(jnp.cos(ang), (1, reps)), jnp.tile(jnp.sin(ang) * sign, (1, reps))


def _rope_partner_cols():
    j = jnp.arange(QK_WIDTH)
    dd = j % HEAD_DIM
    q = HEAD_DIM // 4
    return j - dd + jnp.where((dd % (2 * q)) < q, dd + q, dd - q)


def _dft_tables(n, dtype):
    j = jnp.arange(n, dtype=I32)
    ang = ((j[:, None] * j[None, :]) % n).astype(F32) * (2.0 * math.pi / n)
    return jnp.cos(ang).astype(dtype), jnp.sin(ang).astype(dtype)


def _pack_table(tab):
    half = tab.shape[1] // 2
    bits = lax.bitcast_convert_type(tab.astype(BF16), jnp.uint16).astype(U32)
    return bits[:, :half] | (bits[:, half:] << 16)


def kernel(x, c, ctx, c_ctx, ada_w, ada_b, norm_mix_w, norm_ffn_w, ab_w_in, ab_w_out, attn_sink, gs_w_in, gs_norm_w,
           gs_w_s, gs_b_s, gs_w_out, peer_w_q, peer_keys, peer_u, peer_v, norm_out_w):
    bsz, seq, d = x.shape
    n_ctx = ctx.shape[1]
    depth = ada_w.shape[0]

    cond = jnp.concatenate([c, c_ctx[None], jnp.zeros((16 - bsz - 1, d), F32)], axis=0)
    mods = _ada_all(cond, ada_w, ada_b)

    cos, sin = _rope_tables(seq)
    partner = _rope_partner_cols()
    dft_cc, dft_cs = _dft_tables(FNET_GROUP_DIM, BF16)
    dft_c = jnp.concatenate([dft_cc, dft_cs], axis=1)
    pos_l = _dft_tables(seq, BF16)
    pos_c = _dft_tables(n_ctx, BF16)

    last_attn = ((depth - 1) // 2) * 2
    for i in range(depth):
        j = i // 2
        is_ab = i % 2 == 0
        upd_ctx = i < last_attn
        need_ctx = upd_ctx or is_ab
        m_l = [m[:, None, :] for m in jnp.split(mods[i, :bsz], N_MOD, axis=-1)]
        m_c = [jnp.broadcast_to(m[None], (bsz, 1, d)) for m in jnp.split(mods[i, bsz:bsz + 1], N_MOD, axis=-1)]
        nw_m = norm_mix_w[i].reshape(1, d)
        nw_f = norm_ffn_w[i].reshape(1, d)
        y_ctx = None
        if is_ab:
            w_in = ab_w_in[j]
            w_ext = jnp.concatenate([w_in, w_in[:, partner]], axis=1).astype(BF16)
            w_out = ab_w_out[j].astype(BF16)
            sink = attn_sink[j].reshape(1, N_Q_HEADS)
            q_l, k_l, v_l, fa_l, fb_l = _ab_in(x, nw_m, m_l[0], m_l[1], w_ext, cos, sin, dft_c, True)
            q_c, k_c, v_c, fa_c, fb_c = _ab_in(ctx, nw_m, m_c[0], m_c[1], w_ext, cos[:n_ctx], sin[:n_ctx], dft_c, False)
            a_l = _attn_local(q_l, k_l, v_l, k_c, v_c, sink)
            x = _ab_out(a_l, _dft_pos(fa_l, fb_l, *pos_l), w_out, x, m_l[2])
            if upd_ctx:
                a_c = _attn_ctx(q_c, k_c, v_c, sink)
                ctx = _ab_out(a_c, _dft_pos(fa_c, fb_c, *pos_c), w_out, ctx, m_c[2])
        else:
            sgu_w = (gs_w_in[j].astype(BF16), gs_norm_w[j].reshape(1, -1), gs_w_s[j].astype(BF16), gs_b_s[j].T,
                     gs_w_out[j].astype(BF16))
            x = _sgu(x, nw_m, m_l[0], m_l[1], m_l[2], *sgu_w)
            if upd_ctx:
                ctx = _sgu(ctx, nw_m, m_c[0], m_c[1], m_c[2], *sgu_w)
        w_q = peer_w_q[i].astype(BF16)
        keys = peer_keys[i].reshape(PEER_HEADS * 2, PEER_NKEYS, PEER_HALF_DIM).astype(BF16)
        tab_u = _pack_table(peer_u[i])
        tab_v = _pack_table(peer_v[i])
        x = _peer_ffn(x, nw_f, m_l[3], m_l[4], m_l[5], w_q, keys, tab_u, tab_v)
        if upd_ctx:
            ctx = _peer_ffn(ctx, nw_f, m_c[3], m_c[4], m_c[5], w_q, keys, tab_u, tab_v)
        del need_ctx, y_ctx
    return _final_norm(x, norm_out_w)
```

```python
import functools
import math

import jax
import jax.numpy as jnp
from jax import lax
from jax.experimental import pallas as pl
from jax.experimental.pallas import tpu as pltpu
from jax.experimental.pallas import tpu_sc as plsc

F32 = jnp.float32
BF16 = jnp.bfloat16
I32 = jnp.int32
U32 = jnp.uint32

GRID_W = 64
N_Q_HEADS = 8
N_KV_HEADS = 2
Q_PER_KV = N_Q_HEADS // N_KV_HEADS
HEAD_DIM = 64
WINDOW = 128
ROPE_THETA = 10000.0
ATTN_WIDTH = N_Q_HEADS * HEAD_DIM
KV_WIDTH = N_KV_HEADS * HEAD_DIM
QK_WIDTH = ATTN_WIDTH + KV_WIDTH
FNET_GROUPS = 4
FNET_GROUP_DIM = 128
FNET_WIDTH = FNET_GROUPS * FNET_GROUP_DIM
AB_IN_WIDTH = ATTN_WIDTH + 2 * KV_WIDTH + FNET_WIDTH
SGU_GROUPS = 8
SGU_GROUP_DIM = 128
SGU_WIDTH = SGU_GROUPS * SGU_GROUP_DIM
SGU_CHUNK = 128
PEER_HEADS = 8
PEER_NKEYS = 128
PEER_TOPK = 16
PEER_HALF_DIM = 128
PEER_SEL = PEER_HEADS * PEER_TOPK
N_MOD = 6
RMS_EPS = 1e-6
NEG_INF = -1e30

SC_CORES = 2
SC_SUBCORES = 16
SC_WORKERS = SC_CORES * SC_SUBCORES
SC_LANES = 16
SC_GATHER_ROWS = 64
VMEM_LIMIT = 48 * 1024 * 1024


def _cparams(*sem):
    return pltpu.CompilerParams(dimension_semantics=sem, vmem_limit_bytes=VMEM_LIMIT)


def _norm_mod(x, nw, shift, scale):
    y = x * lax.rsqrt(jnp.mean(x * x, axis=-1, keepdims=True) + RMS_EPS) * nw
    return y * (1.0 + scale) + shift


def _bdot(a, b):
    return jnp.dot(a, b, preferred_element_type=F32)


def _ada_kernel(c_ref, w_ref, b_ref, o_ref):
    c = c_ref[...]
    s = (c * jax.nn.sigmoid(c)).astype(BF16)
    o_ref[0] = _bdot(s, w_ref[0].astype(BF16)) + b_ref[0]


def _ada_all(cond, ada_w, ada_b):
    depth, d, n = ada_w.shape
    r = cond.shape[0]
    tn = 1536
    return pl.pallas_call(
        _ada_kernel,
        grid=(depth, n // tn),
        in_specs=[pl.BlockSpec((r, d), lambda l, j: (0, 0)),
                  pl.BlockSpec((1, d, tn), lambda l, j: (l, 0, j)),
                  pl.BlockSpec((1, 1, tn), lambda l, j: (l, 0, j))],
        out_specs=pl.BlockSpec((1, r, tn), lambda l, j: (l, 0, j)),
        out_shape=jax.ShapeDtypeStruct((depth, r, n), F32),
        compiler_params=_cparams("parallel", "parallel"),
        name="ada",
    )(cond, ada_w, ada_b.reshape(depth, 1, n))


def _ab_in_kernel(x_ref, nw_ref, sh_ref, sc_ref, w_ref, cos_ref, sin_ref, dft_ref,
                  q_ref, k_ref, v_ref, fa_ref, fb_ref, *, rope):
    h = _norm_mod(x_ref[0], nw_ref[...], sh_ref[0], sc_ref[0]).astype(BF16)
    p = _bdot(h, w_ref[...])
    qk = p[:, :QK_WIDTH]
    if rope:
        qk = qk * cos_ref[...] + p[:, AB_IN_WIDTH:] * sin_ref[...]
    q_ref[0] = qk[:, :ATTN_WIDTH].astype(BF16)
    k_ref[0] = qk[:, ATTN_WIDTH:].astype(BF16)
    v_ref[0] = p[:, QK_WIDTH:QK_WIDTH + KV_WIDTH].astype(BF16)
    f0 = QK_WIDTH + KV_WIDTH
    for g in range(FNET_GROUPS):
        fg = p[:, f0 + g * FNET_GROUP_DIM:f0 + (g + 1) * FNET_GROUP_DIM].astype(BF16)
        ab = _bdot(fg, dft_ref[...])
        fa_ref[0, :, g * FNET_GROUP_DIM:(g + 1) * FNET_GROUP_DIM] = ab[:, :FNET_GROUP_DIM].astype(BF16)
        fb_ref[0, :, g * FNET_GROUP_DIM:(g + 1) * FNET_GROUP_DIM] = ab[:, FNET_GROUP_DIM:].astype(BF16)


def _ab_in(x, nw, shift, scale, w_ext, cos, sin, dft_c, rope):
    b, s, d = x.shape
    tm = min(512, s)
    n_ext = w_ext.shape[1]
    outs = [jax.ShapeDtypeStruct((b, s, ATTN_WIDTH), BF16), jax.ShapeDtypeStruct((b, s, KV_WIDTH), BF16),
            jax.ShapeDtypeStruct((b, s, KV_WIDTH), BF16), jax.ShapeDtypeStruct((b, s, FNET_WIDTH), BF16),
            jax.ShapeDtypeStruct((b, s, FNET_WIDTH), BF16)]
    tok = lambda w: pl.BlockSpec((1, tm, w), lambda i, j: (i, j, 0))
    return pl.pallas_call(
        functools.partial(_ab_in_kernel, rope=rope),
        grid=(b, s // tm),
        in_specs=[tok(d),
                  pl.BlockSpec((1, d), lambda i, j: (0, 0)),
                  pl.BlockSpec((1, 1, d), lambda i, j: (i, 0, 0)),
                  pl.BlockSpec((1, 1, d), lambda i, j: (i, 0, 0)),
                  pl.BlockSpec((d, n_ext), lambda i, j: (0, 0)),
                  pl.BlockSpec((tm, QK_WIDTH), lambda i, j: (j, 0)),
                  pl.BlockSpec((tm, QK_WIDTH), lambda i, j: (j, 0)),
                  pl.BlockSpec(dft_c.shape, lambda i, j: (0, 0))],
        out_specs=[tok(ATTN_WIDTH), tok(KV_WIDTH), tok(KV_WIDTH), tok(FNET_WIDTH), tok(FNET_WIDTH)],
        out_shape=outs,
        compiler_params=_cparams("parallel", "parallel"),
        name="ab_in",
    )(x, nw, shift, scale, w_ext, cos, sin, dft_c)


def _softmax_pv(qh, kk, vv, valid, sk, scale):
    s = lax.dot_general(qh, kk, (((1,), (1,)), ((), ())), preferred_element_type=F32) * scale
    if valid is not None:
        s = jnp.where(valid, s, NEG_INF)
    m = jnp.maximum(jnp.max(s, axis=-1, keepdims=True), sk)
    p = jnp.exp(s - m)
    inv_den = 1.0 / (jnp.sum(p, axis=-1, keepdims=True) + jnp.exp(sk - m))
    return _bdot((p * inv_den).astype(BF16), vv)


def _attn_local_kernel(q_ref, kp_ref, ko_ref, kn_ref, vp_ref, vo_ref, vn_ref, kc_ref, vc_ref, sink_ref, o_ref,
                       *, seq):
    j = pl.program_id(1)
    blk = q_ref.shape[1]
    n_ctx = kc_ref.shape[1]
    q = q_ref[0]
    kl = jnp.concatenate([kp_ref[0], ko_ref[0], kn_ref[0], kc_ref[0]], axis=0)
    vl = jnp.concatenate([vp_ref[0], vo_ref[0], vn_ref[0], vc_ref[0]], axis=0)
    nk = 3 * blk + n_ctx
    qi = lax.broadcasted_iota(I32, (blk, nk), 0)
    cj = lax.broadcasted_iota(I32, (blk, nk), 1)
    kj = cj - blk
    kpos = j * blk + kj
    valid = (cj >= 3 * blk) | ((jnp.abs(qi - kj) <= WINDOW) & (kpos >= 0) & (kpos < seq))
    scale = HEAD_DIM ** -0.5
    for kvh in range(N_KV_HEADS):
        kk = kl[:, kvh * HEAD_DIM:(kvh + 1) * HEAD_DIM]
        vv = vl[:, kvh * HEAD_DIM:(kvh + 1) * HEAD_DIM]
        for g in range(Q_PER_KV):
            h = kvh * Q_PER_KV + g
            o = _softmax_pv(q[:, h * HEAD_DIM:(h + 1) * HEAD_DIM], kk, vv, valid, sink_ref[0:1, h:h + 1], scale)
            o_ref[0, :, h * HEAD_DIM:(h + 1) * HEAD_DIM] = o.astype(BF16)


def _attn_local(q, k, v, kc, vc, sink):
    b, s, _ = q.shape
    c = kc.shape[1]
    blk = WINDOW
    nb = s // blk
    qspec = pl.BlockSpec((1, blk, ATTN_WIDTH), lambda i, j: (i, j, 0))
    prev = pl.BlockSpec((1, blk, KV_WIDTH), lambda i, j: (i, jnp.maximum(j - 1, 0), 0))
    own = pl.BlockSpec((1, blk, KV_WIDTH), lambda i, j: (i, j, 0))
    nxt = pl.BlockSpec((1, blk, KV_WIDTH), lambda i, j: (i, jnp.minimum(j + 1, nb - 1), 0))
    cspec = pl.BlockSpec((1, c, KV_WIDTH), lambda i, j: (i, 0, 0))
    return pl.pallas_call(
        functools.partial(_attn_local_kernel, seq=s),
        grid=(b, nb),
        in_specs=[qspec, prev, own, nxt, prev, own, nxt, cspec, cspec,
                  pl.BlockSpec((1, N_Q_HEADS), lambda i, j: (0, 0))],
        out_specs=qspec,
        out_shape=jax.ShapeDtypeStruct((b, s, ATTN_WIDTH), BF16),
        compiler_params=_cparams("parallel", "parallel"),
        name="attn_local",
    )(q, k, k, k, v, v, v, kc, vc, sink)


def _attn_ctx_kernel(q_ref, k_ref, v_ref, sink_ref, o_ref):
    q = q_ref[0]
    scale = HEAD_DIM ** -0.5
    for kvh in range(N_KV_HEADS):
        kk = k_ref[0][:, kvh * HEAD_DIM:(kvh + 1) * HEAD_DIM]
        vv = v_ref[0][:, kvh * HEAD_DIM:(kvh + 1) * HEAD_DIM]
        for g in range(Q_PER_KV):
            h = kvh * Q_PER_KV + g
            o = _softmax_pv(q[:, h * HEAD_DIM:(h + 1) * HEAD_DIM], kk, vv, None, sink_ref[0:1, h:h + 1], scale)
            o_ref[0, :, h * HEAD_DIM:(h + 1) * HEAD_DIM] = o.astype(BF16)


def _attn_ctx(q, k, v, sink):
    b, c, _ = q.shape
    return pl.pallas_call(
        _attn_ctx_kernel,
        grid=(b,),
        in_specs=[pl.BlockSpec((1, c, ATTN_WIDTH), lambda i: (i, 0, 0)),
                  pl.BlockSpec((1, c, KV_WIDTH), lambda i: (i, 0, 0)),
                  pl.BlockSpec((1, c, KV_WIDTH), lambda i: (i, 0, 0)),
                  pl.BlockSpec((1, N_Q_HEADS), lambda i: (0, 0))],
        out_specs=pl.BlockSpec((1, c, ATTN_WIDTH), lambda i: (i, 0, 0)),
        out_shape=jax.ShapeDtypeStruct((b, c, ATTN_WIDTH), BF16),
        compiler_params=_cparams("parallel"),
        name="attn_ctx",
    )(q, k, v, sink)


def _dft_pos_kernel(cs_ref, ss_ref, a_ref, b_ref, o_ref, acc_ref, *, scale):
    kk = pl.program_id(2)

    @pl.when(kk == 0)
    def _():
        acc_ref[...] = jnp.zeros_like(acc_ref)

    acc_ref[...] += _bdot(cs_ref[...], a_ref[0]) - _bdot(ss_ref[...], b_ref[0])

    @pl.when(kk == pl.num_programs(2) - 1)
    def _():
        o_ref[0] = (acc_ref[...] * scale).astype(BF16)


def _dft_pos(fa, fb, cs, ss):
    b, s, w = fa.shape
    t = min(1024, s)
    scale = 1.0 / math.sqrt(s * FNET_GROUP_DIM)
    return pl.pallas_call(
        functools.partial(_dft_pos_kernel, scale=scale),
        grid=(b, s // t, s // t),
        in_specs=[pl.BlockSpec((t, t), lambda i, m, k: (m, k)),
                  pl.BlockSpec((t, t), lambda i, m, k: (m, k)),
                  pl.BlockSpec((1, t, w), lambda i, m, k: (i, k, 0)),
                  pl.BlockSpec((1, t, w), lambda i, m, k: (i, k, 0))],
        out_specs=pl.BlockSpec((1, t, w), lambda i, m, k: (i, m, 0)),
        out_shape=jax.ShapeDtypeStruct((b, s, w), BF16),
        scratch_shapes=[pltpu.VMEM((t, w), F32)],
        compiler_params=_cparams("parallel", "parallel", "arbitrary"),
        name="dft_pos",
    )(cs, ss, fa, fb)


def _ab_out_kernel(a_ref, f_ref, w1_ref, w2_ref, x_ref, g_ref, o_ref):
    y = _bdot(a_ref[0], w1_ref[...]) + _bdot(f_ref[0], w2_ref[...])
    o_ref[0] = x_ref[0] + g_ref[0] * y


def _ab_out(a, fm, w_out, x, gate):
    b, s, d = x.shape
    tm = min(512, s)
    tok = lambda w: pl.BlockSpec((1, tm, w), lambda i, j: (i, j, 0))
    return pl.pallas_call(
        _ab_out_kernel,
        grid=(b, s // tm),
        in_specs=[tok(ATTN_WIDTH), tok(FNET_WIDTH),
                  pl.BlockSpec((ATTN_WIDTH, d), lambda i, j: (0, 0)),
                  pl.BlockSpec((FNET_WIDTH, d), lambda i, j: (0, 0)),
                  tok(d),
                  pl.BlockSpec((1, 1, d), lambda i, j: (i, 0, 0))],
        out_specs=tok(d),
        out_shape=jax.ShapeDtypeStruct((b, s, d), F32),
        compiler_params=_cparams("parallel", "parallel"),
        name="ab_out",
    )(a, fm, w_out[:ATTN_WIDTH], w_out[ATTN_WIDTH:], x, gate)


def _sgu_kernel(x_ref, nw_ref, sh_ref, sc_ref, g_ref, win_ref, gnw_ref, ws_ref, bs_ref, wout_ref, o_ref, uv_ref):
    x = x_ref[0]
    tm = x.shape[0]
    h = _norm_mod(x, nw_ref[...], sh_ref[0], sc_ref[0]).astype(BF16)
    z = jax.nn.gelu(_bdot(h, win_ref[...]))
    u = z[:, :SGU_WIDTH]
    v = z[:, SGU_WIDTH:]
    v = (v * lax.rsqrt(jnp.mean(v * v, axis=-1, keepdims=True) + RMS_EPS) * gnw_ref[...]).astype(BF16)
    for n in range(tm // SGU_CHUNK):
        r = slice(n * SGU_CHUNK, (n + 1) * SGU_CHUNK)
        for g in range(SGU_GROUPS):
            cs = slice(g * SGU_GROUP_DIM, (g + 1) * SGU_GROUP_DIM)
            sv = _bdot(ws_ref[g], v[r, cs]) + bs_ref[:, g:g + 1]
            uv_ref[r, cs] = (u[r, cs] * sv).astype(BF16)
    o_ref[0] = x + g_ref[0] * _bdot(uv_ref[...], wout_ref[...])


def _sgu(x, nw, shift, scale, gate, w_in, gnw, w_s, b_s_t, w_out):
    b, s, d = x.shape
    tm = min(256, s)
    tok = pl.BlockSpec((1, tm, d), lambda i, j: (i, j, 0))
    mod = pl.BlockSpec((1, 1, d), lambda i, j: (i, 0, 0))
    full = lambda a: pl.BlockSpec(a.shape, lambda i, j: (0,) * a.ndim)
    return pl.pallas_call(
        _sgu_kernel,
        grid=(b, s // tm),
        in_specs=[tok, full(nw), mod, mod, mod, full(w_in), full(gnw), full(w_s), full(b_s_t), full(w_out)],
        out_specs=tok,
        out_shape=jax.ShapeDtypeStruct((b, s, d), F32),
        scratch_shapes=[pltpu.VMEM((tm, SGU_WIDTH), BF16)],
        compiler_params=_cparams("parallel", "parallel"),
        name="sgu",
    )(x, nw, shift, scale, gate, w_in, gnw, w_s, b_s_t, w_out)


def _topk_rows(s, rid, k):
    big = jnp.iinfo(jnp.int32).max
    vals, idxs = [], []
    for _ in range(k):
        m = jnp.max(s, axis=0, keepdims=True)
        am = jnp.min(jnp.where(s == m, rid, big), axis=0, keepdims=True)
        vals.append(m)
        idxs.append(am)
        s = jnp.where(rid == am, -jnp.inf, s)
    return jnp.concatenate(vals, axis=0), jnp.concatenate(idxs, axis=0)


_PAIR_PIECES = ((0, 1, 0, 16), (1, 1, 0, 8), (2, 1, 0, 8), (3, 1, 0, 8),
                (8, 8, 0, 1), (0, 8, 0, 1), (0, 8, 1, 1), (0, 8, 2, 1))
_PAIR_ROW_PIECES = 4


def _pair_candidates(t0, t1):
    k = PEER_TOPK
    n = t0.shape[1]
    vals, ids = [], []
    for a0, na, b0, nb in _PAIR_PIECES:
        io = lax.broadcasted_iota(I32, (max(na, nb), n), 0)
        a = a0 + io if na > 1 else jnp.full_like(io, a0)
        b = b0 + io if nb > 1 else jnp.full_like(io, b0)
        ok = (a + 1) * (b + 1) <= k
        if na > 1:
            ok = ok & (a >= _PAIR_ROW_PIECES)
        vals.append(jnp.where(ok, t0[a0:a0 + na] + t1[b0:b0 + nb], -jnp.inf))
        ids.append(a * k + b)
    return jnp.concatenate(vals, axis=0), jnp.concatenate(ids, axis=0)


def _select_rows(sel, table):
    out = jnp.zeros(sel.shape, table.dtype)
    for a in range(table.shape[0]):
        out = jnp.where(sel == a, table[a:a + 1], out)
    return out


def _peer_idx_kernel(x_ref, nw_ref, sh_ref, sc_ref, wq_ref, keys_ref, hq_ref, e_ref, g_ref, q_ref):
    hq = _norm_mod(x_ref[...], nw_ref[...], sh_ref[0], sc_ref[0])
    hq_ref[...] = hq
    q_ref[...] = _bdot(hq.astype(BF16), wq_ref[...]).astype(BF16)

    def head(h, carry):
        tops, topi = [], []
        for s in range(2):
            col = pl.multiple_of((h * 2 + s) * PEER_HALF_DIM, PEER_HALF_DIM)
            qs = q_ref[:, pl.ds(col, PEER_HALF_DIM)]
            st = lax.dot_general(keys_ref[h * 2 + s], qs, (((1,), (1,)), ((), ())), preferred_element_type=F32)
            ts, ti = _topk_rows(st, lax.broadcasted_iota(I32, st.shape, 0), PEER_TOPK)
            tops.append(ts)
            topi.append(ti)
        best_s, best = _topk_rows(*_pair_candidates(tops[0], tops[1]), PEER_TOPK)
        i1 = _select_rows(best // PEER_TOPK, topi[0])
        i2 = _select_rows(best % PEER_TOPK, topi[1])
        p = jnp.exp(best_s - best_s[0:1])
        row = pl.multiple_of(h * PEER_TOPK, PEER_TOPK)
        e_ref[0, pl.ds(row, PEER_TOPK), :] = i1 * PEER_NKEYS + i2
        g_ref[0, pl.ds(row, PEER_TOPK), :] = p / jnp.sum(p, axis=0, keepdims=True)
        return carry

    lax.fori_loop(0, PEER_HEADS, head, 0)


def _peer_idx(x, nw, shift, scale, w_q, keys, tok0, ntok, seq):
    d = x.shape[1]
    tm = min(256, ntok, seq)
    nq = w_q.shape[1]
    t0 = tok0 // tm
    per_seq = seq // tm
    mod = pl.BlockSpec((1, 1, d), lambda i: ((i + t0) // per_seq, 0, 0))
    sel = pl.BlockSpec((1, PEER_SEL, tm), lambda i: (i, 0, 0))
    return pl.pallas_call(
        _peer_idx_kernel,
        grid=(ntok // tm,),
        in_specs=[pl.BlockSpec((tm, d), lambda i: (i + t0, 0)),
                  pl.BlockSpec((1, d), lambda i: (0, 0)),
                  mod, mod,
                  pl.BlockSpec((d, nq), lambda i: (0, 0)),
                  pl.BlockSpec(keys.shape, lambda i: (0, 0, 0))],
        out_specs=[pl.BlockSpec((tm, d), lambda i: (i, 0)), sel, sel],
        out_shape=[jax.ShapeDtypeStruct((ntok, d), F32),
                   jax.ShapeDtypeStruct((ntok // tm, PEER_SEL, tm), I32),
                   jax.ShapeDtypeStruct((ntok // tm, PEER_SEL, tm), F32)],
        scratch_shapes=[pltpu.VMEM((tm, nq), BF16)],
        compiler_params=_cparams("parallel"),
        name="peer_idx",
    )(x, nw, shift, scale, w_q, keys)


def _sc_worker_id():
    return lax.axis_index("s") * SC_CORES + lax.axis_index("c")


SC_SLAB = 4


def _sc_unpack(x):
    return lax.bitcast_convert_type(x << 16, F32), lax.bitcast_convert_type(x & jnp.uint32(0xFFFF0000), F32)


def _sc_per_token(tab, idx, aux, out_width, chunk_fn):
    k = PEER_SEL
    n_tok, a = aux.shape
    w = tab.shape[1]
    ch = SC_GATHER_ROWS
    per_w = n_tok // SC_WORKERS
    assert k == 2 * ch and per_w % 2 == 0
    mesh = plsc.VectorSubcoreMesh(core_axis_name="c", subcore_axis_name="s")

    @functools.partial(
        pl.kernel, mesh=mesh, out_type=jax.ShapeDtypeStruct((n_tok, out_width), F32),
        scratch_types=[pltpu.VMEM((per_w * k,), I32),
                       pltpu.VMEM((a,), F32), pltpu.VMEM((a,), F32),
                       pltpu.VMEM((ch, w), U32), pltpu.VMEM((ch, w), U32),
                       pltpu.VMEM((out_width,), F32), pltpu.VMEM((out_width,), F32)]
        + [pltpu.SemaphoreType.DMA] * 6,
        compiler_params=pltpu.CompilerParams(needs_layout_passes=False))
    def per_token(tab_hbm, idx_hbm, aux_hbm, out_hbm, idx_v, aux_v0, aux_v1, rows0, rows1, out_v0, out_v1,
                  gsem0, gsem1, asem0, asem1, osem0, osem1):
        base = _sc_worker_id() * per_w
        rbuf = ((rows0, gsem0), (rows1, gsem1))
        abuf = ((aux_v0, asem0), (aux_v1, asem1))
        obuf = ((out_v0, osem0), (out_v1, osem1))
        pltpu.sync_copy(idx_hbm.at[pl.ds(pl.multiple_of(base * k, k), per_w * k)], idx_v)

        def fetch_rows(t, c):
            sel = idx_v.at[pl.ds(pl.multiple_of(t * k + c * ch, ch), ch)]
            return pltpu.make_async_copy(tab_hbm.at[sel], rbuf[c][0], rbuf[c][1])

        def fetch_aux(t, p):
            return pltpu.make_async_copy(aux_hbm.at[base + t], abuf[p][0], abuf[p][1])

        def store_out(t, p):
            return pltpu.make_async_copy(obuf[p][0], out_hbm.at[base + t], obuf[p][1])

        fetch_rows(0, 0).start()
        fetch_rows(0, 1).start()
        fetch_aux(0, 0).start()

        @pl.loop(0, per_w, step=2)
        def _(t0):
            for p in range(2):
                t = t0 + p
                fetch_aux(t, p).wait()

                @pl.when(t + 1 < per_w)
                def _():
                    fetch_aux(t + 1, 1 - p).start()

                @pl.when(t >= 2)
                def _():
                    store_out(t - 2, p).wait()

                for c in range(2):
                    fetch_rows(t, c).wait()
                    chunk_fn(c, abuf[p][0], rbuf[c][0], obuf[p][0])

                    @pl.when(t + 1 < per_w)
                    def _():
                        fetch_rows(t + 1, c).start()

                store_out(t, p).start()

        store_out(per_w - 2, 0).wait()
        store_out(per_w - 1, 1).wait()

    return per_token(tab, idx, aux)


def _sc_row_dots(tab, idx, hq):
    w = tab.shape[1]
    lanes, slab, ch = SC_LANES, SC_SLAB, SC_GATHER_ROWS

    rblk = 16

    def chunk(c, h_v, rows, part_v):
        @pl.loop(0, ch // rblk)
        def _(rb):
            r0 = rb * rblk
            acc = [jnp.zeros((lanes,), F32) for _ in range(rblk)]
            for sb in range(w // lanes // slab):
                col = sb * slab * lanes
                h_lo = [h_v[pl.ds(col + j * lanes, lanes)] for j in range(slab)]
                h_hi = [h_v[pl.ds(w + col + j * lanes, lanes)] for j in range(slab)]
                for r in range(rblk):
                    for j in range(slab):
                        lo, hi = _sc_unpack(rows[r0 + r, pl.ds(col + j * lanes, lanes)])
                        acc[r] += h_lo[j] * lo + h_hi[j] * hi
            for r in range(rblk):
                part_v[pl.ds(pl.multiple_of((c * ch + r0 + r) * lanes, lanes), lanes)] = acc[r]

    return _sc_per_token(tab, idx, hq, PEER_SEL * lanes, chunk)


def _sc_weighted_rows(tab, idx, w_rep):
    w = tab.shape[1]
    lanes, slab, ch = SC_LANES, SC_SLAB, SC_GATHER_ROWS

    def chunk(c, w_v, rows, acc_v):
        @pl.loop(0, w // lanes // slab)
        def _(sb):
            col = pl.multiple_of(sb * (slab * lanes), slab * lanes)
            if c == 0:
                acc = [jnp.zeros((lanes,), F32) for _ in range(2 * slab)]
            else:
                acc = [acc_v[pl.ds(part * w + col + j * lanes, lanes)] for j in range(slab) for part in range(2)]
            for r in range(ch):
                wv = w_v[pl.ds((c * ch + r) * lanes, lanes)]
                for j in range(slab):
                    lo, hi = _sc_unpack(rows[r, pl.ds(col + j * lanes, lanes)])
                    acc[2 * j] += wv * lo
                    acc[2 * j + 1] += wv * hi
            for j in range(slab):
                acc_v[pl.ds(col + j * lanes, lanes)] = acc[2 * j]
                acc_v[pl.ds(w + col + j * lanes, lanes)] = acc[2 * j + 1]

    return _sc_per_token(tab, idx, w_rep, 2 * w, chunk)


def _split3(x):
    hi = x.astype(BF16)
    r1 = x - hi.astype(F32)
    mid = r1.astype(BF16)
    lo = (r1 - mid.astype(F32)).astype(BF16)
    return jnp.concatenate([hi, mid, lo], axis=1)


def _peer_weights_kernel(part_ref, g_ref, fold_ref, rep_ref, w_ref):
    score = _bdot(_split3(part_ref[...]), fold_ref[...])
    wgt = g_ref[...] * jax.nn.gelu(score)
    w_ref[...] = _bdot(_split3(wgt), rep_ref[...])


def _peer_weights(part, g):
    ntok, wide = part.shape
    tm = min(256, ntok)
    rep1 = jnp.repeat(jnp.eye(PEER_SEL, dtype=BF16), SC_LANES, axis=1)
    rep = jnp.tile(rep1, (3, 1))
    fold = jnp.tile(rep1.T, (3, 1))
    row = lambda w: pl.BlockSpec((tm, w), lambda i: (i, 0))
    full = lambda a: pl.BlockSpec(a.shape, lambda i: (0, 0))
    return pl.pallas_call(
        _peer_weights_kernel,
        grid=(ntok // tm,),
        in_specs=[row(wide), row(PEER_SEL), full(fold), full(rep)],
        out_specs=row(wide),
        out_shape=jax.ShapeDtypeStruct((ntok, wide), F32),
        compiler_params=_cparams("parallel"),
        name="peer_weights",
    )(part, g, fold, rep)


def _resid_kernel(x_ref, y_ref, gate_ref, o_ref):
    o_ref[...] = x_ref[...] + gate_ref[0] * y_ref[...]


def _resid(x, y, gate, tok0, seq):
    t, d = x.shape
    ntok = y.shape[0]
    tm = min(512, ntok, seq)
    t0 = tok0 // tm
    per_seq = seq // tm
    tok = pl.BlockSpec((tm, d), lambda i: (i + t0, 0))
    return pl.pallas_call(
        _resid_kernel,
        grid=(ntok // tm,),
        in_specs=[tok, pl.BlockSpec((tm, d), lambda i: (i, 0)),
                  pl.BlockSpec((1, 1, d), lambda i: ((i + t0) // per_seq, 0, 0))],
        out_specs=tok,
        out_shape=jax.ShapeDtypeStruct((t, d), F32),
        input_output_aliases={0: 0},
        compiler_params=_cparams("parallel"),
        name="peer_resid",
    )(x, y, gate)


def _peer_groups(t):
    return max(1, min(4, t // 2048))


def _peer_ffn(x, nw, shift, scale, gate, w_q, keys, tab_u, tab_v):
    b, s, d = x.shape
    t = b * s
    x2 = x.reshape(t, d)
    n_groups = _peer_groups(t)
    gtok = t // n_groups
    staged = []
    for gi in range(n_groups):
        hq, e_t, g_t = _peer_idx(x2, nw, shift, scale, w_q, keys, gi * gtok, gtok, s)
        idx = e_t.transpose(0, 2, 1).reshape(-1)
        g = g_t.transpose(0, 2, 1).reshape(gtok, PEER_SEL)
        staged.append((idx, g, _sc_row_dots(tab_u, idx, hq)))
    ys = [_sc_weighted_rows(tab_v, idx, _peer_weights(part, g)) for idx, g, part in staged]
    for gi, y in enumerate(ys):
        x2 = _resid(x2, y, gate, gi * gtok, s)
    return x2.reshape(b, s, d)


def _rms_kernel(x_ref, w_ref, o_ref):
    x = x_ref[...]
    o_ref[...] = x * lax.rsqrt(jnp.mean(x * x, axis=-1, keepdims=True) + RMS_EPS) * w_ref[...]


def _final_norm(x, w):
    b, s, d = x.shape
    t = b * s
    tm = min(512, t)
    out = pl.pallas_call(
        _rms_kernel,
        grid=(t // tm,),
        in_specs=[pl.BlockSpec((tm, d), lambda i: (i, 0)), pl.BlockSpec((1, d), lambda i: (0, 0))],
        out_specs=pl.BlockSpec((tm, d), lambda i: (i, 0)),
        out_shape=jax.ShapeDtypeStruct((t, d), F32),
        compiler_params=_cparams("parallel"),
        name="final_norm",
    )(x.reshape(t, d), w.reshape(1, d))
    return out.reshape(b, s, d)


def _rope_tables(s):
    t = jnp.arange(s)
    row = (t // GRID_W).astype(F32)
    col = (t % GRID_W).astype(F32)
    half = HEAD_DIM // 2
    inv = ROPE_THETA ** (-jnp.arange(0, half, 2, dtype=F32) / half)
    ar = row[:, None] * inv
    ac = col[:, None] * inv
    ang = jnp.concatenate([ar, ar, ac, ac], axis=1)
    q = half // 2
    sign = jnp.concatenate([-jnp.ones(q), jnp.ones(q), -jnp.ones(q), jnp.ones(q)]).astype(F32)
    reps = QK_WIDTH // HEAD_DIM
    return jnp.tile(jnp.cos(ang), (1, reps)), jnp.tile(jnp.sin(ang) * sign, (1, reps))


def _rope_partner_cols():
    j = jnp.arange(QK_WIDTH)
    dd = j % HEAD_DIM
    q = HEAD_DIM // 4
    return j - dd + jnp.where((dd % (2 * q)) < q, dd + q, dd - q)


def _dft_tables(n, dtype):
    j = jnp.arange(n, dtype=I32)
    ang = ((j[:, None] * j[None, :]) % n).astype(F32) * (2.0 * math.pi / n)
    return jnp.cos(ang).astype(dtype), jnp.sin(ang).astype(dtype)


def _pack_table(tab):
    half = tab.shape[1] // 2
    bits = lax.bitcast_convert_type(tab.astype(BF16), jnp.uint16).astype(U32)
    return bits[:, :half] | (bits[:, half:] << 16)


def kernel(x, c, ctx, c_ctx, ada_w, ada_b, norm_mix_w, norm_ffn_w, ab_w_in, ab_w_out, attn_sink, gs_w_in, gs_norm_w,
           gs_w_s, gs_b_s, gs_w_out, peer_w_q, peer_keys, peer_u, peer_v, norm_out_w):
    bsz, seq, d = x.shape
    n_ctx = ctx.shape[1]
    depth = ada_w.shape[0]

    cond = jnp.concatenate([c, c_ctx[None], jnp.zeros((16 - bsz - 1, d), F32)], axis=0)
    mods = _ada_all(cond, ada_w, ada_b)

    cos, sin = _rope_tables(seq)
    partner = _rope_partner_cols()
    dft_cc, dft_cs = _dft_tables(FNET_GROUP_DIM, BF16)
    dft_c = jnp.concatenate([dft_cc, dft_cs], axis=1)
    pos_l = _dft_tables(seq, BF16)
    pos_c = _dft_tables(n_ctx, BF16)

    last_attn = ((depth - 1) // 2) * 2
    for i in range(depth):
        j = i // 2
        is_ab = i % 2 == 0
        upd_ctx = i < last_attn
        need_ctx = upd_ctx or is_ab
        m_l = [m[:, None, :] for m in jnp.split(mods[i, :bsz], N_MOD, axis=-1)]
        m_c = [jnp.broadcast_to(m[None], (bsz, 1, d)) for m in jnp.split(mods[i, bsz:bsz + 1], N_MOD, axis=-1)]
        nw_m = norm_mix_w[i].reshape(1, d)
        nw_f = norm_ffn_w[i].reshape(1, d)
        y_ctx = None
        if is_ab:
            w_in = ab_w_in[j]
            w_ext = jnp.concatenate([w_in, w_in[:, partner]], axis=1).astype(BF16)
            w_out = ab_w_out[j].astype(BF16)
            sink = attn_sink[j].reshape(1, N_Q_HEADS)
            q_l, k_l, v_l, fa_l, fb_l = _ab_in(x, nw_m, m_l[0], m_l[1], w_ext, cos, sin, dft_c, True)
            q_c, k_c, v_c, fa_c, fb_c = _ab_in(ctx, nw_m, m_c[0], m_c[1], w_ext, cos[:n_ctx], sin[:n_ctx], dft_c, False)
            a_l = _attn_local(q_l, k_l, v_l, k_c, v_c, sink)
            x = _ab_out(a_l, _dft_pos(fa_l, fb_l, *pos_l), w_out, x, m_l[2])
            if upd_ctx:
                a_c = _attn_ctx(q_c, k_c, v_c, sink)
                ctx = _ab_out(a_c, _dft_pos(fa_c, fb_c, *pos_c), w_out, ctx, m_c[2])
        else:
            sgu_w = (gs_w_in[j].astype(BF16), gs_norm_w[j].reshape(1, -1), gs_w_s[j].astype(BF16), gs_b_s[j].T,
                     gs_w_out[j].astype(BF16))
            x = _sgu(x, nw_m, m_l[0], m_l[1], m_l[2], *sgu_w)
            if upd_ctx:
                ctx = _sgu(ctx, nw_m, m_c[0], m_c[1], m_c[2], *sgu_w)
        w_q = peer_w_q[i].astype(BF16)
        keys = peer_keys[i].reshape(PEER_HEADS * 2, PEER_NKEYS, PEER_HALF_DIM).astype(BF16)
        tab_u = _pack_table(peer_u[i])
        tab_v = _pack_table(peer_v[i])
        x = _peer_ffn(x, nw_f, m_l[3], m_l[4], m_l[5], w_q, keys, tab_u, tab_v)
        if upd_ctx:
            ctx = _peer_ffn(ctx, nw_f, m_c[3], m_c[4], m_c[5], w_q, keys, tab_u, tab_v)
        del need_ctx, y_ctx
    return _final_norm(x, norm_out_w)
```

```python
import functools
import math

import jax
import jax.numpy as jnp
from jax import lax
from jax.experimental import pallas as pl
from jax.experimental.pallas import tpu as pltpu
from jax.experimental.pallas import tpu_sc as plsc

F32 = jnp.float32
BF16 = jnp.bfloat16
I32 = jnp.int32
U32 = jnp.uint32

GRID_W = 64
N_Q_HEADS = 8
N_KV_HEADS = 2
Q_PER_KV = N_Q_HEADS // N_KV_HEADS
HEAD_DIM = 64
WINDOW = 128
ROPE_THETA = 10000.0
ATTN_WIDTH = N_Q_HEADS * HEAD_DIM
KV_WIDTH = N_KV_HEADS * HEAD_DIM
QK_WIDTH = ATTN_WIDTH + KV_WIDTH
FNET_GROUPS = 4
FNET_GROUP_DIM = 128
FNET_WIDTH = FNET_GROUPS * FNET_GROUP_DIM
AB_IN_WIDTH = ATTN_WIDTH + 2 * KV_WIDTH + FNET_WIDTH
SGU_GROUPS = 8
SGU_GROUP_DIM = 128
SGU_WIDTH = SGU_GROUPS * SGU_GROUP_DIM
SGU_CHUNK = 128
PEER_HEADS = 8
PEER_NKEYS = 128
PEER_TOPK = 16
PEER_HALF_DIM = 128
PEER_SEL = PEER_HEADS * PEER_TOPK
N_MOD = 6
RMS_EPS = 1e-6
NEG_INF = -1e30

SC_CORES = 2
SC_SUBCORES = 16
SC_WORKERS = SC_CORES * SC_SUBCORES
SC_LANES = 16
SC_GATHER_ROWS = 64
VMEM_LIMIT = 48 * 1024 * 1024


def _cparams(*sem):
    return pltpu.CompilerParams(dimension_semantics=sem, vmem_limit_bytes=VMEM_LIMIT)


def _norm_mod(x, nw, shift, scale):
    y = x * lax.rsqrt(jnp.mean(x * x, axis=-1, keepdims=True) + RMS_EPS) * nw
    return y * (1.0 + scale) + shift


def _bdot(a, b):
    return jnp.dot(a, b, preferred_element_type=F32)


def _ada_kernel(c_ref, w_ref, b_ref, o_ref):
    c = c_ref[...]
    s = (c * jax.nn.sigmoid(c)).astype(BF16)
    o_ref[0] = _bdot(s, w_ref[0].astype(BF16)) + b_ref[0]


def _ada_all(cond, ada_w, ada_b):
    depth, d, n = ada_w.shape
    r = cond.shape[0]
    tn = 1536
    return pl.pallas_call(
        _ada_kernel,
        grid=(depth, n // tn),
        in_specs=[pl.BlockSpec((r, d), lambda l, j: (0, 0)),
                  pl.BlockSpec((1, d, tn), lambda l, j: (l, 0, j)),
                  pl.BlockSpec((1, 1, tn), lambda l, j: (l, 0, j))],
        out_specs=pl.BlockSpec((1, r, tn), lambda l, j: (l, 0, j)),
        out_shape=jax.ShapeDtypeStruct((depth, r, n), F32),
        compiler_params=_cparams("parallel", "parallel"),
        name="ada",
    )(cond, ada_w, ada_b.reshape(depth, 1, n))


def _ab_in_kernel(x_ref, nw_ref, sh_ref, sc_ref, w_ref, cos_ref, sin_ref, dft_ref,
                  q_ref, k_ref, v_ref, fa_ref, fb_ref, *, rope):
    h = _norm_mod(x_ref[0], nw_ref[...], sh_ref[0], sc_ref[0]).astype(BF16)
    p = _bdot(h, w_ref[...])
    qk = p[:, :QK_WIDTH]
    if rope:
        qk = qk * cos_ref[...] + p[:, AB_IN_WIDTH:] * sin_ref[...]
    q_ref[0] = qk[:, :ATTN_WIDTH].astype(BF16)
    k_ref[0] = qk[:, ATTN_WIDTH:].astype(BF16)
    v_ref[0] = p[:, QK_WIDTH:QK_WIDTH + KV_WIDTH].astype(BF16)
    f0 = QK_WIDTH + KV_WIDTH
    for g in range(FNET_GROUPS):
        fg = p[:, f0 + g * FNET_GROUP_DIM:f0 + (g + 1) * FNET_GROUP_DIM].astype(BF16)
        ab = _bdot(fg, dft_ref[...])
        fa_ref[0, :, g * FNET_GROUP_DIM:(g + 1) * FNET_GROUP_DIM] = ab[:, :FNET_GROUP_DIM].astype(BF16)
        fb_ref[0, :, g * FNET_GROUP_DIM:(g + 1) * FNET_GROUP_DIM] = ab[:, FNET_GROUP_DIM:].astype(BF16)


def _ab_in(x, nw, shift, scale, w_ext, cos, sin, dft_c, rope):
    b, s, d = x.shape
    tm = min(512, s)
    n_ext = w_ext.shape[1]
    outs = [jax.ShapeDtypeStruct((b, s, ATTN_WIDTH), BF16), jax.ShapeDtypeStruct((b, s, KV_WIDTH), BF16),
            jax.ShapeDtypeStruct((b, s, KV_WIDTH), BF16), jax.ShapeDtypeStruct((b, s, FNET_WIDTH), BF16),
            jax.ShapeDtypeStruct((b, s, FNET_WIDTH), BF16)]
    tok = lambda w: pl.BlockSpec((1, tm, w), lambda i, j: (i, j, 0))
    return pl.pallas_call(
        functools.partial(_ab_in_kernel, rope=rope),
        grid=(b, s // tm),
        in_specs=[tok(d),
                  pl.BlockSpec((1, d), lambda i, j: (0, 0)),
                  pl.BlockSpec((1, 1, d), lambda i, j: (i, 0, 0)),
                  pl.BlockSpec((1, 1, d), lambda i, j: (i, 0, 0)),
                  pl.BlockSpec((d, n_ext), lambda i, j: (0, 0)),
                  pl.BlockSpec((tm, QK_WIDTH), lambda i, j: (j, 0)),
                  pl.BlockSpec((tm, QK_WIDTH), lambda i, j: (j, 0)),
                  pl.BlockSpec(dft_c.shape, lambda i, j: (0, 0))],
        out_specs=[tok(ATTN_WIDTH), tok(KV_WIDTH), tok(KV_WIDTH), tok(FNET_WIDTH), tok(FNET_WIDTH)],
        out_shape=outs,
        compiler_params=_cparams("parallel", "parallel"),
        name="ab_in",
    )(x, nw, shift, scale, w_ext, cos, sin, dft_c)


def _softmax_pv(qh, kk, vv, valid, sk, scale):
    s = lax.dot_general(qh, kk, (((1,), (1,)), ((), ())), preferred_element_type=F32) * scale
    if valid is not None:
        s = jnp.where(valid, s, NEG_INF)
    m = jnp.maximum(jnp.max(s, axis=-1, keepdims=True), sk)
    p = jnp.exp(s - m)
    inv_den = 1.0 / (jnp.sum(p, axis=-1, keepdims=True) + jnp.exp(sk - m))
    return _bdot((p * inv_den).astype(BF16), vv)


def _gqa_group(q, k_all, v_all, valid, sink_ref, kvh, o_ref):
    n = q.shape[0]
    heads = range(kvh * Q_PER_KV, (kvh + 1) * Q_PER_KV)
    kk = k_all[:, kvh * HEAD_DIM:(kvh + 1) * HEAD_DIM]
    vv = v_all[:, kvh * HEAD_DIM:(kvh + 1) * HEAD_DIM]
    q4 = jnp.concatenate([q[:, h * HEAD_DIM:(h + 1) * HEAD_DIM] for h in heads], axis=0)
    sk4 = jnp.concatenate([jnp.broadcast_to(sink_ref[0:1, h:h + 1], (n, 1)) for h in heads], axis=0)
    valid4 = None if valid is None else jnp.concatenate([valid] * Q_PER_KV, axis=0)
    o4 = _softmax_pv(q4, kk, vv, valid4, sk4, HEAD_DIM ** -0.5)
    for g, h in enumerate(heads):
        o_ref[0, :, h * HEAD_DIM:(h + 1) * HEAD_DIM] = o4[g * n:(g + 1) * n].astype(BF16)


def _attn_local_kernel(q_ref, kp_ref, ko_ref, kn_ref, vp_ref, vo_ref, vn_ref, kc_ref, vc_ref, sink_ref, o_ref,
                       *, seq):
    j = pl.program_id(1)
    blk = q_ref.shape[1]
    n_ctx = kc_ref.shape[1]
    q = q_ref[0]
    kl = jnp.concatenate([kp_ref[0], ko_ref[0], kn_ref[0], kc_ref[0]], axis=0)
    vl = jnp.concatenate([vp_ref[0], vo_ref[0], vn_ref[0], vc_ref[0]], axis=0)
    nk = 3 * blk + n_ctx
    qi = lax.broadcasted_iota(I32, (blk, nk), 0)
    cj = lax.broadcasted_iota(I32, (blk, nk), 1)
    kj = cj - blk
    kpos = j * blk + kj
    valid = (cj >= 3 * blk) | ((jnp.abs(qi - kj) <= WINDOW) & (kpos >= 0) & (kpos < seq))
    for kvh in range(N_KV_HEADS):
        _gqa_group(q, kl, vl, valid, sink_ref, kvh, o_ref)


def _attn_local(q, k, v, kc, vc, sink):
    b, s, _ = q.shape
    c = kc.shape[1]
    blk = WINDOW
    nb = s // blk
    qspec = pl.BlockSpec((1, blk, ATTN_WIDTH), lambda i, j: (i, j, 0))
    prev = pl.BlockSpec((1, blk, KV_WIDTH), lambda i, j: (i, jnp.maximum(j - 1, 0), 0))
    own = pl.BlockSpec((1, blk, KV_WIDTH), lambda i, j: (i, j, 0))
    nxt = pl.BlockSpec((1, blk, KV_WIDTH), lambda i, j: (i, jnp.minimum(j + 1, nb - 1), 0))
    cspec = pl.BlockSpec((1, c, KV_WIDTH), lambda i, j: (i, 0, 0))
    return pl.pallas_call(
        functools.partial(_attn_local_kernel, seq=s),
        grid=(b, nb),
        in_specs=[qspec, prev, own, nxt, prev, own, nxt, cspec, cspec,
                  pl.BlockSpec((1, N_Q_HEADS), lambda i, j: (0, 0))],
        out_specs=qspec,
        out_shape=jax.ShapeDtypeStruct((b, s, ATTN_WIDTH), BF16),
        compiler_params=_cparams("parallel", "parallel"),
        name="attn_local",
    )(q, k, k, k, v, v, v, kc, vc, sink)


def _attn_ctx_kernel(q_ref, k_ref, v_ref, sink_ref, o_ref):
    for kvh in range(N_KV_HEADS):
        _gqa_group(q_ref[0], k_ref[0], v_ref[0], None, sink_ref, kvh, o_ref)


def _attn_ctx(q, k, v, sink):
    b, c, _ = q.shape
    return pl.pallas_call(
        _attn_ctx_kernel,
        grid=(b,),
        in_specs=[pl.BlockSpec((1, c, ATTN_WIDTH), lambda i: (i, 0, 0)),
                  pl.BlockSpec((1, c, KV_WIDTH), lambda i: (i, 0, 0)),
                  pl.BlockSpec((1, c, KV_WIDTH), lambda i: (i, 0, 0)),
                  pl.BlockSpec((1, N_Q_HEADS), lambda i: (0, 0))],
        out_specs=pl.BlockSpec((1, c, ATTN_WIDTH), lambda i: (i, 0, 0)),
        out_shape=jax.ShapeDtypeStruct((b, c, ATTN_WIDTH), BF16),
        compiler_params=_cparams("parallel"),
        name="attn_ctx",
    )(q, k, v, sink)


def _dft_pos_kernel(cs_ref, ss_ref, a_ref, b_ref, o_ref, acc_ref, *, scale):
    kk = pl.program_id(2)

    @pl.when(kk == 0)
    def _():
        acc_ref[...] = jnp.zeros_like(acc_ref)

    acc_ref[...] += _bdot(cs_ref[...], a_ref[0]) - _bdot(ss_ref[...], b_ref[0])

    @pl.when(kk == pl.num_programs(2) - 1)
    def _():
        o_ref[0] = (acc_ref[...] * scale).astype(BF16)


def _dft_pos(fa, fb, cs, ss):
    b, s, w = fa.shape
    t = min(1024, s)
    scale = 1.0 / math.sqrt(s * FNET_GROUP_DIM)
    return pl.pallas_call(
        functools.partial(_dft_pos_kernel, scale=scale),
        grid=(b, s // t, s // t),
        in_specs=[pl.BlockSpec((t, t), lambda i, m, k: (m, k)),
                  pl.BlockSpec((t, t), lambda i, m, k: (m, k)),
                  pl.BlockSpec((1, t, w), lambda i, m, k: (i, k, 0)),
                  pl.BlockSpec((1, t, w), lambda i, m, k: (i, k, 0))],
        out_specs=pl.BlockSpec((1, t, w), lambda i, m, k: (i, m, 0)),
        out_shape=jax.ShapeDtypeStruct((b, s, w), BF16),
        scratch_shapes=[pltpu.VMEM((t, w), F32)],
        compiler_params=_cparams("parallel", "parallel", "arbitrary"),
        name="dft_pos",
    )(cs, ss, fa, fb)


def _ab_out_kernel(a_ref, f_ref, w1_ref, w2_ref, x_ref, g_ref, o_ref):
    y = _bdot(a_ref[0], w1_ref[...]) + _bdot(f_ref[0], w2_ref[...])
    o_ref[0] = x_ref[0] + g_ref[0] * y


def _ab_out(a, fm, w_out, x, gate):
    b, s, d = x.shape
    tm = min(512, s)
    tok = lambda w: pl.BlockSpec((1, tm, w), lambda i, j: (i, j, 0))
    return pl.pallas_call(
        _ab_out_kernel,
        grid=(b, s // tm),
        in_specs=[tok(ATTN_WIDTH), tok(FNET_WIDTH),
                  pl.BlockSpec((ATTN_WIDTH, d), lambda i, j: (0, 0)),
                  pl.BlockSpec((FNET_WIDTH, d), lambda i, j: (0, 0)),
                  tok(d),
                  pl.BlockSpec((1, 1, d), lambda i, j: (i, 0, 0))],
        out_specs=tok(d),
        out_shape=jax.ShapeDtypeStruct((b, s, d), F32),
        compiler_params=_cparams("parallel", "parallel"),
        name="ab_out",
    )(a, fm, w_out[:ATTN_WIDTH], w_out[ATTN_WIDTH:], x, gate)


def _sgu_kernel(x_ref, nw_ref, sh_ref, sc_ref, g_ref, win_ref, gnw_ref, ws_ref, bs_ref, wout_ref, o_ref, uv_ref):
    x = x_ref[0]
    tm = x.shape[0]
    h = _norm_mod(x, nw_ref[...], sh_ref[0], sc_ref[0]).astype(BF16)
    z = jax.nn.gelu(_bdot(h, win_ref[...]))
    u = z[:, :SGU_WIDTH]
    v = z[:, SGU_WIDTH:]
    v = (v * lax.rsqrt(jnp.mean(v * v, axis=-1, keepdims=True) + RMS_EPS) * gnw_ref[...]).astype(BF16)
    for n in range(tm // SGU_CHUNK):
        r = slice(n * SGU_CHUNK, (n + 1) * SGU_CHUNK)
        for g in range(SGU_GROUPS):
            cs = slice(g * SGU_GROUP_DIM, (g + 1) * SGU_GROUP_DIM)
            sv = _bdot(ws_ref[g], v[r, cs]) + bs_ref[:, g:g + 1]
            uv_ref[r, cs] = (u[r, cs] * sv).astype(BF16)
    o_ref[0] = x + g_ref[0] * _bdot(uv_ref[...], wout_ref[...])


def _sgu(x, nw, shift, scale, gate, w_in, gnw, w_s, b_s_t, w_out):
    b, s, d = x.shape
    tm = min(256, s)
    tok = pl.BlockSpec((1, tm, d), lambda i, j: (i, j, 0))
    mod = pl.BlockSpec((1, 1, d), lambda i, j: (i, 0, 0))
    full = lambda a: pl.BlockSpec(a.shape, lambda i, j: (0,) * a.ndim)
    return pl.pallas_call(
        _sgu_kernel,
        grid=(b, s // tm),
        in_specs=[tok, full(nw), mod, mod, mod, full(w_in), full(gnw), full(w_s), full(b_s_t), full(w_out)],
        out_specs=tok,
        out_shape=jax.ShapeDtypeStruct((b, s, d), F32),
        scratch_shapes=[pltpu.VMEM((tm, SGU_WIDTH), BF16)],
        compiler_params=_cparams("parallel", "parallel"),
        name="sgu",
    )(x, nw, shift, scale, gate, w_in, gnw, w_s, b_s_t, w_out)


def _topk_rows(s, rid, k):
    big = jnp.iinfo(jnp.int32).max
    vals, idxs = [], []
    for _ in range(k):
        m = jnp.max(s, axis=0, keepdims=True)
        am = jnp.min(jnp.where(s == m, rid, big), axis=0, keepdims=True)
        vals.append(m)
        idxs.append(am)
        s = jnp.where(rid == am, -jnp.inf, s)
    return jnp.concatenate(vals, axis=0), jnp.concatenate(idxs, axis=0)


_PAIR_PIECES = ((0, 1, 0, 16), (1, 1, 0, 8), (2, 1, 0, 8), (3, 1, 0, 8),
                (8, 8, 0, 1), (0, 8, 0, 1), (0, 8, 1, 1), (0, 8, 2, 1))
_PAIR_ROW_PIECES = 4


def _pair_candidates(t0, t1):
    k = PEER_TOPK
    n = t0.shape[1]
    vals, ids = [], []
    for a0, na, b0, nb in _PAIR_PIECES:
        io = lax.broadcasted_iota(I32, (max(na, nb), n), 0)
        a = a0 + io if na > 1 else jnp.full_like(io, a0)
        b = b0 + io if nb > 1 else jnp.full_like(io, b0)
        ok = (a + 1) * (b + 1) <= k
        if na > 1:
            ok = ok & (a >= _PAIR_ROW_PIECES)
        vals.append(jnp.where(ok, t0[a0:a0 + na] + t1[b0:b0 + nb], -jnp.inf))
        ids.append(a * k + b)
    return jnp.concatenate(vals, axis=0), jnp.concatenate(ids, axis=0)


def _select_rows(sel, table):
    out = jnp.zeros(sel.shape, table.dtype)
    for a in range(table.shape[0]):
        out = jnp.where(sel == a, table[a:a + 1], out)
    return out


def _peer_idx_kernel(x_ref, nw_ref, sh_ref, sc_ref, wq_ref, keys_ref, hq_ref, e_ref, g_ref, q_ref):
    hq = _norm_mod(x_ref[...], nw_ref[...], sh_ref[0], sc_ref[0])
    hq_ref[...] = hq
    q_ref[...] = _bdot(hq.astype(BF16), wq_ref[...]).astype(BF16)

    def head(h, carry):
        tops, topi = [], []
        for s in range(2):
            col = pl.multiple_of((h * 2 + s) * PEER_HALF_DIM, PEER_HALF_DIM)
            qs = q_ref[:, pl.ds(col, PEER_HALF_DIM)]
            st = lax.dot_general(keys_ref[h * 2 + s], qs, (((1,), (1,)), ((), ())), preferred_element_type=F32)
            ts, ti = _topk_rows(st, lax.broadcasted_iota(I32, st.shape, 0), PEER_TOPK)
            tops.append(ts)
            topi.append(ti)
        best_s, best = _topk_rows(*_pair_candidates(tops[0], tops[1]), PEER_TOPK)
        i1 = _select_rows(best // PEER_TOPK, topi[0])
        i2 = _select_rows(best % PEER_TOPK, topi[1])
        p = jnp.exp(best_s - best_s[0:1])
        row = pl.multiple_of(h * PEER_TOPK, PEER_TOPK)
        e_ref[0, pl.ds(row, PEER_TOPK), :] = i1 * PEER_NKEYS + i2
        g_ref[0, pl.ds(row, PEER_TOPK), :] = p / jnp.sum(p, axis=0, keepdims=True)
        return carry

    lax.fori_loop(0, PEER_HEADS, head, 0)


def _peer_idx(x, nw, shift, scale, w_q, keys, tok0, ntok, seq):
    d = x.shape[1]
    tm = min(256, ntok, seq)
    nq = w_q.shape[1]
    t0 = tok0 // tm
    per_seq = seq // tm
    mod = pl.BlockSpec((1, 1, d), lambda i: ((i + t0) // per_seq, 0, 0))
    sel = pl.BlockSpec((1, PEER_SEL, tm), lambda i: (i, 0, 0))
    return pl.pallas_call(
        _peer_idx_kernel,
        grid=(ntok // tm,),
        in_specs=[pl.BlockSpec((tm, d), lambda i: (i + t0, 0)),
                  pl.BlockSpec((1, d), lambda i: (0, 0)),
                  mod, mod,
                  pl.BlockSpec((d, nq), lambda i: (0, 0)),
                  pl.BlockSpec(keys.shape, lambda i: (0, 0, 0))],
        out_specs=[pl.BlockSpec((tm, d), lambda i: (i, 0)), sel, sel],
        out_shape=[jax.ShapeDtypeStruct((ntok, d), F32),
                   jax.ShapeDtypeStruct((ntok // tm, PEER_SEL, tm), I32),
                   jax.ShapeDtypeStruct((ntok // tm, PEER_SEL, tm), F32)],
        scratch_shapes=[pltpu.VMEM((tm, nq), BF16)],
        compiler_params=_cparams("parallel"),
        name="peer_idx",
    )(x, nw, shift, scale, w_q, keys)


def _sc_worker_id():
    return lax.axis_index("s") * SC_CORES + lax.axis_index("c")


SC_SLAB = 2


def _sc_unpack(x):
    return lax.bitcast_convert_type(x << 16, F32), lax.bitcast_convert_type(x & jnp.uint32(0xFFFF0000), F32)


def _sc_per_token(tab, idx, aux, out_width, chunk_fn):
    k = PEER_SEL
    n_tok, a = aux.shape
    w = tab.shape[1]
    ch = SC_GATHER_ROWS
    per_w = n_tok // SC_WORKERS
    assert k == 2 * ch and per_w % 2 == 0
    mesh = plsc.VectorSubcoreMesh(core_axis_name="c", subcore_axis_name="s")

    @functools.partial(
        pl.kernel, mesh=mesh, out_type=jax.ShapeDtypeStruct((n_tok, out_width), F32),
        scratch_types=[pltpu.VMEM((per_w * k,), I32), pltpu.VMEM((2, a), F32),
                       pltpu.VMEM((ch, w), U32), pltpu.VMEM((ch, w), U32), pltpu.VMEM((2, out_width), F32),
                       pltpu.SemaphoreType.DMA, pltpu.SemaphoreType.DMA,
                       pltpu.SemaphoreType.DMA((2,)), pltpu.SemaphoreType.DMA((2,))],
        compiler_params=pltpu.CompilerParams(needs_layout_passes=False))
    def per_token(tab_hbm, idx_hbm, aux_hbm, out_hbm, idx_v, aux_v, rows0, rows1, out_v, gsem0, gsem1, asem, osem):
        base = _sc_worker_id() * per_w
        rbuf = ((rows0, gsem0), (rows1, gsem1))
        pltpu.sync_copy(idx_hbm.at[pl.ds(pl.multiple_of(base * k, k), per_w * k)], idx_v)

        def fetch_rows(t, c):
            sel = idx_v.at[pl.ds(pl.multiple_of(t * k + c * ch, ch), ch)]
            return pltpu.make_async_copy(tab_hbm.at[sel], rbuf[c][0], rbuf[c][1])

        def fetch_aux(t, p):
            return pltpu.make_async_copy(aux_hbm.at[base + t], aux_v.at[p], asem.at[p])

        def store_out(t, p):
            return pltpu.make_async_copy(out_v.at[p], out_hbm.at[base + t], osem.at[p])

        fetch_rows(0, 0).start()
        fetch_rows(0, 1).start()
        fetch_aux(0, 0).start()

        @pl.loop(0, per_w)
        def _(t):
            p = lax.rem(t, 2)
            fetch_aux(t, p).wait()

            @pl.when(t + 1 < per_w)
            def _():
                fetch_aux(t + 1, 1 - p).start()

            @pl.when(t >= 2)
            def _():
                store_out(t - 2, p).wait()

            for c in range(2):
                fetch_rows(t, c).wait()
                chunk_fn(c, p, aux_v, rbuf[c][0], out_v)

                @pl.when(t + 1 < per_w)
                def _():
                    fetch_rows(t + 1, c).start()

            store_out(t, p).start()

        store_out(per_w - 2, 0).wait()
        store_out(per_w - 1, 1).wait()

    return per_token(tab, idx, aux)


def _sc_row_dots(tab, idx, hq):
    w = tab.shape[1]
    lanes, slab, ch = SC_LANES, SC_SLAB, SC_GATHER_ROWS

    rblk = 8

    def chunk(c, p, h_v, rows, part_v):
        @pl.loop(0, ch // rblk)
        def _(rb):
            r0 = rb * rblk
            acc = [jnp.zeros((lanes,), F32) for _ in range(rblk)]
            for sb in range(w // lanes // slab):
                col = sb * slab * lanes
                h_lo = [h_v[p, pl.ds(col + j * lanes, lanes)] for j in range(slab)]
                h_hi = [h_v[p, pl.ds(w + col + j * lanes, lanes)] for j in range(slab)]
                for r in range(rblk):
                    for j in range(slab):
                        lo, hi = _sc_unpack(rows[r0 + r, pl.ds(col + j * lanes, lanes)])
                        acc[r] += h_lo[j] * lo + h_hi[j] * hi
            for r in range(rblk):
                part_v[p, pl.ds(pl.multiple_of((c * ch + r0 + r) * lanes, lanes), lanes)] = acc[r]

    return _sc_per_token(tab, idx, hq, PEER_SEL * lanes, chunk)


def _sc_weighted_rows(tab, idx, w_rep):
    w = tab.shape[1]
    lanes, slab, ch = SC_LANES, SC_SLAB, SC_GATHER_ROWS

    def chunk(c, p, w_v, rows, acc_v):
        @pl.loop(0, w // lanes // slab)
        def _(sb):
            col = pl.multiple_of(sb * (slab * lanes), slab * lanes)
            if c == 0:
                acc = [jnp.zeros((lanes,), F32) for _ in range(2 * slab)]
            else:
                acc = [acc_v[p, pl.ds(part * w + col + j * lanes, lanes)] for j in range(slab) for part in range(2)]
            for r in range(ch):
                wv = w_v[p, pl.ds((c * ch + r) * lanes, lanes)]
                for j in range(slab):
                    lo, hi = _sc_unpack(rows[r, pl.ds(col + j * lanes, lanes)])
                    acc[2 * j] += wv * lo
                    acc[2 * j + 1] += wv * hi
            for j in range(slab):
                acc_v[p, pl.ds(col + j * lanes, lanes)] = acc[2 * j]
                acc_v[p, pl.ds(w + col + j * lanes, lanes)] = acc[2 * j + 1]

    return _sc_per_token(tab, idx, w_rep, 2 * w, chunk)


def _split3(x):
    hi = x.astype(BF16)
    r1 = x - hi.astype(F32)
    mid = r1.astype(BF16)
    lo = (r1 - mid.astype(F32)).astype(BF16)
    return jnp.concatenate([hi, mid, lo], axis=1)


def _peer_weights_kernel(part_ref, g_ref, fold_ref, rep_ref, w_ref):
    score = _bdot(_split3(part_ref[...]), fold_ref[...])
    wgt = g_ref[...] * jax.nn.gelu(score)
    w_ref[...] = _bdot(_split3(wgt), rep_ref[...])


def _peer_weights(part, g):
    ntok, wide = part.shape
    tm = min(256, ntok)
    rep1 = jnp.repeat(jnp.eye(PEER_SEL, dtype=BF16), SC_LANES, axis=1)
    rep = jnp.tile(rep1, (3, 1))
    fold = jnp.tile(rep1.T, (3, 1))
    row = lambda w: pl.BlockSpec((tm, w), lambda i: (i, 0))
    full = lambda a: pl.BlockSpec(a.shape, lambda i: (0, 0))
    return pl.pallas_call(
        _peer_weights_kernel,
        grid=(ntok // tm,),
        in_specs=[row(wide), row(PEER_SEL), full(fold), full(rep)],
        out_specs=row(wide),
        out_shape=jax.ShapeDtypeStruct((ntok, wide), F32),
        compiler_params=_cparams("parallel"),
        name="peer_weights",
    )(part, g, fold, rep)


def _resid_kernel(x_ref, y_ref, gate_ref, o_ref):
    o_ref[...] = x_ref[...] + gate_ref[0] * y_ref[...]


def _resid(x, y, gate, tok0, seq):
    t, d = x.shape
    ntok = y.shape[0]
    tm = min(512, ntok, seq)
    t0 = tok0 // tm
    per_seq = seq // tm
    tok = pl.BlockSpec((tm, d), lambda i: (i + t0, 0))
    return pl.pallas_call(
        _resid_kernel,
        grid=(ntok // tm,),
        in_specs=[tok, pl.BlockSpec((tm, d), lambda i: (i, 0)),
                  pl.BlockSpec((1, 1, d), lambda i: ((i + t0) // per_seq, 0, 0))],
        out_specs=tok,
        out_shape=jax.ShapeDtypeStruct((t, d), F32),
        input_output_aliases={0: 0},
        compiler_params=_cparams("parallel"),
        name="peer_resid",
    )(x, y, gate)


def _peer_groups(t):
    return max(1, min(4, t // 2048))


def _peer_ffn(x, nw, shift, scale, gate, w_q, keys, tab_u, tab_v):
    b, s, d = x.shape
    t = b * s
    x2 = x.reshape(t, d)
    n_groups = _peer_groups(t)
    gtok = t // n_groups
    staged = []
    for gi in range(n_groups):
        hq, e_t, g_t = _peer_idx(x2, nw, shift, scale, w_q, keys, gi * gtok, gtok, s)
        idx = e_t.transpose(0, 2, 1).reshape(-1)
        g = g_t.transpose(0, 2, 1).reshape(gtok, PEER_SEL)
        staged.append((idx, g, _sc_row_dots(tab_u, idx, hq)))
    ys = [_sc_weighted_rows(tab_v, idx, _peer_weights(part, g)) for idx, g, part in staged]
    for gi, y in enumerate(ys):
        x2 = _resid(x2, y, gate, gi * gtok, s)
    return x2.reshape(b, s, d)


def _rms_kernel(x_ref, w_ref, o_ref):
    x = x_ref[...]
    o_ref[...] = x * lax.rsqrt(jnp.mean(x * x, axis=-1, keepdims=True) + RMS_EPS) * w_ref[...]


def _final_norm(x, w):
    b, s, d = x.shape
    t = b * s
    tm = min(512, t)
    out = pl.pallas_call(
        _rms_kernel,
        grid=(t // tm,),
        in_specs=[pl.BlockSpec((tm, d), lambda i: (i, 0)), pl.BlockSpec((1, d), lambda i: (0, 0))],
        out_specs=pl.BlockSpec((tm, d), lambda i: (i, 0)),
        out_shape=jax.ShapeDtypeStruct((t, d), F32),
        compiler_params=_cparams("parallel"),
        name="final_norm",
    )(x.reshape(t, d), w.reshape(1, d))
    return out.reshape(b, s, d)


def _rope_tables(s):
    t = jnp.arange(s)
    row = (t // GRID_W).astype(F32)
    col = (t % GRID_W).astype(F32)
    half = HEAD_DIM // 2
    inv = ROPE_THETA ** (-jnp.arange(0, half, 2, dtype=F32) / half)
    ar = row[:, None] * inv
    ac = col[:, None] * inv
    ang = jnp.concatenate([ar, ar, ac, ac], axis=1)
    q = half // 2
    sign = jnp.concatenate([-jnp.ones(q), jnp.ones(q), -jnp.ones(q), jnp.ones(q)]).astype(F32)
    reps = QK_WIDTH // HEAD_DIM
    return jnp.tile(jnp.cos(ang), (1, reps)), jnp.tile(jnp.sin(ang) * sign, (1, reps))


def _rope_partner_cols():
    j = jnp.arange(QK_WIDTH)
    dd = j % HEAD_DIM
    q = HEAD_DIM // 4
    return j - dd + jnp.where((dd % (2 * q)) < q, dd + q, dd - q)


def _dft_tables(n, dtype):
    j = jnp.arange(n, dtype=I32)
    ang = ((j[:, None] * j[None, :]) % n).astype(F32) * (2.0 * math.pi / n)
    return jnp.cos(ang).astype(dtype), jnp.sin(ang).astype(dtype)


def _pack_table(tab):
    half = tab.shape[1] // 2
    bits = lax.bitcast_convert_type(tab.astype(BF16), jnp.uint16).astype(U32)
    return bits[:, :half] | (bits[:, half:] << 16)


def kernel(x, c, ctx, c_ctx, ada_w, ada_b, norm_mix_w, norm_ffn_w, ab_w_in, ab_w_out, attn_sink, gs_w_in, gs_norm_w,
           gs_w_s, gs_b_s, gs_w_out, peer_w_q, peer_keys, peer_u, peer_v, norm_out_w):
    bsz, seq, d = x.shape
    n_ctx = ctx.shape[1]
    depth = ada_w.shape[0]

    cond = jnp.concatenate([c, c_ctx[None], jnp.zeros((16 - bsz - 1, d), F32)], axis=0)
    mods = _ada_all(cond, ada_w, ada_b)

    cos, sin = _rope_tables(seq)
    partner = _rope_partner_cols()
    dft_cc, dft_cs = _dft_tables(FNET_GROUP_DIM, BF16)
    dft_c = jnp.concatenate([dft_cc, dft_cs], axis=1)
    pos_l = _dft_tables(seq, BF16)
    pos_c = _dft_tables(n_ctx, BF16)

    last_attn = ((depth - 1) // 2) * 2
    for i in range(depth):
        j = i // 2
        is_ab = i % 2 == 0
        upd_ctx = i < last_attn
        need_ctx = upd_ctx or is_ab
        m_l = [m[:, None, :] for m in jnp.split(mods[i, :bsz], N_MOD, axis=-1)]
        m_c = [jnp.broadcast_to(m[None], (bsz, 1, d)) for m in jnp.split(mods[i, bsz:bsz + 1], N_MOD, axis=-1)]
        nw_m = norm_mix_w[i].reshape(1, d)
        nw_f = norm_ffn_w[i].reshape(1, d)
        y_ctx = None
        if is_ab:
            w_in = ab_w_in[j]
            w_ext = jnp.concatenate([w_in, w_in[:, partner]], axis=1).astype(BF16)
            w_out = ab_w_out[j].astype(BF16)
            sink = attn_sink[j].reshape(1, N_Q_HEADS)
            q_l, k_l, v_l, fa_l, fb_l = _ab_in(x, nw_m, m_l[0], m_l[1], w_ext, cos, sin, dft_c, True)
            q_c, k_c, v_c, fa_c, fb_c = _ab_in(ctx, nw_m, m_c[0], m_c[1], w_ext, cos[:n_ctx], sin[:n_ctx], dft_c, False)
            a_l = _attn_local(q_l, k_l, v_l, k_c, v_c, sink)
            x = _ab_out(a_l, _dft_pos(fa_l, fb_l, *pos_l), w_out, x, m_l[2])
            if upd_ctx:
                a_c = _attn_ctx(q_c, k_c, v_c, sink)
                ctx = _ab_out(a_c, _dft_pos(fa_c, fb_c, *pos_c), w_out, ctx, m_c[2])
        else:
            sgu_w = (gs_w_in[j].astype(BF16), gs_norm_w[j].reshape(1, -1), gs_w_s[j].astype(BF16), gs_b_s[j].T,
                     gs_w_out[j].astype(BF16))
            x = _sgu(x, nw_m, m_l[0], m_l[1], m_l[2], *sgu_w)
            if upd_ctx:
                ctx = _sgu(ctx, nw_m, m_c[0], m_c[1], m_c[2], *sgu_w)
        w_q = peer_w_q[i].astype(BF16)
        keys = peer_keys[i].reshape(PEER_HEADS * 2, PEER_NKEYS, PEER_HALF_DIM).astype(BF16)
        tab_u = _pack_table(peer_u[i])
        tab_v = _pack_table(peer_v[i])
        x = _peer_ffn(x, nw_f, m_l[3], m_l[4], m_l[5], w_q, keys, tab_u, tab_v)
        if upd_ctx:
            ctx = _peer_ffn(ctx, nw_f, m_c[3], m_c[4], m_c[5], w_q, keys, tab_u, tab_v)
        del need_ctx, y_ctx
    return _final_norm(x, norm_out_w)
```

```python
import functools
import math

import jax
import jax.numpy as jnp
from jax import lax
from jax.experimental import pallas as pl
from jax.experimental.pallas import tpu as pltpu
from jax.experimental.pallas import tpu_sc as plsc

F32 = jnp.float32
BF16 = jnp.bfloat16
I32 = jnp.int32
U32 = jnp.uint32

GRID_W = 64
N_Q_HEADS = 8
N_KV_HEADS = 2
Q_PER_KV = N_Q_HEADS // N_KV_HEADS
HEAD_DIM = 64
WINDOW = 128
ROPE_THETA = 10000.0
ATTN_WIDTH = N_Q_HEADS * HEAD_DIM
KV_WIDTH = N_KV_HEADS * HEAD_DIM
QK_WIDTH = ATTN_WIDTH + KV_WIDTH
FNET_GROUPS = 4
FNET_GROUP_DIM = 128
FNET_WIDTH = FNET_GROUPS * FNET_GROUP_DIM
AB_IN_WIDTH = ATTN_WIDTH + 2 * KV_WIDTH + FNET_WIDTH
SGU_GROUPS = 8
SGU_GROUP_DIM = 128
SGU_WIDTH = SGU_GROUPS * SGU_GROUP_DIM
SGU_CHUNK = 128
PEER_HEADS = 8
PEER_NKEYS = 128
PEER_TOPK = 16
PEER_HALF_DIM = 128
PEER_SEL = PEER_HEADS * PEER_TOPK
N_MOD = 6
RMS_EPS = 1e-6
NEG_INF = -1e30

SC_CORES = 2
SC_SUBCORES = 16
SC_WORKERS = SC_CORES * SC_SUBCORES
SC_LANES = 16
SC_GATHER_ROWS = 64
VMEM_LIMIT = 48 * 1024 * 1024


def _cparams(*sem):
    return pltpu.CompilerParams(dimension_semantics=sem, vmem_limit_bytes=VMEM_LIMIT)


def _norm_mod(x, nw, shift, scale):
    y = x * lax.rsqrt(jnp.mean(x * x, axis=-1, keepdims=True) + RMS_EPS) * nw
    return y * (1.0 + scale) + shift


def _bdot(a, b):
    return jnp.dot(a, b, preferred_element_type=F32)


def _ada_kernel(c_ref, w_ref, b_ref, o_ref):
    c = c_ref[...]
    s = (c * jax.nn.sigmoid(c)).astype(BF16)
    o_ref[0] = _bdot(s, w_ref[0].astype(BF16)) + b_ref[0]


def _ada_all(cond, ada_w, ada_b):
    depth, d, n = ada_w.shape
    r = cond.shape[0]
    tn = 1536
    return pl.pallas_call(
        _ada_kernel,
        grid=(depth, n // tn),
        in_specs=[pl.BlockSpec((r, d), lambda l, j: (0, 0)),
                  pl.BlockSpec((1, d, tn), lambda l, j: (l, 0, j)),
                  pl.BlockSpec((1, 1, tn), lambda l, j: (l, 0, j))],
        out_specs=pl.BlockSpec((1, r, tn), lambda l, j: (l, 0, j)),
        out_shape=jax.ShapeDtypeStruct((depth, r, n), F32),
        compiler_params=_cparams("parallel", "parallel"),
        name="ada",
    )(cond, ada_w, ada_b.reshape(depth, 1, n))


def _ab_in_kernel(x_ref, nw_ref, sh_ref, sc_ref, w_ref, cos_ref, sin_ref, dft_ref,
                  q_ref, k_ref, v_ref, fa_ref, fb_ref, *, rope):
    h = _norm_mod(x_ref[0], nw_ref[...], sh_ref[0], sc_ref[0]).astype(BF16)
    p = _bdot(h, w_ref[...])
    qk = p[:, :QK_WIDTH]
    if rope:
        qk = qk * cos_ref[...] + p[:, AB_IN_WIDTH:] * sin_ref[...]
    q_ref[0] = qk[:, :ATTN_WIDTH].astype(BF16)
    k_ref[0] = qk[:, ATTN_WIDTH:].astype(BF16)
    v_ref[0] = p[:, QK_WIDTH:QK_WIDTH + KV_WIDTH].astype(BF16)
    f0 = QK_WIDTH + KV_WIDTH
    for g in range(FNET_GROUPS):
        fg = p[:, f0 + g * FNET_GROUP_DIM:f0 + (g + 1) * FNET_GROUP_DIM].astype(BF16)
        ab = _bdot(fg, dft_ref[...])
        fa_ref[0, :, g * FNET_GROUP_DIM:(g + 1) * FNET_GROUP_DIM] = ab[:, :FNET_GROUP_DIM].astype(BF16)
        fb_ref[0, :, g * FNET_GROUP_DIM:(g + 1) * FNET_GROUP_DIM] = ab[:, FNET_GROUP_DIM:].astype(BF16)


def _ab_in(x, nw, shift, scale, w_ext, cos, sin, dft_c, rope, b0, nb):
    _, s, d = x.shape
    tm = min(512, s)
    n_ext = w_ext.shape[1]
    outs = [jax.ShapeDtypeStruct((nb, s, ATTN_WIDTH), BF16), jax.ShapeDtypeStruct((nb, s, KV_WIDTH), BF16),
            jax.ShapeDtypeStruct((nb, s, KV_WIDTH), BF16), jax.ShapeDtypeStruct((nb, s, FNET_WIDTH), BF16),
            jax.ShapeDtypeStruct((nb, s, FNET_WIDTH), BF16)]
    tok = lambda w: pl.BlockSpec((1, tm, w), lambda i, j: (i, j, 0))
    return pl.pallas_call(
        functools.partial(_ab_in_kernel, rope=rope),
        grid=(nb, s // tm),
        in_specs=[pl.BlockSpec((1, tm, d), lambda i, j: (i + b0, j, 0)),
                  pl.BlockSpec((1, d), lambda i, j: (0, 0)),
                  pl.BlockSpec((1, 1, d), lambda i, j: (i + b0, 0, 0)),
                  pl.BlockSpec((1, 1, d), lambda i, j: (i + b0, 0, 0)),
                  pl.BlockSpec((d, n_ext), lambda i, j: (0, 0)),
                  pl.BlockSpec((tm, QK_WIDTH), lambda i, j: (j, 0)),
                  pl.BlockSpec((tm, QK_WIDTH), lambda i, j: (j, 0)),
                  pl.BlockSpec(dft_c.shape, lambda i, j: (0, 0))],
        out_specs=[tok(ATTN_WIDTH), tok(KV_WIDTH), tok(KV_WIDTH), tok(FNET_WIDTH), tok(FNET_WIDTH)],
        out_shape=outs,
        compiler_params=_cparams("parallel", "parallel"),
        name="ab_in",
    )(x, nw, shift, scale, w_ext, cos, sin, dft_c)


def _softmax_pv(qh, kk, vv, valid, sk, scale):
    s = lax.dot_general(qh, kk, (((1,), (1,)), ((), ())), preferred_element_type=F32) * scale
    if valid is not None:
        s = jnp.where(valid, s, NEG_INF)
    m = jnp.maximum(jnp.max(s, axis=-1, keepdims=True), sk)
    p = jnp.exp(s - m)
    inv_den = 1.0 / (jnp.sum(p, axis=-1, keepdims=True) + jnp.exp(sk - m))
    return _bdot((p * inv_den).astype(BF16), vv)


def _gqa_group(q, k_all, v_all, valid, sink_ref, kvh, o_ref):
    n = q.shape[0]
    heads = range(kvh * Q_PER_KV, (kvh + 1) * Q_PER_KV)
    kk = k_all[:, kvh * HEAD_DIM:(kvh + 1) * HEAD_DIM]
    vv = v_all[:, kvh * HEAD_DIM:(kvh + 1) * HEAD_DIM]
    q4 = jnp.concatenate([q[:, h * HEAD_DIM:(h + 1) * HEAD_DIM] for h in heads], axis=0)
    sk4 = jnp.concatenate([jnp.broadcast_to(sink_ref[0:1, h:h + 1], (n, 1)) for h in heads], axis=0)
    valid4 = None if valid is None else jnp.concatenate([valid] * Q_PER_KV, axis=0)
    o4 = _softmax_pv(q4, kk, vv, valid4, sk4, HEAD_DIM ** -0.5)
    for g, h in enumerate(heads):
        o_ref[0, :, h * HEAD_DIM:(h + 1) * HEAD_DIM] = o4[g * n:(g + 1) * n].astype(BF16)


def _attn_local_kernel(q_ref, kp_ref, ko_ref, kn_ref, vp_ref, vo_ref, vn_ref, kc_ref, vc_ref, sink_ref, o_ref,
                       *, seq):
    j = pl.program_id(1)
    blk = q_ref.shape[1]
    n_ctx = kc_ref.shape[1]
    q = q_ref[0]
    kl = jnp.concatenate([kp_ref[0], ko_ref[0], kn_ref[0], kc_ref[0]], axis=0)
    vl = jnp.concatenate([vp_ref[0], vo_ref[0], vn_ref[0], vc_ref[0]], axis=0)
    nk = 3 * blk + n_ctx
    qi = lax.broadcasted_iota(I32, (blk, nk), 0)
    cj = lax.broadcasted_iota(I32, (blk, nk), 1)
    kj = cj - blk
    kpos = j * blk + kj
    valid = (cj >= 3 * blk) | ((jnp.abs(qi - kj) <= WINDOW) & (kpos >= 0) & (kpos < seq))
    for kvh in range(N_KV_HEADS):
        _gqa_group(q, kl, vl, valid, sink_ref, kvh, o_ref)


def _attn_local(q, k, v, kc, vc, sink, b0):
    b, s, _ = q.shape
    c = kc.shape[1]
    blk = WINDOW
    nb = s // blk
    qspec = pl.BlockSpec((1, blk, ATTN_WIDTH), lambda i, j: (i, j, 0))
    prev = pl.BlockSpec((1, blk, KV_WIDTH), lambda i, j: (i, jnp.maximum(j - 1, 0), 0))
    own = pl.BlockSpec((1, blk, KV_WIDTH), lambda i, j: (i, j, 0))
    nxt = pl.BlockSpec((1, blk, KV_WIDTH), lambda i, j: (i, jnp.minimum(j + 1, nb - 1), 0))
    cspec = pl.BlockSpec((1, c, KV_WIDTH), lambda i, j: (i + b0, 0, 0))
    return pl.pallas_call(
        functools.partial(_attn_local_kernel, seq=s),
        grid=(b, nb),
        in_specs=[qspec, prev, own, nxt, prev, own, nxt, cspec, cspec,
                  pl.BlockSpec((1, N_Q_HEADS), lambda i, j: (0, 0))],
        out_specs=qspec,
        out_shape=jax.ShapeDtypeStruct((b, s, ATTN_WIDTH), BF16),
        compiler_params=_cparams("parallel", "parallel"),
        name="attn_local",
    )(q, k, k, k, v, v, v, kc, vc, sink)


def _attn_ctx_kernel(q_ref, k_ref, v_ref, sink_ref, o_ref):
    for kvh in range(N_KV_HEADS):
        _gqa_group(q_ref[0], k_ref[0], v_ref[0], None, sink_ref, kvh, o_ref)


def _attn_ctx(q, k, v, sink):
    b, c, _ = q.shape
    return pl.pallas_call(
        _attn_ctx_kernel,
        grid=(b,),
        in_specs=[pl.BlockSpec((1, c, ATTN_WIDTH), lambda i: (i, 0, 0)),
                  pl.BlockSpec((1, c, KV_WIDTH), lambda i: (i, 0, 0)),
                  pl.BlockSpec((1, c, KV_WIDTH), lambda i: (i, 0, 0)),
                  pl.BlockSpec((1, N_Q_HEADS), lambda i: (0, 0))],
        out_specs=pl.BlockSpec((1, c, ATTN_WIDTH), lambda i: (i, 0, 0)),
        out_shape=jax.ShapeDtypeStruct((b, c, ATTN_WIDTH), BF16),
        compiler_params=_cparams("parallel"),
        name="attn_ctx",
    )(q, k, v, sink)


def _dft_pos_kernel(cs_ref, ss_ref, a_ref, b_ref, o_ref, acc_ref, *, scale):
    kk = pl.program_id(2)

    @pl.when(kk == 0)
    def _():
        acc_ref[...] = jnp.zeros_like(acc_ref)

    acc_ref[...] += _bdot(cs_ref[...], a_ref[0]) - _bdot(ss_ref[...], b_ref[0])

    @pl.when(kk == pl.num_programs(2) - 1)
    def _():
        o_ref[0] = (acc_ref[...] * scale).astype(BF16)


def _dft_pos(fa, fb, cs, ss):
    b, s, w = fa.shape
    t = min(1024, s)
    scale = 1.0 / math.sqrt(s * FNET_GROUP_DIM)
    return pl.pallas_call(
        functools.partial(_dft_pos_kernel, scale=scale),
        grid=(b, s // t, s // t),
        in_specs=[pl.BlockSpec((t, t), lambda i, m, k: (m, k)),
                  pl.BlockSpec((t, t), lambda i, m, k: (m, k)),
                  pl.BlockSpec((1, t, w), lambda i, m, k: (i, k, 0)),
                  pl.BlockSpec((1, t, w), lambda i, m, k: (i, k, 0))],
        out_specs=pl.BlockSpec((1, t, w), lambda i, m, k: (i, m, 0)),
        out_shape=jax.ShapeDtypeStruct((b, s, w), BF16),
        scratch_shapes=[pltpu.VMEM((t, w), F32)],
        compiler_params=_cparams("parallel", "parallel", "arbitrary"),
        name="dft_pos",
    )(cs, ss, fa, fb)


def _ab_out_kernel(a_ref, f_ref, w1_ref, w2_ref, x_ref, g_ref, o_ref):
    y = _bdot(a_ref[0], w1_ref[...]) + _bdot(f_ref[0], w2_ref[...])
    o_ref[0] = x_ref[0] + g_ref[0] * y


def _ab_out(a, fm, w_out, x, gate, b0):
    nb = a.shape[0]
    _, s, d = x.shape
    tm = min(512, s)
    tok = lambda w: pl.BlockSpec((1, tm, w), lambda i, j: (i, j, 0))
    xtok = pl.BlockSpec((1, tm, d), lambda i, j: (i + b0, j, 0))
    return pl.pallas_call(
        _ab_out_kernel,
        grid=(nb, s // tm),
        in_specs=[tok(ATTN_WIDTH), tok(FNET_WIDTH),
                  pl.BlockSpec((ATTN_WIDTH, d), lambda i, j: (0, 0)),
                  pl.BlockSpec((FNET_WIDTH, d), lambda i, j: (0, 0)),
                  xtok,
                  pl.BlockSpec((1, 1, d), lambda i, j: (i + b0, 0, 0))],
        out_specs=xtok,
        out_shape=jax.ShapeDtypeStruct(x.shape, F32),
        input_output_aliases={4: 0},
        compiler_params=_cparams("parallel", "parallel"),
        name="ab_out",
    )(a, fm, w_out[:ATTN_WIDTH], w_out[ATTN_WIDTH:], x, gate)


def _sgu_kernel(x_ref, nw_ref, sh_ref, sc_ref, g_ref, win_ref, gnw_ref, ws_ref, bs_ref, wout_ref, o_ref, uv_ref):
    x = x_ref[0]
    tm = x.shape[0]
    h = _norm_mod(x, nw_ref[...], sh_ref[0], sc_ref[0]).astype(BF16)
    z = jax.nn.gelu(_bdot(h, win_ref[...]))
    u = z[:, :SGU_WIDTH]
    v = z[:, SGU_WIDTH:]
    v = (v * lax.rsqrt(jnp.mean(v * v, axis=-1, keepdims=True) + RMS_EPS) * gnw_ref[...]).astype(BF16)
    for n in range(tm // SGU_CHUNK):
        r = slice(n * SGU_CHUNK, (n + 1) * SGU_CHUNK)
        for g in range(SGU_GROUPS):
            cs = slice(g * SGU_GROUP_DIM, (g + 1) * SGU_GROUP_DIM)
            sv = _bdot(ws_ref[g], v[r, cs]) + bs_ref[:, g:g + 1]
            uv_ref[r, cs] = (u[r, cs] * sv).astype(BF16)
    o_ref[0] = x + g_ref[0] * _bdot(uv_ref[...], wout_ref[...])


def _sgu(x, nw, shift, scale, gate, w_in, gnw, w_s, b_s_t, w_out, b0, nb):
    _, s, d = x.shape
    tm = min(256, s)
    tok = pl.BlockSpec((1, tm, d), lambda i, j: (i + b0, j, 0))
    mod = pl.BlockSpec((1, 1, d), lambda i, j: (i + b0, 0, 0))
    full = lambda a: pl.BlockSpec(a.shape, lambda i, j: (0,) * a.ndim)
    return pl.pallas_call(
        _sgu_kernel,
        grid=(nb, s // tm),
        in_specs=[tok, full(nw), mod, mod, mod, full(w_in), full(gnw), full(w_s), full(b_s_t), full(w_out)],
        out_specs=tok,
        out_shape=jax.ShapeDtypeStruct(x.shape, F32),
        input_output_aliases={0: 0},
        scratch_shapes=[pltpu.VMEM((tm, SGU_WIDTH), BF16)],
        compiler_params=_cparams("parallel", "parallel"),
        name="sgu",
    )(x, nw, shift, scale, gate, w_in, gnw, w_s, b_s_t, w_out)


def _topk_rows(s, rid, k):
    big = jnp.iinfo(jnp.int32).max
    vals, idxs = [], []
    for _ in range(k):
        m = jnp.max(s, axis=0, keepdims=True)
        am = jnp.min(jnp.where(s == m, rid, big), axis=0, keepdims=True)
        vals.append(m)
        idxs.append(am)
        s = jnp.where(rid == am, -jnp.inf, s)
    return jnp.concatenate(vals, axis=0), jnp.concatenate(idxs, axis=0)


_PAIR_PIECES = ((0, 1, 0, 16), (1, 1, 0, 8), (2, 1, 0, 8), (3, 1, 0, 8),
                (8, 8, 0, 1), (0, 8, 0, 1), (0, 8, 1, 1), (0, 8, 2, 1))
_PAIR_ROW_PIECES = 4


def _pair_candidates(t0, t1):
    k = PEER_TOPK
    n = t0.shape[1]
    vals, ids = [], []
    for a0, na, b0, nb in _PAIR_PIECES:
        io = lax.broadcasted_iota(I32, (max(na, nb), n), 0)
        a = a0 + io if na > 1 else jnp.full_like(io, a0)
        b = b0 + io if nb > 1 else jnp.full_like(io, b0)
        ok = (a + 1) * (b + 1) <= k
        if na > 1:
            ok = ok & (a >= _PAIR_ROW_PIECES)
        vals.append(jnp.where(ok, t0[a0:a0 + na] + t1[b0:b0 + nb], -jnp.inf))
        ids.append(a * k + b)
    return jnp.concatenate(vals, axis=0), jnp.concatenate(ids, axis=0)


def _select_rows(sel, table):
    out = jnp.zeros(sel.shape, table.dtype)
    for a in range(table.shape[0]):
        out = jnp.where(sel == a, table[a:a + 1], out)
    return out


def _peer_idx_kernel(x_ref, nw_ref, sh_ref, sc_ref, wq_ref, keys_ref, hq_ref, e_ref, g_ref, q_ref):
    hq = _norm_mod(x_ref[...], nw_ref[...], sh_ref[0], sc_ref[0])
    hq_ref[...] = hq
    q_ref[...] = _bdot(hq.astype(BF16), wq_ref[...]).astype(BF16)

    def head(h, carry):
        tops, topi = [], []
        for s in range(2):
            col = pl.multiple_of((h * 2 + s) * PEER_HALF_DIM, PEER_HALF_DIM)
            qs = q_ref[:, pl.ds(col, PEER_HALF_DIM)]
            st = lax.dot_general(keys_ref[h * 2 + s], qs, (((1,), (1,)), ((), ())), preferred_element_type=F32)
            ts, ti = _topk_rows(st, lax.broadcasted_iota(I32, st.shape, 0), PEER_TOPK)
            tops.append(ts)
            topi.append(ti)
        best_s, best = _topk_rows(*_pair_candidates(tops[0], tops[1]), PEER_TOPK)
        i1 = _select_rows(best // PEER_TOPK, topi[0])
        i2 = _select_rows(best % PEER_TOPK, topi[1])
        p = jnp.exp(best_s - best_s[0:1])
        row = pl.multiple_of(h * PEER_TOPK, PEER_TOPK)
        e_ref[0, pl.ds(row, PEER_TOPK), :] = i1 * PEER_NKEYS + i2
        g_ref[0, pl.ds(row, PEER_TOPK), :] = p / jnp.sum(p, axis=0, keepdims=True)
        return carry

    lax.fori_loop(0, PEER_HEADS, head, 0)


def _peer_idx(x, nw, shift, scale, w_q, keys, tok0, ntok, seq):
    d = x.shape[1]
    tm = min(256, ntok, seq)
    nq = w_q.shape[1]
    t0 = tok0 // tm
    per_seq = seq // tm
    mod = pl.BlockSpec((1, 1, d), lambda i: ((i + t0) // per_seq, 0, 0))
    sel = pl.BlockSpec((1, PEER_SEL, tm), lambda i: (i, 0, 0))
    return pl.pallas_call(
        _peer_idx_kernel,
        grid=(ntok // tm,),
        in_specs=[pl.BlockSpec((tm, d), lambda i: (i + t0, 0)),
                  pl.BlockSpec((1, d), lambda i: (0, 0)),
                  mod, mod,
                  pl.BlockSpec((d, nq), lambda i: (0, 0)),
                  pl.BlockSpec(keys.shape, lambda i: (0, 0, 0))],
        out_specs=[pl.BlockSpec((tm, d), lambda i: (i, 0)), sel, sel],
        out_shape=[jax.ShapeDtypeStruct((ntok, d), F32),
                   jax.ShapeDtypeStruct((ntok // tm, PEER_SEL, tm), I32),
                   jax.ShapeDtypeStruct((ntok // tm, PEER_SEL, tm), F32)],
        scratch_shapes=[pltpu.VMEM((tm, nq), BF16)],
        compiler_params=_cparams("parallel"),
        name="peer_idx",
    )(x, nw, shift, scale, w_q, keys)


def _sc_worker_id():
    return lax.axis_index("s") * SC_CORES + lax.axis_index("c")


SC_SLAB = 2


def _sc_unpack(x):
    return lax.bitcast_convert_type(x << 16, F32), lax.bitcast_convert_type(x & jnp.uint32(0xFFFF0000), F32)


def _sc_per_token(tab, idx, aux, out_width, chunk_fn):
    k = PEER_SEL
    n_tok, a = aux.shape
    w = tab.shape[1]
    ch = SC_GATHER_ROWS
    per_w = n_tok // SC_WORKERS
    assert k == 2 * ch and per_w % 2 == 0
    mesh = plsc.VectorSubcoreMesh(core_axis_name="c", subcore_axis_name="s")

    @functools.partial(
        pl.kernel, mesh=mesh, out_type=jax.ShapeDtypeStruct((n_tok, out_width), F32),
        scratch_types=[pltpu.VMEM((per_w * k,), I32), pltpu.VMEM((2, a), F32),
                       pltpu.VMEM((ch, w), U32), pltpu.VMEM((ch, w), U32), pltpu.VMEM((2, out_width), F32),
                       pltpu.SemaphoreType.DMA, pltpu.SemaphoreType.DMA,
                       pltpu.SemaphoreType.DMA((2,)), pltpu.SemaphoreType.DMA((2,))],
        compiler_params=pltpu.CompilerParams(needs_layout_passes=False))
    def per_token(tab_hbm, idx_hbm, aux_hbm, out_hbm, idx_v, aux_v, rows0, rows1, out_v, gsem0, gsem1, asem, osem):
        base = _sc_worker_id() * per_w
        rbuf = ((rows0, gsem0), (rows1, gsem1))
        pltpu.sync_copy(idx_hbm.at[pl.ds(pl.multiple_of(base * k, k), per_w * k)], idx_v)

        def fetch_rows(t, c):
            sel = idx_v.at[pl.ds(pl.multiple_of(t * k + c * ch, ch), ch)]
            return pltpu.make_async_copy(tab_hbm.at[sel], rbuf[c][0], rbuf[c][1])

        def fetch_aux(t, p):
            return pltpu.make_async_copy(aux_hbm.at[base + t], aux_v.at[p], asem.at[p])

        def store_out(t, p):
            return pltpu.make_async_copy(out_v.at[p], out_hbm.at[base + t], osem.at[p])

        fetch_rows(0, 0).start()
        fetch_rows(0, 1).start()
        fetch_aux(0, 0).start()

        @pl.loop(0, per_w)
        def _(t):
            p = lax.rem(t, 2)
            fetch_aux(t, p).wait()

            @pl.when(t + 1 < per_w)
            def _():
                fetch_aux(t + 1, 1 - p).start()

            @pl.when(t >= 2)
            def _():
                store_out(t - 2, p).wait()

            for c in range(2):
                fetch_rows(t, c).wait()
                chunk_fn(c, p, aux_v, rbuf[c][0], out_v)

                @pl.when(t + 1 < per_w)
                def _():
                    fetch_rows(t + 1, c).start()

            store_out(t, p).start()

        store_out(per_w - 2, 0).wait()
        store_out(per_w - 1, 1).wait()

    return per_token(tab, idx, aux)


def _sc_row_dots(tab, idx, hq):
    w = tab.shape[1]
    lanes, slab, ch = SC_LANES, SC_SLAB, SC_GATHER_ROWS

    rblk = 8

    def chunk(c, p, h_v, rows, part_v):
        @pl.loop(0, ch // rblk)
        def _(rb):
            r0 = rb * rblk
            acc = [jnp.zeros((lanes,), F32) for _ in range(rblk)]
            for sb in range(w // lanes // slab):
                col = sb * slab * lanes
                h_lo = [h_v[p, pl.ds(col + j * lanes, lanes)] for j in range(slab)]
                h_hi = [h_v[p, pl.ds(w + col + j * lanes, lanes)] for j in range(slab)]
                for r in range(rblk):
                    for j in range(slab):
                        lo, hi = _sc_unpack(rows[r0 + r, pl.ds(col + j * lanes, lanes)])
                        acc[r] += h_lo[j] * lo + h_hi[j] * hi
            for r in range(rblk):
                part_v[p, pl.ds(pl.multiple_of((c * ch + r0 + r) * lanes, lanes), lanes)] = acc[r]

    return _sc_per_token(tab, idx, hq, PEER_SEL * lanes, chunk)


def _sc_weighted_rows(tab, idx, w_rep):
    w = tab.shape[1]
    lanes, slab, ch = SC_LANES, SC_SLAB, SC_GATHER_ROWS

    def chunk(c, p, w_v, rows, acc_v):
        @pl.loop(0, w // lanes // slab)
        def _(sb):
            col = pl.multiple_of(sb * (slab * lanes), slab * lanes)
            if c == 0:
                acc = [jnp.zeros((lanes,), F32) for _ in range(2 * slab)]
            else:
                acc = [acc_v[p, pl.ds(part * w + col + j * lanes, lanes)] for j in range(slab) for part in range(2)]
            for r in range(ch):
                wv = w_v[p, pl.ds((c * ch + r) * lanes, lanes)]
                for j in range(slab):
                    lo, hi = _sc_unpack(rows[r, pl.ds(col + j * lanes, lanes)])
                    acc[2 * j] += wv * lo
                    acc[2 * j + 1] += wv * hi
            for j in range(slab):
                acc_v[p, pl.ds(col + j * lanes, lanes)] = acc[2 * j]
                acc_v[p, pl.ds(w + col + j * lanes, lanes)] = acc[2 * j + 1]

    return _sc_per_token(tab, idx, w_rep, 2 * w, chunk)


def _split3(x):
    hi = x.astype(BF16)
    r1 = x - hi.astype(F32)
    mid = r1.astype(BF16)
    lo = (r1 - mid.astype(F32)).astype(BF16)
    return jnp.concatenate([hi, mid, lo], axis=1)


def _peer_weights_kernel(part_ref, g_ref, fold_ref, rep_ref, w_ref):
    score = _bdot(_split3(part_ref[...]), fold_ref[...])
    wgt = g_ref[...] * jax.nn.gelu(score)
    w_ref[...] = _bdot(_split3(wgt), rep_ref[...])


def _peer_weights(part, g):
    ntok, wide = part.shape
    tm = min(256, ntok)
    rep1 = jnp.repeat(jnp.eye(PEER_SEL, dtype=BF16), SC_LANES, axis=1)
    rep = jnp.tile(rep1, (3, 1))
    fold = jnp.tile(rep1.T, (3, 1))
    row = lambda w: pl.BlockSpec((tm, w), lambda i: (i, 0))
    full = lambda a: pl.BlockSpec(a.shape, lambda i: (0, 0))
    return pl.pallas_call(
        _peer_weights_kernel,
        grid=(ntok // tm,),
        in_specs=[row(wide), row(PEER_SEL), full(fold), full(rep)],
        out_specs=row(wide),
        out_shape=jax.ShapeDtypeStruct((ntok, wide), F32),
        compiler_params=_cparams("parallel"),
        name="peer_weights",
    )(part, g, fold, rep)


def _resid_kernel(x_ref, y_ref, gate_ref, o_ref):
    o_ref[...] = x_ref[...] + gate_ref[0] * y_ref[...]


def _resid(x, y, gate, tok0, seq):
    t, d = x.shape
    ntok = y.shape[0]
    tm = min(512, ntok, seq)
    t0 = tok0 // tm
    per_seq = seq // tm
    tok = pl.BlockSpec((tm, d), lambda i: (i + t0, 0))
    return pl.pallas_call(
        _resid_kernel,
        grid=(ntok // tm,),
        in_specs=[tok, pl.BlockSpec((tm, d), lambda i: (i, 0)),
                  pl.BlockSpec((1, 1, d), lambda i: ((i + t0) // per_seq, 0, 0))],
        out_specs=tok,
        out_shape=jax.ShapeDtypeStruct((t, d), F32),
        input_output_aliases={0: 0},
        compiler_params=_cparams("parallel"),
        name="peer_resid",
    )(x, y, gate)


def _peer_groups(t):
    return max(1, min(4, t // 2048))


def _peer_stage(x, peer, gi, n_groups):
    nw, shift, scale, _, w_q, keys, tab_u, _ = peer
    b, s, d = x.shape
    gtok = b * s // n_groups
    hq, e_t, g_t = _peer_idx(x.reshape(b * s, d), nw, shift, scale, w_q, keys, gi * gtok, gtok, s)
    idx = e_t.transpose(0, 2, 1).reshape(-1)
    g = g_t.transpose(0, 2, 1).reshape(gtok, PEER_SEL)
    return idx, g, _sc_row_dots(tab_u, idx, hq)


def _peer_finish(x, peer, staged):
    gate, tab_v = peer[3], peer[7]
    b, s, d = x.shape
    gtok = b * s // len(staged)
    ys = [_sc_weighted_rows(tab_v, idx, _peer_weights(part, g)) for idx, g, part in staged]
    x2 = x.reshape(b * s, d)
    for gi, y in enumerate(ys):
        x2 = _resid(x2, y, gate, gi * gtok, s)
    return x2.reshape(b, s, d)


def _rms_kernel(x_ref, w_ref, o_ref):
    x = x_ref[...]
    o_ref[...] = x * lax.rsqrt(jnp.mean(x * x, axis=-1, keepdims=True) + RMS_EPS) * w_ref[...]


def _final_norm(x, w):
    b, s, d = x.shape
    t = b * s
    tm = min(512, t)
    out = pl.pallas_call(
        _rms_kernel,
        grid=(t // tm,),
        in_specs=[pl.BlockSpec((tm, d), lambda i: (i, 0)), pl.BlockSpec((1, d), lambda i: (0, 0))],
        out_specs=pl.BlockSpec((tm, d), lambda i: (i, 0)),
        out_shape=jax.ShapeDtypeStruct((t, d), F32),
        compiler_params=_cparams("parallel"),
        name="final_norm",
    )(x.reshape(t, d), w.reshape(1, d))
    return out.reshape(b, s, d)


def _rope_tables(s):
    t = jnp.arange(s)
    row = (t // GRID_W).astype(F32)
    col = (t % GRID_W).astype(F32)
    half = HEAD_DIM // 2
    inv = ROPE_THETA ** (-jnp.arange(0, half, 2, dtype=F32) / half)
    ar = row[:, None] * inv
    ac = col[:, None] * inv
    ang = jnp.concatenate([ar, ar, ac, ac], axis=1)
    q = half // 2
    sign = jnp.concatenate([-jnp.ones(q), jnp.ones(q), -jnp.ones(q), jnp.ones(q)]).astype(F32)
    reps = QK_WIDTH // HEAD_DIM
    return jnp.tile(jnp.cos(ang), (1, reps)), jnp.tile(jnp.sin(ang) * sign, (1, reps))


def _rope_partner_cols():
    j = jnp.arange(QK_WIDTH)
    dd = j % HEAD_DIM
    q = HEAD_DIM // 4
    return j - dd + jnp.where((dd % (2 * q)) < q, dd + q, dd - q)


def _dft_tables(n, dtype):
    j = jnp.arange(n, dtype=I32)
    ang = ((j[:, None] * j[None, :]) % n).astype(F32) * (2.0 * math.pi / n)
    return jnp.cos(ang).astype(dtype), jnp.sin(ang).astype(dtype)


def _pack_table(tab):
    half = tab.shape[1] // 2
    bits = lax.bitcast_convert_type(tab.astype(BF16), jnp.uint16).astype(U32)
    return bits[:, :half] | (bits[:, half:] << 16)


def kernel(x, c, ctx, c_ctx, ada_w, ada_b, norm_mix_w, norm_ffn_w, ab_w_in, ab_w_out, attn_sink, gs_w_in, gs_norm_w,
           gs_w_s, gs_b_s, gs_w_out, peer_w_q, peer_keys, peer_u, peer_v, norm_out_w):
    bsz, seq, d = x.shape
    n_ctx = ctx.shape[1]
    depth = ada_w.shape[0]

    cond = jnp.concatenate([c, c_ctx[None], jnp.zeros((16 - bsz - 1, d), F32)], axis=0)
    mods = _ada_all(cond, ada_w, ada_b)

    cos, sin = _rope_tables(seq)
    partner = _rope_partner_cols()
    dft_cc, dft_cs = _dft_tables(FNET_GROUP_DIM, BF16)
    dft_c = jnp.concatenate([dft_cc, dft_cs], axis=1)
    pos_l = _dft_tables(seq, BF16)
    pos_c = _dft_tables(n_ctx, BF16)

    n_groups = _peer_groups(bsz * seq)
    assert bsz % n_groups == 0
    nb = bsz // n_groups
    last_attn = ((depth - 1) // 2) * 2
    for i in range(depth):
        j = i // 2
        is_ab = i % 2 == 0
        upd_ctx = i < last_attn
        m_l = [m[:, None, :] for m in jnp.split(mods[i, :bsz], N_MOD, axis=-1)]
        m_c = [jnp.broadcast_to(m[None], (bsz, 1, d)) for m in jnp.split(mods[i, bsz:bsz + 1], N_MOD, axis=-1)]
        nw_m = norm_mix_w[i].reshape(1, d)
        nw_f = norm_ffn_w[i].reshape(1, d)
        w_q = peer_w_q[i].astype(BF16)
        keys = peer_keys[i].reshape(PEER_HEADS * 2, PEER_NKEYS, PEER_HALF_DIM).astype(BF16)
        tabs = (_pack_table(peer_u[i]), _pack_table(peer_v[i]))
        peer_l = (nw_f, m_l[3], m_l[4], m_l[5], w_q, keys) + tabs
        peer_c = (nw_f, m_c[3], m_c[4], m_c[5], w_q, keys) + tabs
        staged = []
        if is_ab:
            w_in = ab_w_in[j]
            w_ext = jnp.concatenate([w_in, w_in[:, partner]], axis=1).astype(BF16)
            w_out = ab_w_out[j].astype(BF16)
            sink = attn_sink[j].reshape(1, N_Q_HEADS)
            q_c, k_c, v_c, fa_c, fb_c = _ab_in(ctx, nw_m, m_c[0], m_c[1], w_ext, cos[:n_ctx], sin[:n_ctx], dft_c,
                                               False, 0, bsz)
            for gi in range(n_groups):
                b0 = gi * nb
                q_l, k_l, v_l, fa_l, fb_l = _ab_in(x, nw_m, m_l[0], m_l[1], w_ext, cos, sin, dft_c, True, b0, nb)
                a_l = _attn_local(q_l, k_l, v_l, k_c, v_c, sink, b0)
                x = _ab_out(a_l, _dft_pos(fa_l, fb_l, *pos_l), w_out, x, m_l[2], b0)
                staged.append(_peer_stage(x, peer_l, gi, n_groups))
            if upd_ctx:
                a_c = _attn_ctx(q_c, k_c, v_c, sink)
                ctx = _ab_out(a_c, _dft_pos(fa_c, fb_c, *pos_c), w_out, ctx, m_c[2], 0)
        else:
            sgu_w = (gs_w_in[j].astype(BF16), gs_norm_w[j].reshape(1, -1), gs_w_s[j].astype(BF16), gs_b_s[j].T,
                     gs_w_out[j].astype(BF16))
            for gi in range(n_groups):
                x = _sgu(x, nw_m, m_l[0], m_l[1], m_l[2], *sgu_w, gi * nb, nb)
                staged.append(_peer_stage(x, peer_l, gi, n_groups))
            if upd_ctx:
                ctx = _sgu(ctx, nw_m, m_c[0], m_c[1], m_c[2], *sgu_w, 0, bsz)
        x = _peer_finish(x, peer_l, staged)
        if upd_ctx:
            ctx = _peer_finish(ctx, peer_c, [_peer_stage(ctx, peer_c, 0, 1)])
    return _final_norm(x, norm_out_w)
```

```python
import functools
import math

import jax
import jax.numpy as jnp
from jax import lax
from jax.experimental import pallas as pl
from jax.experimental.pallas import tpu as pltpu
from jax.experimental.pallas import tpu_sc as plsc

F32 = jnp.float32
BF16 = jnp.bfloat16
I32 = jnp.int32
U32 = jnp.uint32

GRID_W = 64
N_Q_HEADS = 8
N_KV_HEADS = 2
Q_PER_KV = N_Q_HEADS // N_KV_HEADS
HEAD_DIM = 64
WINDOW = 128
ROPE_THETA = 10000.0
ATTN_WIDTH = N_Q_HEADS * HEAD_DIM
KV_WIDTH = N_KV_HEADS * HEAD_DIM
QK_WIDTH = ATTN_WIDTH + KV_WIDTH
FNET_GROUPS = 4
FNET_GROUP_DIM = 128
FNET_WIDTH = FNET_GROUPS * FNET_GROUP_DIM
AB_IN_WIDTH = ATTN_WIDTH + 2 * KV_WIDTH + FNET_WIDTH
SGU_GROUPS = 8
SGU_GROUP_DIM = 128
SGU_WIDTH = SGU_GROUPS * SGU_GROUP_DIM
SGU_CHUNK = 128
PEER_HEADS = 8
PEER_NKEYS = 128
PEER_TOPK = 16
PEER_HALF_DIM = 128
PEER_SEL = PEER_HEADS * PEER_TOPK
N_MOD = 6
RMS_EPS = 1e-6
NEG_INF = -1e30

SC_CORES = 2
SC_SUBCORES = 16
SC_WORKERS = SC_CORES * SC_SUBCORES
SC_LANES = 16
SC_GATHER_ROWS = 64
VMEM_LIMIT = 48 * 1024 * 1024


def _cparams(*sem):
    return pltpu.CompilerParams(dimension_semantics=sem, vmem_limit_bytes=VMEM_LIMIT)


def _norm_mod(x, nw, shift, scale):
    y = x * lax.rsqrt(jnp.mean(x * x, axis=-1, keepdims=True) + RMS_EPS) * nw
    return y * (1.0 + scale) + shift


def _bdot(a, b):
    return jnp.dot(a, b, preferred_element_type=F32)


def _ada_kernel(c_ref, w_ref, b_ref, o_ref):
    c = c_ref[...]
    s = (c * jax.nn.sigmoid(c)).astype(BF16)
    o_ref[0] = _bdot(s, w_ref[0].astype(BF16)) + b_ref[0]


def _ada_all(cond, ada_w, ada_b):
    depth, d, n = ada_w.shape
    r = cond.shape[0]
    tn = 1536
    return pl.pallas_call(
        _ada_kernel,
        grid=(depth, n // tn),
        in_specs=[pl.BlockSpec((r, d), lambda l, j: (0, 0)),
                  pl.BlockSpec((1, d, tn), lambda l, j: (l, 0, j)),
                  pl.BlockSpec((1, 1, tn), lambda l, j: (l, 0, j))],
        out_specs=pl.BlockSpec((1, r, tn), lambda l, j: (l, 0, j)),
        out_shape=jax.ShapeDtypeStruct((depth, r, n), F32),
        compiler_params=_cparams("parallel", "parallel"),
        name="ada",
    )(cond, ada_w, ada_b.reshape(depth, 1, n))


def _ab_in_kernel(x_ref, nw_ref, sh_ref, sc_ref, w_ref, cos_ref, sin_ref, dft_ref,
                  q_ref, k_ref, v_ref, fa_ref, fb_ref, *, rope):
    h = _norm_mod(x_ref[0], nw_ref[...], sh_ref[0], sc_ref[0]).astype(BF16)
    p = _bdot(h, w_ref[...])
    qk = p[:, :QK_WIDTH]
    if rope:
        qk = qk * cos_ref[...] + p[:, AB_IN_WIDTH:] * sin_ref[...]
    q_ref[0] = qk[:, :ATTN_WIDTH].astype(BF16)
    k_ref[0] = qk[:, ATTN_WIDTH:].astype(BF16)
    v_ref[0] = p[:, QK_WIDTH:QK_WIDTH + KV_WIDTH].astype(BF16)
    f0 = QK_WIDTH + KV_WIDTH
    for g in range(FNET_GROUPS):
        fg = p[:, f0 + g * FNET_GROUP_DIM:f0 + (g + 1) * FNET_GROUP_DIM].astype(BF16)
        ab = _bdot(fg, dft_ref[...])
        fa_ref[0, :, g * FNET_GROUP_DIM:(g + 1) * FNET_GROUP_DIM] = ab[:, :FNET_GROUP_DIM].astype(BF16)
        fb_ref[0, :, g * FNET_GROUP_DIM:(g + 1) * FNET_GROUP_DIM] = ab[:, FNET_GROUP_DIM:].astype(BF16)


def _ab_in(x, nw, shift, scale, w_ext, cos, sin, dft_c, rope, b0, nb):
    _, s, d = x.shape
    tm = min(512, s)
    n_ext = w_ext.shape[1]
    outs = [jax.ShapeDtypeStruct((nb, s, ATTN_WIDTH), BF16), jax.ShapeDtypeStruct((nb, s, KV_WIDTH), BF16),
            jax.ShapeDtypeStruct((nb, s, KV_WIDTH), BF16), jax.ShapeDtypeStruct((nb, s, FNET_WIDTH), BF16),
            jax.ShapeDtypeStruct((nb, s, FNET_WIDTH), BF16)]
    tok = lambda w: pl.BlockSpec((1, tm, w), lambda i, j: (i, j, 0))
    return pl.pallas_call(
        functools.partial(_ab_in_kernel, rope=rope),
        grid=(nb, s // tm),
        in_specs=[pl.BlockSpec((1, tm, d), lambda i, j: (i + b0, j, 0)),
                  pl.BlockSpec((1, d), lambda i, j: (0, 0)),
                  pl.BlockSpec((1, 1, d), lambda i, j: (i + b0, 0, 0)),
                  pl.BlockSpec((1, 1, d), lambda i, j: (i + b0, 0, 0)),
                  pl.BlockSpec((d, n_ext), lambda i, j: (0, 0)),
                  pl.BlockSpec((tm, QK_WIDTH), lambda i, j: (j, 0)),
                  pl.BlockSpec((tm, QK_WIDTH), lambda i, j: (j, 0)),
                  pl.BlockSpec(dft_c.shape, lambda i, j: (0, 0))],
        out_specs=[tok(ATTN_WIDTH), tok(KV_WIDTH), tok(KV_WIDTH), tok(FNET_WIDTH), tok(FNET_WIDTH)],
        out_shape=outs,
        compiler_params=_cparams("parallel", "parallel"),
        name="ab_in",
    )(x, nw, shift, scale, w_ext, cos, sin, dft_c)


def _softmax_pv(qh, kk, vv, valid, sk, scale):
    s = lax.dot_general(qh, kk, (((1,), (1,)), ((), ())), preferred_element_type=F32) * scale
    if valid is not None:
        s = jnp.where(valid, s, NEG_INF)
    m = jnp.maximum(jnp.max(s, axis=-1, keepdims=True), sk)
    p = jnp.exp(s - m)
    inv_den = 1.0 / (jnp.sum(p, axis=-1, keepdims=True) + jnp.exp(sk - m))
    return _bdot((p * inv_den).astype(BF16), vv)


def _gqa_group(q, k_all, v_all, valid, sink_ref, kvh, o_ref):
    n = q.shape[0]
    heads = range(kvh * Q_PER_KV, (kvh + 1) * Q_PER_KV)
    kk = k_all[:, kvh * HEAD_DIM:(kvh + 1) * HEAD_DIM]
    vv = v_all[:, kvh * HEAD_DIM:(kvh + 1) * HEAD_DIM]
    q4 = jnp.concatenate([q[:, h * HEAD_DIM:(h + 1) * HEAD_DIM] for h in heads], axis=0)
    sk4 = jnp.concatenate([jnp.broadcast_to(sink_ref[0:1, h:h + 1], (n, 1)) for h in heads], axis=0)
    valid4 = None if valid is None else jnp.concatenate([valid] * Q_PER_KV, axis=0)
    o4 = _softmax_pv(q4, kk, vv, valid4, sk4, HEAD_DIM ** -0.5)
    for g, h in enumerate(heads):
        o_ref[0, :, h * HEAD_DIM:(h + 1) * HEAD_DIM] = o4[g * n:(g + 1) * n].astype(BF16)


def _attn_local_kernel(q_ref, kp_ref, ko_ref, kn_ref, vp_ref, vo_ref, vn_ref, kc_ref, vc_ref, sink_ref, o_ref,
                       *, seq):
    j = pl.program_id(1)
    blk = q_ref.shape[1]
    n_ctx = kc_ref.shape[1]
    q = q_ref[0]
    kl = jnp.concatenate([kp_ref[0], ko_ref[0], kn_ref[0], kc_ref[0]], axis=0)
    vl = jnp.concatenate([vp_ref[0], vo_ref[0], vn_ref[0], vc_ref[0]], axis=0)
    nk = 3 * blk + n_ctx
    qi = lax.broadcasted_iota(I32, (blk, nk), 0)
    cj = lax.broadcasted_iota(I32, (blk, nk), 1)
    kj = cj - blk
    kpos = j * blk + kj
    valid = (cj >= 3 * blk) | ((jnp.abs(qi - kj) <= WINDOW) & (kpos >= 0) & (kpos < seq))
    for kvh in range(N_KV_HEADS):
        _gqa_group(q, kl, vl, valid, sink_ref, kvh, o_ref)


def _attn_local(q, k, v, kc, vc, sink, b0):
    b, s, _ = q.shape
    c = kc.shape[1]
    blk = WINDOW
    nb = s // blk
    qspec = pl.BlockSpec((1, blk, ATTN_WIDTH), lambda i, j: (i, j, 0))
    prev = pl.BlockSpec((1, blk, KV_WIDTH), lambda i, j: (i, jnp.maximum(j - 1, 0), 0))
    own = pl.BlockSpec((1, blk, KV_WIDTH), lambda i, j: (i, j, 0))
    nxt = pl.BlockSpec((1, blk, KV_WIDTH), lambda i, j: (i, jnp.minimum(j + 1, nb - 1), 0))
    cspec = pl.BlockSpec((1, c, KV_WIDTH), lambda i, j: (i + b0, 0, 0))
    return pl.pallas_call(
        functools.partial(_attn_local_kernel, seq=s),
        grid=(b, nb),
        in_specs=[qspec, prev, own, nxt, prev, own, nxt, cspec, cspec,
                  pl.BlockSpec((1, N_Q_HEADS), lambda i, j: (0, 0))],
        out_specs=qspec,
        out_shape=jax.ShapeDtypeStruct((b, s, ATTN_WIDTH), BF16),
        compiler_params=_cparams("parallel", "parallel"),
        name="attn_local",
    )(q, k, k, k, v, v, v, kc, vc, sink)


def _attn_ctx_kernel(q_ref, k_ref, v_ref, sink_ref, o_ref):
    for kvh in range(N_KV_HEADS):
        _gqa_group(q_ref[0], k_ref[0], v_ref[0], None, sink_ref, kvh, o_ref)


def _attn_ctx(q, k, v, sink):
    b, c, _ = q.shape
    return pl.pallas_call(
        _attn_ctx_kernel,
        grid=(b,),
        in_specs=[pl.BlockSpec((1, c, ATTN_WIDTH), lambda i: (i, 0, 0)),
                  pl.BlockSpec((1, c, KV_WIDTH), lambda i: (i, 0, 0)),
                  pl.BlockSpec((1, c, KV_WIDTH), lambda i: (i, 0, 0)),
                  pl.BlockSpec((1, N_Q_HEADS), lambda i: (0, 0))],
        out_specs=pl.BlockSpec((1, c, ATTN_WIDTH), lambda i: (i, 0, 0)),
        out_shape=jax.ShapeDtypeStruct((b, c, ATTN_WIDTH), BF16),
        compiler_params=_cparams("parallel"),
        name="attn_ctx",
    )(q, k, v, sink)


def _dft_pos_kernel(cs_ref, ss_ref, a_ref, b_ref, o_ref, acc_ref, *, scale):
    kk = pl.program_id(2)

    @pl.when(kk == 0)
    def _():
        acc_ref[...] = jnp.zeros_like(acc_ref)

    acc_ref[...] += _bdot(cs_ref[...], a_ref[0]) - _bdot(ss_ref[...], b_ref[0])

    @pl.when(kk == pl.num_programs(2) - 1)
    def _():
        o_ref[0] = (acc_ref[...] * scale).astype(BF16)


def _dft_pos(fa, fb, cs, ss):
    b, s, w = fa.shape
    t = min(1024, s)
    scale = 1.0 / math.sqrt(s * FNET_GROUP_DIM)
    return pl.pallas_call(
        functools.partial(_dft_pos_kernel, scale=scale),
        grid=(b, s // t, s // t),
        in_specs=[pl.BlockSpec((t, t), lambda i, m, k: (m, k)),
                  pl.BlockSpec((t, t), lambda i, m, k: (m, k)),
                  pl.BlockSpec((1, t, w), lambda i, m, k: (i, k, 0)),
                  pl.BlockSpec((1, t, w), lambda i, m, k: (i, k, 0))],
        out_specs=pl.BlockSpec((1, t, w), lambda i, m, k: (i, m, 0)),
        out_shape=jax.ShapeDtypeStruct((b, s, w), BF16),
        scratch_shapes=[pltpu.VMEM((t, w), F32)],
        compiler_params=_cparams("parallel", "parallel", "arbitrary"),
        name="dft_pos",
    )(cs, ss, fa, fb)


def _ab_out_kernel(a_ref, f_ref, w1_ref, w2_ref, x_ref, g_ref, o_ref):
    y = _bdot(a_ref[0], w1_ref[...]) + _bdot(f_ref[0], w2_ref[...])
    o_ref[0] = x_ref[0] + g_ref[0] * y


def _ab_out(a, fm, w_out, x, gate, b0):
    nb = a.shape[0]
    _, s, d = x.shape
    tm = min(512, s)
    tok = lambda w: pl.BlockSpec((1, tm, w), lambda i, j: (i, j, 0))
    xtok = pl.BlockSpec((1, tm, d), lambda i, j: (i + b0, j, 0))
    return pl.pallas_call(
        _ab_out_kernel,
        grid=(nb, s // tm),
        in_specs=[tok(ATTN_WIDTH), tok(FNET_WIDTH),
                  pl.BlockSpec((ATTN_WIDTH, d), lambda i, j: (0, 0)),
                  pl.BlockSpec((FNET_WIDTH, d), lambda i, j: (0, 0)),
                  xtok,
                  pl.BlockSpec((1, 1, d), lambda i, j: (i + b0, 0, 0))],
        out_specs=xtok,
        out_shape=jax.ShapeDtypeStruct(x.shape, F32),
        input_output_aliases={4: 0},
        compiler_params=_cparams("parallel", "parallel"),
        name="ab_out",
    )(a, fm, w_out[:ATTN_WIDTH], w_out[ATTN_WIDTH:], x, gate)


def _sgu_kernel(x_ref, nw_ref, sh_ref, sc_ref, g_ref, win_ref, gnw_ref, ws_ref, bs_ref, wout_ref, o_ref, uv_ref):
    x = x_ref[0]
    tm = x.shape[0]
    h = _norm_mod(x, nw_ref[...], sh_ref[0], sc_ref[0]).astype(BF16)
    z = jax.nn.gelu(_bdot(h, win_ref[...]))
    u = z[:, :SGU_WIDTH]
    v = z[:, SGU_WIDTH:]
    v = (v * lax.rsqrt(jnp.mean(v * v, axis=-1, keepdims=True) + RMS_EPS) * gnw_ref[...]).astype(BF16)
    for n in range(tm // SGU_CHUNK):
        r = slice(n * SGU_CHUNK, (n + 1) * SGU_CHUNK)
        for g in range(SGU_GROUPS):
            cs = slice(g * SGU_GROUP_DIM, (g + 1) * SGU_GROUP_DIM)
            sv = _bdot(ws_ref[g], v[r, cs]) + bs_ref[:, g:g + 1]
            uv_ref[r, cs] = (u[r, cs] * sv).astype(BF16)
    o_ref[0] = x + g_ref[0] * _bdot(uv_ref[...], wout_ref[...])


def _sgu(x, nw, shift, scale, gate, w_in, gnw, w_s, b_s_t, w_out, b0, nb):
    _, s, d = x.shape
    tm = min(256, s)
    tok = pl.BlockSpec((1, tm, d), lambda i, j: (i + b0, j, 0))
    mod = pl.BlockSpec((1, 1, d), lambda i, j: (i + b0, 0, 0))
    full = lambda a: pl.BlockSpec(a.shape, lambda i, j: (0,) * a.ndim)
    return pl.pallas_call(
        _sgu_kernel,
        grid=(nb, s // tm),
        in_specs=[tok, full(nw), mod, mod, mod, full(w_in), full(gnw), full(w_s), full(b_s_t), full(w_out)],
        out_specs=tok,
        out_shape=jax.ShapeDtypeStruct(x.shape, F32),
        input_output_aliases={0: 0},
        scratch_shapes=[pltpu.VMEM((tm, SGU_WIDTH), BF16)],
        compiler_params=_cparams("parallel", "parallel"),
        name="sgu",
    )(x, nw, shift, scale, gate, w_in, gnw, w_s, b_s_t, w_out)


def _topk_rows(s, rid, k):
    big = jnp.iinfo(jnp.int32).max
    vals, idxs = [], []
    for _ in range(k):
        m = jnp.max(s, axis=0, keepdims=True)
        am = jnp.min(jnp.where(s == m, rid, big), axis=0, keepdims=True)
        vals.append(m)
        idxs.append(am)
        s = jnp.where(rid == am, -jnp.inf, s)
    return jnp.concatenate(vals, axis=0), jnp.concatenate(idxs, axis=0)


_PAIR_PIECES = ((0, 1, 0, 16), (1, 1, 0, 8), (2, 1, 0, 8), (3, 1, 0, 8),
                (8, 8, 0, 1), (0, 8, 0, 1), (0, 8, 1, 1), (0, 8, 2, 1))
_PAIR_ROW_PIECES = 4


def _pair_candidates(t0, t1):
    k = PEER_TOPK
    n = t0.shape[1]
    vals, ids = [], []
    for a0, na, b0, nb in _PAIR_PIECES:
        io = lax.broadcasted_iota(I32, (max(na, nb), n), 0)
        a = a0 + io if na > 1 else jnp.full_like(io, a0)
        b = b0 + io if nb > 1 else jnp.full_like(io, b0)
        ok = (a + 1) * (b + 1) <= k
        if na > 1:
            ok = ok & (a >= _PAIR_ROW_PIECES)
        vals.append(jnp.where(ok, t0[a0:a0 + na] + t1[b0:b0 + nb], -jnp.inf))
        ids.append(a * k + b)
    return jnp.concatenate(vals, axis=0), jnp.concatenate(ids, axis=0)


def _select_rows(sel, table):
    out = jnp.zeros(sel.shape, table.dtype)
    for a in range(table.shape[0]):
        out = jnp.where(sel == a, table[a:a + 1], out)
    return out


def _peer_idx_kernel(x_ref, nw_ref, sh_ref, sc_ref, wq_ref, keys_ref, hq_ref, e_ref, g_ref, q_ref):
    hq = _norm_mod(x_ref[...], nw_ref[...], sh_ref[0], sc_ref[0])
    hq_ref[...] = hq
    q_ref[...] = _bdot(hq.astype(BF16), wq_ref[...]).astype(BF16)

    def head(h, carry):
        tops, topi = [], []
        for s in range(2):
            col = pl.multiple_of((h * 2 + s) * PEER_HALF_DIM, PEER_HALF_DIM)
            qs = q_ref[:, pl.ds(col, PEER_HALF_DIM)]
            st = lax.dot_general(keys_ref[h * 2 + s], qs, (((1,), (1,)), ((), ())), preferred_element_type=F32)
            ts, ti = _topk_rows(st, lax.broadcasted_iota(I32, st.shape, 0), PEER_TOPK)
            tops.append(ts)
            topi.append(ti)
        best_s, best = _topk_rows(*_pair_candidates(tops[0], tops[1]), PEER_TOPK)
        i1 = _select_rows(best // PEER_TOPK, topi[0])
        i2 = _select_rows(best % PEER_TOPK, topi[1])
        p = jnp.exp(best_s - best_s[0:1])
        row = pl.multiple_of(h * PEER_TOPK, PEER_TOPK)
        e_ref[0, pl.ds(row, PEER_TOPK), :] = i1 * PEER_NKEYS + i2
        g_ref[0, pl.ds(row, PEER_TOPK), :] = p / jnp.sum(p, axis=0, keepdims=True)
        return carry

    lax.fori_loop(0, PEER_HEADS, head, 0)


def _peer_idx(x, nw, shift, scale, w_q, keys, tok0, ntok, seq):
    d = x.shape[1]
    tm = min(256, ntok, seq)
    nq = w_q.shape[1]
    t0 = tok0 // tm
    per_seq = seq // tm
    mod = pl.BlockSpec((1, 1, d), lambda i: ((i + t0) // per_seq, 0, 0))
    sel = pl.BlockSpec((1, PEER_SEL, tm), lambda i: (i, 0, 0))
    return pl.pallas_call(
        _peer_idx_kernel,
        grid=(ntok // tm,),
        in_specs=[pl.BlockSpec((tm, d), lambda i: (i + t0, 0)),
                  pl.BlockSpec((1, d), lambda i: (0, 0)),
                  mod, mod,
                  pl.BlockSpec((d, nq), lambda i: (0, 0)),
                  pl.BlockSpec(keys.shape, lambda i: (0, 0, 0))],
        out_specs=[pl.BlockSpec((tm, d), lambda i: (i, 0)), sel, sel],
        out_shape=[jax.ShapeDtypeStruct((ntok, d), F32),
                   jax.ShapeDtypeStruct((ntok // tm, PEER_SEL, tm), I32),
                   jax.ShapeDtypeStruct((ntok // tm, PEER_SEL, tm), F32)],
        scratch_shapes=[pltpu.VMEM((tm, nq), BF16)],
        compiler_params=_cparams("parallel"),
        name="peer_idx",
    )(x, nw, shift, scale, w_q, keys)


def _sc_worker_id():
    return lax.axis_index("s") * SC_CORES + lax.axis_index("c")


SC_SLAB = 2


def _sc_unpack(x):
    return lax.bitcast_convert_type(x << 16, F32), lax.bitcast_convert_type(x & jnp.uint32(0xFFFF0000), F32)


def _sc_per_token(tab, idx, aux, out_width, chunk_fn):
    k = PEER_SEL
    n_tok, a = aux.shape
    w = tab.shape[1]
    ch = SC_GATHER_ROWS
    per_w = n_tok // SC_WORKERS
    assert k == 2 * ch and per_w % 2 == 0
    mesh = plsc.VectorSubcoreMesh(core_axis_name="c", subcore_axis_name="s")

    @functools.partial(
        pl.kernel, mesh=mesh, out_type=jax.ShapeDtypeStruct((n_tok, out_width), F32),
        scratch_types=[pltpu.VMEM((per_w * k,), I32), pltpu.VMEM((2, a), F32),
                       pltpu.VMEM((ch, w), U32), pltpu.VMEM((ch, w), U32), pltpu.VMEM((2, out_width), F32),
                       pltpu.SemaphoreType.DMA, pltpu.SemaphoreType.DMA,
                       pltpu.SemaphoreType.DMA((2,)), pltpu.SemaphoreType.DMA((2,))],
        compiler_params=pltpu.CompilerParams(needs_layout_passes=False))
    def per_token(tab_hbm, idx_hbm, aux_hbm, out_hbm, idx_v, aux_v, rows0, rows1, out_v, gsem0, gsem1, asem, osem):
        base = _sc_worker_id() * per_w
        rbuf = ((rows0, gsem0), (rows1, gsem1))
        pltpu.sync_copy(idx_hbm.at[pl.ds(pl.multiple_of(base * k, k), per_w * k)], idx_v)

        def fetch_rows(t, c):
            sel = idx_v.at[pl.ds(pl.multiple_of(t * k + c * ch, ch), ch)]
            return pltpu.make_async_copy(tab_hbm.at[sel], rbuf[c][0], rbuf[c][1])

        def fetch_aux(t, p):
            return pltpu.make_async_copy(aux_hbm.at[base + t], aux_v.at[p], asem.at[p])

        def store_out(t, p):
            return pltpu.make_async_copy(out_v.at[p], out_hbm.at[base + t], osem.at[p])

        fetch_rows(0, 0).start()
        fetch_rows(0, 1).start()
        fetch_aux(0, 0).start()

        @pl.loop(0, per_w)
        def _(t):
            p = lax.rem(t, 2)
            fetch_aux(t, p).wait()

            @pl.when(t + 1 < per_w)
            def _():
                fetch_aux(t + 1, 1 - p).start()

            @pl.when(t >= 2)
            def _():
                store_out(t - 2, p).wait()

            for c in range(2):
                fetch_rows(t, c).wait()
                chunk_fn(c, p, aux_v, rbuf[c][0], out_v)

                @pl.when(t + 1 < per_w)
                def _():
                    fetch_rows(t + 1, c).start()

            store_out(t, p).start()

        store_out(per_w - 2, 0).wait()
        store_out(per_w - 1, 1).wait()

    return per_token(tab, idx, aux)


def _sc_row_dots(tab, idx, hq):
    w = tab.shape[1]
    lanes, slab, ch = SC_LANES, SC_SLAB, SC_GATHER_ROWS

    rblk = 8

    def chunk(c, p, h_v, rows, part_v):
        @pl.loop(0, ch // rblk)
        def _(rb):
            r0 = rb * rblk
            acc = [jnp.zeros((lanes,), F32) for _ in range(rblk)]
            for sb in range(w // lanes // slab):
                col = sb * slab * lanes
                h_lo = [h_v[p, pl.ds(col + j * lanes, lanes)] for j in range(slab)]
                h_hi = [h_v[p, pl.ds(w + col + j * lanes, lanes)] for j in range(slab)]
                for r in range(rblk):
                    for j in range(slab):
                        lo, hi = _sc_unpack(rows[r0 + r, pl.ds(col + j * lanes, lanes)])
                        acc[r] += h_lo[j] * lo + h_hi[j] * hi
            for r in range(rblk):
                part_v[p, pl.ds(pl.multiple_of((c * ch + r0 + r) * lanes, lanes), lanes)] = acc[r]

    return _sc_per_token(tab, idx, hq, PEER_SEL * lanes, chunk)


def _sc_weighted_rows(tab, idx, w_rep):
    w = tab.shape[1]
    lanes, slab, ch = SC_LANES, SC_SLAB, SC_GATHER_ROWS

    def chunk(c, p, w_v, rows, acc_v):
        @pl.loop(0, w // lanes // slab)
        def _(sb):
            col = pl.multiple_of(sb * (slab * lanes), slab * lanes)
            if c == 0:
                acc = [jnp.zeros((lanes,), F32) for _ in range(2 * slab)]
            else:
                acc = [acc_v[p, pl.ds(part * w + col + j * lanes, lanes)] for j in range(slab) for part in range(2)]
            for r in range(ch):
                wv = w_v[p, pl.ds((c * ch + r) * lanes, lanes)]
                for j in range(slab):
                    lo, hi = _sc_unpack(rows[r, pl.ds(col + j * lanes, lanes)])
                    acc[2 * j] += wv * lo
                    acc[2 * j + 1] += wv * hi
            for j in range(slab):
                acc_v[p, pl.ds(col + j * lanes, lanes)] = acc[2 * j]
                acc_v[p, pl.ds(w + col + j * lanes, lanes)] = acc[2 * j + 1]

    return _sc_per_token(tab, idx, w_rep, 2 * w, chunk)


def _split3(x):
    hi = x.astype(BF16)
    r1 = x - hi.astype(F32)
    mid = r1.astype(BF16)
    lo = (r1 - mid.astype(F32)).astype(BF16)
    return jnp.concatenate([hi, mid, lo], axis=1)


def _peer_weights_kernel(part_ref, g_ref, fold_ref, rep_ref, w_ref):
    score = _bdot(_split3(part_ref[...]), fold_ref[...])
    wgt = g_ref[...] * jax.nn.gelu(score)
    w_ref[...] = _bdot(_split3(wgt), rep_ref[...])


def _peer_weights(part, g):
    ntok, wide = part.shape
    tm = min(256, ntok)
    rep1 = jnp.repeat(jnp.eye(PEER_SEL, dtype=BF16), SC_LANES, axis=1)
    rep = jnp.tile(rep1, (3, 1))
    fold = jnp.tile(rep1.T, (3, 1))
    row = lambda w: pl.BlockSpec((tm, w), lambda i: (i, 0))
    full = lambda a: pl.BlockSpec(a.shape, lambda i: (0, 0))
    return pl.pallas_call(
        _peer_weights_kernel,
        grid=(ntok // tm,),
        in_specs=[row(wide), row(PEER_SEL), full(fold), full(rep)],
        out_specs=row(wide),
        out_shape=jax.ShapeDtypeStruct((ntok, wide), F32),
        compiler_params=_cparams("parallel"),
        name="peer_weights",
    )(part, g, fold, rep)


def _resid_kernel(x_ref, y_ref, gate_ref, o_ref):
    o_ref[...] = x_ref[...] + gate_ref[0] * y_ref[...]


def _resid(x, y, gate, tok0, seq):
    t, d = x.shape
    ntok = y.shape[0]
    tm = min(512, ntok, seq)
    t0 = tok0 // tm
    per_seq = seq // tm
    tok = pl.BlockSpec((tm, d), lambda i: (i + t0, 0))
    return pl.pallas_call(
        _resid_kernel,
        grid=(ntok // tm,),
        in_specs=[tok, pl.BlockSpec((tm, d), lambda i: (i, 0)),
                  pl.BlockSpec((1, 1, d), lambda i: ((i + t0) // per_seq, 0, 0))],
        out_specs=tok,
        out_shape=jax.ShapeDtypeStruct((t, d), F32),
        input_output_aliases={0: 0},
        compiler_params=_cparams("parallel"),
        name="peer_resid",
    )(x, y, gate)


def _peer_groups(t):
    return max(1, min(4, t // 2048))


def _peer_stage(x, peer):
    nw, shift, scale, _, w_q, keys, tab_u, _ = peer
    b, s, d = x.shape
    hq, e_t, g_t = _peer_idx(x.reshape(b * s, d), nw, shift, scale, w_q, keys, 0, b * s, s)
    idx = e_t.transpose(0, 2, 1).reshape(-1)
    g = g_t.transpose(0, 2, 1).reshape(b * s, PEER_SEL)
    return idx, g, _sc_row_dots(tab_u, idx, hq)


def _peer_reduce(peer, staged):
    idx, g, part = staged
    return _sc_weighted_rows(peer[7], idx, _peer_weights(part, g))


def _peer_resid(x, peer, y):
    b, s, d = x.shape
    return _resid(x.reshape(b * s, d), y, peer[3], 0, s).reshape(b, s, d)


def _rms_kernel(x_ref, w_ref, o_ref):
    x = x_ref[...]
    o_ref[...] = x * lax.rsqrt(jnp.mean(x * x, axis=-1, keepdims=True) + RMS_EPS) * w_ref[...]


def _final_norm(x, w):
    b, s, d = x.shape
    t = b * s
    tm = min(512, t)
    out = pl.pallas_call(
        _rms_kernel,
        grid=(t // tm,),
        in_specs=[pl.BlockSpec((tm, d), lambda i: (i, 0)), pl.BlockSpec((1, d), lambda i: (0, 0))],
        out_specs=pl.BlockSpec((tm, d), lambda i: (i, 0)),
        out_shape=jax.ShapeDtypeStruct((t, d), F32),
        compiler_params=_cparams("parallel"),
        name="final_norm",
    )(x.reshape(t, d), w.reshape(1, d))
    return out.reshape(b, s, d)


def _rope_tables(s):
    t = jnp.arange(s)
    row = (t // GRID_W).astype(F32)
    col = (t % GRID_W).astype(F32)
    half = HEAD_DIM // 2
    inv = ROPE_THETA ** (-jnp.arange(0, half, 2, dtype=F32) / half)
    ar = row[:, None] * inv
    ac = col[:, None] * inv
    ang = jnp.concatenate([ar, ar, ac, ac], axis=1)
    q = half // 2
    sign = jnp.concatenate([-jnp.ones(q), jnp.ones(q), -jnp.ones(q), jnp.ones(q)]).astype(F32)
    reps = QK_WIDTH // HEAD_DIM
    return jnp.tile(jnp.cos(ang), (1, reps)), jnp.tile(jnp.sin(ang) * sign, (1, reps))


def _rope_partner_cols():
    j = jnp.arange(QK_WIDTH)
    dd = j % HEAD_DIM
    q = HEAD_DIM // 4
    return j - dd + jnp.where((dd % (2 * q)) < q, dd + q, dd - q)


def _dft_tables(n, dtype):
    j = jnp.arange(n, dtype=I32)
    ang = ((j[:, None] * j[None, :]) % n).astype(F32) * (2.0 * math.pi / n)
    return jnp.cos(ang).astype(dtype), jnp.sin(ang).astype(dtype)


def _pack_table(tab):
    half = tab.shape[1] // 2
    bits = lax.bitcast_convert_type(tab.astype(BF16), jnp.uint16).astype(U32)
    return bits[:, :half] | (bits[:, half:] << 16)


def kernel(x, c, ctx, c_ctx, ada_w, ada_b, norm_mix_w, norm_ffn_w, ab_w_in, ab_w_out, attn_sink, gs_w_in, gs_norm_w,
           gs_w_s, gs_b_s, gs_w_out, peer_w_q, peer_keys, peer_u, peer_v, norm_out_w):
    bsz, seq, d = x.shape
    n_ctx = ctx.shape[1]
    depth = ada_w.shape[0]

    cond = jnp.concatenate([c, c_ctx[None], jnp.zeros((16 - bsz - 1, d), F32)], axis=0)
    mods = _ada_all(cond, ada_w, ada_b)

    cos, sin = _rope_tables(seq)
    partner = _rope_partner_cols()
    dft_cc, dft_cs = _dft_tables(FNET_GROUP_DIM, BF16)
    dft_c = jnp.concatenate([dft_cc, dft_cs], axis=1)
    pos_l = _dft_tables(seq, BF16)
    pos_c = _dft_tables(n_ctx, BF16)

    n_groups = _peer_groups(bsz * seq)
    assert bsz % n_groups == 0
    nb = bsz // n_groups
    xs = [x[g * nb:(g + 1) * nb] for g in range(n_groups)]
    cs = [ctx[g * nb:(g + 1) * nb] for g in range(n_groups)]
    last_attn = ((depth - 1) // 2) * 2

    layers = []
    for i in range(depth):
        j = i // 2
        p = dict(is_ab=i % 2 == 0, upd_ctx=i < last_attn, nw_m=norm_mix_w[i].reshape(1, d),
                 nw_f=norm_ffn_w[i].reshape(1, d), w_q=peer_w_q[i].astype(BF16),
                 keys=peer_keys[i].reshape(PEER_HEADS * 2, PEER_NKEYS, PEER_HALF_DIM).astype(BF16),
                 tabs=(_pack_table(peer_u[i]), _pack_table(peer_v[i])))
        if p["is_ab"]:
            w_in = ab_w_in[j]
            p.update(w_ext=jnp.concatenate([w_in, w_in[:, partner]], axis=1).astype(BF16),
                     w_out=ab_w_out[j].astype(BF16), sink=attn_sink[j].reshape(1, N_Q_HEADS))
        else:
            p.update(sgu_w=(gs_w_in[j].astype(BF16), gs_norm_w[j].reshape(1, -1), gs_w_s[j].astype(BF16),
                            gs_b_s[j].T, gs_w_out[j].astype(BF16)))
        layers.append(p)

    def group_mods(i, g):
        m_l = [m[:, None, :] for m in jnp.split(mods[i, g * nb:(g + 1) * nb], N_MOD, axis=-1)]
        m_c = [jnp.broadcast_to(m[None], (nb, 1, d)) for m in jnp.split(mods[i, bsz:bsz + 1], N_MOD, axis=-1)]
        return m_l, m_c

    def peer_params(i, m):
        p = layers[i]
        return (p["nw_f"], m[3], m[4], m[5], p["w_q"], p["keys"]) + p["tabs"]

    def mix_and_stage(i, g):
        p = layers[i]
        m_l, m_c = group_mods(i, g)
        xg, cg = xs[g], cs[g]
        if p["is_ab"]:
            q_c, k_c, v_c, fa_c, fb_c = _ab_in(cg, p["nw_m"], m_c[0], m_c[1], p["w_ext"], cos[:n_ctx], sin[:n_ctx],
                                               dft_c, False, 0, nb)
            q_l, k_l, v_l, fa_l, fb_l = _ab_in(xg, p["nw_m"], m_l[0], m_l[1], p["w_ext"], cos, sin, dft_c, True, 0, nb)
            a_l = _attn_local(q_l, k_l, v_l, k_c, v_c, p["sink"], 0)
            xg = _ab_out(a_l, _dft_pos(fa_l, fb_l, *pos_l), p["w_out"], xg, m_l[2], 0)
            if p["upd_ctx"]:
                a_c = _attn_ctx(q_c, k_c, v_c, p["sink"])
                cg = _ab_out(a_c, _dft_pos(fa_c, fb_c, *pos_c), p["w_out"], cg, m_c[2], 0)
        else:
            xg = _sgu(xg, p["nw_m"], m_l[0], m_l[1], m_l[2], *p["sgu_w"], 0, nb)
            if p["upd_ctx"]:
                cg = _sgu(cg, p["nw_m"], m_c[0], m_c[1], m_c[2], *p["sgu_w"], 0, nb)
        xs[g], cs[g] = xg, cg
        st_l = _peer_stage(xg, peer_params(i, m_l))
        st_c = _peer_stage(cg, peer_params(i, m_c)) if p["upd_ctx"] else None
        return st_l, st_c

    def reduce_rows(i, g, staged):
        m_l, m_c = group_mods(i, g)
        y_l = _peer_reduce(peer_params(i, m_l), staged[0])
        y_c = _peer_reduce(peer_params(i, m_c), staged[1]) if staged[1] is not None else None
        return y_l, y_c

    def add_residual(i, g, ys):
        m_l, m_c = group_mods(i, g)
        xs[g] = _peer_resid(xs[g], peer_params(i, m_l), ys[0])
        if ys[1] is not None:
            cs[g] = _peer_resid(cs[g], peer_params(i, m_c), ys[1])

    staged = [mix_and_stage(0, g) for g in range(n_groups)]
    for i in range(depth):
        ys = [reduce_rows(i, g, staged[g]) for g in range(n_groups)]
        for g in range(n_groups):
            add_residual(i, g, ys[g])
            if i + 1 < depth:
                staged[g] = mix_and_stage(i + 1, g)
    return jnp.concatenate([_final_norm(xg, norm_out_w) for xg in xs], axis=0)
```

```python
import functools
import math

import jax
import jax.numpy as jnp
from jax import lax
from jax.experimental import pallas as pl
from jax.experimental.pallas import tpu as pltpu
from jax.experimental.pallas import tpu_sc as plsc

F32 = jnp.float32
BF16 = jnp.bfloat16
I32 = jnp.int32
U32 = jnp.uint32

GRID_W = 64
N_Q_HEADS = 8
N_KV_HEADS = 2
Q_PER_KV = N_Q_HEADS // N_KV_HEADS
HEAD_DIM = 64
WINDOW = 128
ROPE_THETA = 10000.0
ATTN_WIDTH = N_Q_HEADS * HEAD_DIM
KV_WIDTH = N_KV_HEADS * HEAD_DIM
QK_WIDTH = ATTN_WIDTH + KV_WIDTH
FNET_GROUPS = 4
FNET_GROUP_DIM = 128
FNET_WIDTH = FNET_GROUPS * FNET_GROUP_DIM
AB_IN_WIDTH = ATTN_WIDTH + 2 * KV_WIDTH + FNET_WIDTH
SGU_GROUPS = 8
SGU_GROUP_DIM = 128
SGU_WIDTH = SGU_GROUPS * SGU_GROUP_DIM
SGU_CHUNK = 128
PEER_HEADS = 8
PEER_NKEYS = 128
PEER_TOPK = 16
PEER_HALF_DIM = 128
PEER_SEL = PEER_HEADS * PEER_TOPK
N_MOD = 6
RMS_EPS = 1e-6
NEG_INF = -1e30

SC_CORES = 2
SC_SUBCORES = 16
SC_WORKERS = SC_CORES * SC_SUBCORES
SC_LANES = 16
SC_GATHER_ROWS = 64
VMEM_LIMIT = 48 * 1024 * 1024


def _cparams(*sem):
    return pltpu.CompilerParams(dimension_semantics=sem, vmem_limit_bytes=VMEM_LIMIT)


def _norm_mod(x, nw, shift, scale):
    y = x * lax.rsqrt(jnp.mean(x * x, axis=-1, keepdims=True) + RMS_EPS) * nw
    return y * (1.0 + scale) + shift


def _bdot(a, b):
    return jnp.dot(a, b, preferred_element_type=F32)


def _ada_kernel(c_ref, w_ref, b_ref, o_ref):
    c = c_ref[...]
    s = (c * jax.nn.sigmoid(c)).astype(BF16)
    o_ref[0] = _bdot(s, w_ref[0].astype(BF16)) + b_ref[0]


def _ada_all(cond, ada_w, ada_b):
    depth, d, n = ada_w.shape
    r = cond.shape[0]
    tn = 1536
    return pl.pallas_call(
        _ada_kernel,
        grid=(depth, n // tn),
        in_specs=[pl.BlockSpec((r, d), lambda l, j: (0, 0)),
                  pl.BlockSpec((1, d, tn), lambda l, j: (l, 0, j)),
                  pl.BlockSpec((1, 1, tn), lambda l, j: (l, 0, j))],
        out_specs=pl.BlockSpec((1, r, tn), lambda l, j: (l, 0, j)),
        out_shape=jax.ShapeDtypeStruct((depth, r, n), F32),
        compiler_params=_cparams("parallel", "parallel"),
        name="ada",
    )(cond, ada_w, ada_b.reshape(depth, 1, n))


def _ab_in_kernel(x_ref, nw_ref, sh_ref, sc_ref, w_ref, cos_ref, sin_ref, dft_ref,
                  q_ref, k_ref, v_ref, fa_ref, fb_ref, *, rope):
    h = _norm_mod(x_ref[0], nw_ref[...], sh_ref[0], sc_ref[0]).astype(BF16)
    p = _bdot(h, w_ref[...])
    qk = p[:, :QK_WIDTH]
    if rope:
        qk = qk * cos_ref[...] + p[:, AB_IN_WIDTH:] * sin_ref[...]
    q_ref[0] = qk[:, :ATTN_WIDTH].astype(BF16)
    k_ref[0] = qk[:, ATTN_WIDTH:].astype(BF16)
    v_ref[0] = p[:, QK_WIDTH:QK_WIDTH + KV_WIDTH].astype(BF16)
    f0 = QK_WIDTH + KV_WIDTH
    for g in range(FNET_GROUPS):
        fg = p[:, f0 + g * FNET_GROUP_DIM:f0 + (g + 1) * FNET_GROUP_DIM].astype(BF16)
        ab = _bdot(fg, dft_ref[...])
        fa_ref[0, :, g * FNET_GROUP_DIM:(g + 1) * FNET_GROUP_DIM] = ab[:, :FNET_GROUP_DIM].astype(BF16)
        fb_ref[0, :, g * FNET_GROUP_DIM:(g + 1) * FNET_GROUP_DIM] = ab[:, FNET_GROUP_DIM:].astype(BF16)


def _ab_in(x, nw, shift, scale, w_ext, cos, sin, dft_c, rope, b0, nb):
    _, s, d = x.shape
    tm = min(512, s)
    n_ext = w_ext.shape[1]
    outs = [jax.ShapeDtypeStruct((nb, s, ATTN_WIDTH), BF16), jax.ShapeDtypeStruct((nb, s, KV_WIDTH), BF16),
            jax.ShapeDtypeStruct((nb, s, KV_WIDTH), BF16), jax.ShapeDtypeStruct((nb, s, FNET_WIDTH), BF16),
            jax.ShapeDtypeStruct((nb, s, FNET_WIDTH), BF16)]
    tok = lambda w: pl.BlockSpec((1, tm, w), lambda i, j: (i, j, 0))
    return pl.pallas_call(
        functools.partial(_ab_in_kernel, rope=rope),
        grid=(nb, s // tm),
        in_specs=[pl.BlockSpec((1, tm, d), lambda i, j: (i + b0, j, 0)),
                  pl.BlockSpec((1, d), lambda i, j: (0, 0)),
                  pl.BlockSpec((1, 1, d), lambda i, j: (i + b0, 0, 0)),
                  pl.BlockSpec((1, 1, d), lambda i, j: (i + b0, 0, 0)),
                  pl.BlockSpec((d, n_ext), lambda i, j: (0, 0)),
                  pl.BlockSpec((tm, QK_WIDTH), lambda i, j: (j, 0)),
                  pl.BlockSpec((tm, QK_WIDTH), lambda i, j: (j, 0)),
                  pl.BlockSpec(dft_c.shape, lambda i, j: (0, 0))],
        out_specs=[tok(ATTN_WIDTH), tok(KV_WIDTH), tok(KV_WIDTH), tok(FNET_WIDTH), tok(FNET_WIDTH)],
        out_shape=outs,
        compiler_params=_cparams("parallel", "parallel"),
        name="ab_in",
    )(x, nw, shift, scale, w_ext, cos, sin, dft_c)


def _softmax_pv(qh, kk, vv, valid, sk, scale):
    s = lax.dot_general(qh, kk, (((1,), (1,)), ((), ())), preferred_element_type=F32) * scale
    if valid is not None:
        s = jnp.where(valid, s, NEG_INF)
    m = jnp.maximum(jnp.max(s, axis=-1, keepdims=True), sk)
    p = jnp.exp(s - m)
    inv_den = 1.0 / (jnp.sum(p, axis=-1, keepdims=True) + jnp.exp(sk - m))
    return _bdot((p * inv_den).astype(BF16), vv)


def _gqa_group(q, k_all, v_all, valid, sink_ref, kvh, o_ref):
    n = q.shape[0]
    heads = range(kvh * Q_PER_KV, (kvh + 1) * Q_PER_KV)
    kk = k_all[:, kvh * HEAD_DIM:(kvh + 1) * HEAD_DIM]
    vv = v_all[:, kvh * HEAD_DIM:(kvh + 1) * HEAD_DIM]
    q4 = jnp.concatenate([q[:, h * HEAD_DIM:(h + 1) * HEAD_DIM] for h in heads], axis=0)
    sk4 = jnp.concatenate([jnp.broadcast_to(sink_ref[0:1, h:h + 1], (n, 1)) for h in heads], axis=0)
    valid4 = None if valid is None else jnp.concatenate([valid] * Q_PER_KV, axis=0)
    o4 = _softmax_pv(q4, kk, vv, valid4, sk4, HEAD_DIM ** -0.5)
    for g, h in enumerate(heads):
        o_ref[0, :, h * HEAD_DIM:(h + 1) * HEAD_DIM] = o4[g * n:(g + 1) * n].astype(BF16)


def _attn_local_kernel(q_ref, kp_ref, ko_ref, kn_ref, vp_ref, vo_ref, vn_ref, kc_ref, vc_ref, sink_ref, o_ref,
                       *, seq):
    j = pl.program_id(1)
    blk = q_ref.shape[1]
    n_ctx = kc_ref.shape[1]
    q = q_ref[0]
    kl = jnp.concatenate([kp_ref[0], ko_ref[0], kn_ref[0], kc_ref[0]], axis=0)
    vl = jnp.concatenate([vp_ref[0], vo_ref[0], vn_ref[0], vc_ref[0]], axis=0)
    nk = 3 * blk + n_ctx
    qi = lax.broadcasted_iota(I32, (blk, nk), 0)
    cj = lax.broadcasted_iota(I32, (blk, nk), 1)
    kj = cj - blk
    kpos = j * blk + kj
    valid = (cj >= 3 * blk) | ((jnp.abs(qi - kj) <= WINDOW) & (kpos >= 0) & (kpos < seq))
    for kvh in range(N_KV_HEADS):
        _gqa_group(q, kl, vl, valid, sink_ref, kvh, o_ref)


def _attn_local(q, k, v, kc, vc, sink, b0):
    b, s, _ = q.shape
    c = kc.shape[1]
    blk = WINDOW
    nb = s // blk
    qspec = pl.BlockSpec((1, blk, ATTN_WIDTH), lambda i, j: (i, j, 0))
    prev = pl.BlockSpec((1, blk, KV_WIDTH), lambda i, j: (i, jnp.maximum(j - 1, 0), 0))
    own = pl.BlockSpec((1, blk, KV_WIDTH), lambda i, j: (i, j, 0))
    nxt = pl.BlockSpec((1, blk, KV_WIDTH), lambda i, j: (i, jnp.minimum(j + 1, nb - 1), 0))
    cspec = pl.BlockSpec((1, c, KV_WIDTH), lambda i, j: (i + b0, 0, 0))
    return pl.pallas_call(
        functools.partial(_attn_local_kernel, seq=s),
        grid=(b, nb),
        in_specs=[qspec, prev, own, nxt, prev, own, nxt, cspec, cspec,
                  pl.BlockSpec((1, N_Q_HEADS), lambda i, j: (0, 0))],
        out_specs=qspec,
        out_shape=jax.ShapeDtypeStruct((b, s, ATTN_WIDTH), BF16),
        compiler_params=_cparams("parallel", "parallel"),
        name="attn_local",
    )(q, k, k, k, v, v, v, kc, vc, sink)


def _attn_ctx_kernel(q_ref, k_ref, v_ref, sink_ref, o_ref):
    for kvh in range(N_KV_HEADS):
        _gqa_group(q_ref[0], k_ref[0], v_ref[0], None, sink_ref, kvh, o_ref)


def _attn_ctx(q, k, v, sink):
    b, c, _ = q.shape
    return pl.pallas_call(
        _attn_ctx_kernel,
        grid=(b,),
        in_specs=[pl.BlockSpec((1, c, ATTN_WIDTH), lambda i: (i, 0, 0)),
                  pl.BlockSpec((1, c, KV_WIDTH), lambda i: (i, 0, 0)),
                  pl.BlockSpec((1, c, KV_WIDTH), lambda i: (i, 0, 0)),
                  pl.BlockSpec((1, N_Q_HEADS), lambda i: (0, 0))],
        out_specs=pl.BlockSpec((1, c, ATTN_WIDTH), lambda i: (i, 0, 0)),
        out_shape=jax.ShapeDtypeStruct((b, c, ATTN_WIDTH), BF16),
        compiler_params=_cparams("parallel"),
        name="attn_ctx",
    )(q, k, v, sink)


def _dft_pos_kernel(cs_ref, ss_ref, a_ref, b_ref, o_ref, acc_ref, *, scale):
    kk = pl.program_id(2)

    @pl.when(kk == 0)
    def _():
        acc_ref[...] = jnp.zeros_like(acc_ref)

    acc_ref[...] += _bdot(cs_ref[...], a_ref[0]) - _bdot(ss_ref[...], b_ref[0])

    @pl.when(kk == pl.num_programs(2) - 1)
    def _():
        o_ref[0] = (acc_ref[...] * scale).astype(BF16)


def _dft_pos(fa, fb, cs, ss):
    b, s, w = fa.shape
    t = min(1024, s)
    scale = 1.0 / math.sqrt(s * FNET_GROUP_DIM)
    return pl.pallas_call(
        functools.partial(_dft_pos_kernel, scale=scale),
        grid=(b, s // t, s // t),
        in_specs=[pl.BlockSpec((t, t), lambda i, m, k: (m, k)),
                  pl.BlockSpec((t, t), lambda i, m, k: (m, k)),
                  pl.BlockSpec((1, t, w), lambda i, m, k: (i, k, 0)),
                  pl.BlockSpec((1, t, w), lambda i, m, k: (i, k, 0))],
        out_specs=pl.BlockSpec((1, t, w), lambda i, m, k: (i, m, 0)),
        out_shape=jax.ShapeDtypeStruct((b, s, w), BF16),
        scratch_shapes=[pltpu.VMEM((t, w), F32)],
        compiler_params=_cparams("parallel", "parallel", "arbitrary"),
        name="dft_pos",
    )(cs, ss, fa, fb)


def _ab_out_kernel(a_ref, f_ref, w1_ref, w2_ref, x_ref, g_ref, o_ref):
    y = _bdot(a_ref[0], w1_ref[...]) + _bdot(f_ref[0], w2_ref[...])
    o_ref[0] = x_ref[0] + g_ref[0] * y


def _ab_out(a, fm, w_out, x, gate, b0):
    nb = a.shape[0]
    _, s, d = x.shape
    tm = min(512, s)
    tok = lambda w: pl.BlockSpec((1, tm, w), lambda i, j: (i, j, 0))
    xtok = pl.BlockSpec((1, tm, d), lambda i, j: (i + b0, j, 0))
    return pl.pallas_call(
        _ab_out_kernel,
        grid=(nb, s // tm),
        in_specs=[tok(ATTN_WIDTH), tok(FNET_WIDTH),
                  pl.BlockSpec((ATTN_WIDTH, d), lambda i, j: (0, 0)),
                  pl.BlockSpec((FNET_WIDTH, d), lambda i, j: (0, 0)),
                  xtok,
                  pl.BlockSpec((1, 1, d), lambda i, j: (i + b0, 0, 0))],
        out_specs=xtok,
        out_shape=jax.ShapeDtypeStruct(x.shape, F32),
        input_output_aliases={4: 0},
        compiler_params=_cparams("parallel", "parallel"),
        name="ab_out",
    )(a, fm, w_out[:ATTN_WIDTH], w_out[ATTN_WIDTH:], x, gate)


def _sgu_kernel(x_ref, nw_ref, sh_ref, sc_ref, g_ref, win_ref, gnw_ref, ws_ref, bs_ref, wout_ref, o_ref, uv_ref):
    x = x_ref[0]
    tm = x.shape[0]
    h = _norm_mod(x, nw_ref[...], sh_ref[0], sc_ref[0]).astype(BF16)
    z = jax.nn.gelu(_bdot(h, win_ref[...]))
    u = z[:, :SGU_WIDTH]
    v = z[:, SGU_WIDTH:]
    v = (v * lax.rsqrt(jnp.mean(v * v, axis=-1, keepdims=True) + RMS_EPS) * gnw_ref[...]).astype(BF16)
    for n in range(tm // SGU_CHUNK):
        r = slice(n * SGU_CHUNK, (n + 1) * SGU_CHUNK)
        for g in range(SGU_GROUPS):
            cs = slice(g * SGU_GROUP_DIM, (g + 1) * SGU_GROUP_DIM)
            sv = _bdot(ws_ref[g], v[r, cs]) + bs_ref[:, g:g + 1]
            uv_ref[r, cs] = (u[r, cs] * sv).astype(BF16)
    o_ref[0] = x + g_ref[0] * _bdot(uv_ref[...], wout_ref[...])


def _sgu(x, nw, shift, scale, gate, w_in, gnw, w_s, b_s_t, w_out, b0, nb):
    _, s, d = x.shape
    tm = min(256, s)
    tok = pl.BlockSpec((1, tm, d), lambda i, j: (i + b0, j, 0))
    mod = pl.BlockSpec((1, 1, d), lambda i, j: (i + b0, 0, 0))
    full = lambda a: pl.BlockSpec(a.shape, lambda i, j: (0,) * a.ndim)
    return pl.pallas_call(
        _sgu_kernel,
        grid=(nb, s // tm),
        in_specs=[tok, full(nw), mod, mod, mod, full(w_in), full(gnw), full(w_s), full(b_s_t), full(w_out)],
        out_specs=tok,
        out_shape=jax.ShapeDtypeStruct(x.shape, F32),
        input_output_aliases={0: 0},
        scratch_shapes=[pltpu.VMEM((tm, SGU_WIDTH), BF16)],
        compiler_params=_cparams("parallel", "parallel"),
        name="sgu",
    )(x, nw, shift, scale, gate, w_in, gnw, w_s, b_s_t, w_out)


def _topk_rows(s, rid, k):
    big = jnp.iinfo(jnp.int32).max
    vals, idxs = [], []
    for _ in range(k):
        m = jnp.max(s, axis=0, keepdims=True)
        am = jnp.min(jnp.where(s == m, rid, big), axis=0, keepdims=True)
        vals.append(m)
        idxs.append(am)
        s = jnp.where(rid == am, -jnp.inf, s)
    return jnp.concatenate(vals, axis=0), jnp.concatenate(idxs, axis=0)


_PAIR_PIECES = ((0, 1, 0, 16), (1, 1, 0, 8), (2, 1, 0, 8), (3, 1, 0, 8),
                (8, 8, 0, 1), (0, 8, 0, 1), (0, 8, 1, 1), (0, 8, 2, 1))
_PAIR_ROW_PIECES = 4


def _pair_candidates(t0, t1):
    k = PEER_TOPK
    n = t0.shape[1]
    vals, ids = [], []
    for a0, na, b0, nb in _PAIR_PIECES:
        io = lax.broadcasted_iota(I32, (max(na, nb), n), 0)
        a = a0 + io if na > 1 else jnp.full_like(io, a0)
        b = b0 + io if nb > 1 else jnp.full_like(io, b0)
        ok = (a + 1) * (b + 1) <= k
        if na > 1:
            ok = ok & (a >= _PAIR_ROW_PIECES)
        vals.append(jnp.where(ok, t0[a0:a0 + na] + t1[b0:b0 + nb], -jnp.inf))
        ids.append(a * k + b)
    return jnp.concatenate(vals, axis=0), jnp.concatenate(ids, axis=0)


def _select_rows(sel, table):
    out = jnp.zeros(sel.shape, table.dtype)
    for a in range(table.shape[0]):
        out = jnp.where(sel == a, table[a:a + 1], out)
    return out


def _pair_words(lo, hi):
    lo_bits = pltpu.bitcast(lo.astype(F32), U32) >> 16
    hi_bits = pltpu.bitcast(hi.astype(F32), U32) & jnp.uint32(0xFFFF0000)
    return lo_bits | hi_bits


def _peer_idx_kernel(x_ref, nw_ref, sh_ref, sc_ref, wq_ref, keys_ref, hq_ref, e_ref, g_ref, q_ref):
    hq = _norm_mod(x_ref[...], nw_ref[...], sh_ref[0], sc_ref[0])
    hb = hq.astype(BF16)
    half = hq.shape[1] // 2
    hq_ref[...] = _pair_words(hb[:, :half], hb[:, half:])
    q_ref[...] = _bdot(hb, wq_ref[...]).astype(BF16)

    def head(h, carry):
        tops, topi = [], []
        for s in range(2):
            col = pl.multiple_of((h * 2 + s) * PEER_HALF_DIM, PEER_HALF_DIM)
            qs = q_ref[:, pl.ds(col, PEER_HALF_DIM)]
            st = lax.dot_general(keys_ref[h * 2 + s], qs, (((1,), (1,)), ((), ())), preferred_element_type=F32)
            ts, ti = _topk_rows(st, lax.broadcasted_iota(I32, st.shape, 0), PEER_TOPK)
            tops.append(ts)
            topi.append(ti)
        best_s, best = _topk_rows(*_pair_candidates(tops[0], tops[1]), PEER_TOPK)
        i1 = _select_rows(best // PEER_TOPK, topi[0])
        i2 = _select_rows(best % PEER_TOPK, topi[1])
        p = jnp.exp(best_s - best_s[0:1])
        row = pl.multiple_of(h * PEER_TOPK, PEER_TOPK)
        e_ref[0, pl.ds(row, PEER_TOPK), :] = i1 * PEER_NKEYS + i2
        g_ref[0, pl.ds(row, PEER_TOPK), :] = p / jnp.sum(p, axis=0, keepdims=True)
        return carry

    lax.fori_loop(0, PEER_HEADS, head, 0)


def _peer_idx(x, nw, shift, scale, w_q, keys, tok0, ntok, seq):
    d = x.shape[1]
    tm = min(256, ntok, seq)
    nq = w_q.shape[1]
    t0 = tok0 // tm
    per_seq = seq // tm
    mod = pl.BlockSpec((1, 1, d), lambda i: ((i + t0) // per_seq, 0, 0))
    sel = pl.BlockSpec((1, PEER_SEL, tm), lambda i: (i, 0, 0))
    return pl.pallas_call(
        _peer_idx_kernel,
        grid=(ntok // tm,),
        in_specs=[pl.BlockSpec((tm, d), lambda i: (i + t0, 0)),
                  pl.BlockSpec((1, d), lambda i: (0, 0)),
                  mod, mod,
                  pl.BlockSpec((d, nq), lambda i: (0, 0)),
                  pl.BlockSpec(keys.shape, lambda i: (0, 0, 0))],
        out_specs=[pl.BlockSpec((tm, d // 2), lambda i: (i, 0)), sel, sel],
        out_shape=[jax.ShapeDtypeStruct((ntok, d // 2), U32),
                   jax.ShapeDtypeStruct((ntok // tm, PEER_SEL, tm), I32),
                   jax.ShapeDtypeStruct((ntok // tm, PEER_SEL, tm), F32)],
        scratch_shapes=[pltpu.VMEM((tm, nq), BF16)],
        compiler_params=_cparams("parallel"),
        name="peer_idx",
    )(x, nw, shift, scale, w_q, keys)


def _sc_worker_id():
    return lax.axis_index("s") * SC_CORES + lax.axis_index("c")


SC_SLAB = 2


def _sc_pair_products(x, y):
    prod = plsc.bitcast(x, BF16) * plsc.bitcast(y, BF16)
    return plsc.unpack(prod, format=plsc.PackFormat.INTERLEAVED, preferred_element_type=F32)


def _sc_per_token(tab, idx, aux, out_width, chunk_fn):
    k = PEER_SEL
    n_tok, a = aux.shape
    w = tab.shape[1]
    ch = SC_GATHER_ROWS
    per_w = n_tok // SC_WORKERS
    assert k == 2 * ch and per_w % 2 == 0
    mesh = plsc.VectorSubcoreMesh(core_axis_name="c", subcore_axis_name="s")

    @functools.partial(
        pl.kernel, mesh=mesh, out_type=jax.ShapeDtypeStruct((n_tok, out_width), F32),
        scratch_types=[pltpu.VMEM((per_w * k,), I32), pltpu.VMEM((2, a), aux.dtype),
                       pltpu.VMEM((ch, w), U32), pltpu.VMEM((ch, w), U32), pltpu.VMEM((2, out_width), F32),
                       pltpu.SemaphoreType.DMA, pltpu.SemaphoreType.DMA,
                       pltpu.SemaphoreType.DMA((2,)), pltpu.SemaphoreType.DMA((2,))],
        compiler_params=pltpu.CompilerParams(needs_layout_passes=False))
    def per_token(tab_hbm, idx_hbm, aux_hbm, out_hbm, idx_v, aux_v, rows0, rows1, out_v, gsem0, gsem1, asem, osem):
        base = _sc_worker_id() * per_w
        rbuf = ((rows0, gsem0), (rows1, gsem1))
        pltpu.sync_copy(idx_hbm.at[pl.ds(pl.multiple_of(base * k, k), per_w * k)], idx_v)

        def fetch_rows(t, c):
            sel = idx_v.at[pl.ds(pl.multiple_of(t * k + c * ch, ch), ch)]
            return pltpu.make_async_copy(tab_hbm.at[sel], rbuf[c][0], rbuf[c][1])

        def fetch_aux(t, p):
            return pltpu.make_async_copy(aux_hbm.at[base + t], aux_v.at[p], asem.at[p])

        def store_out(t, p):
            return pltpu.make_async_copy(out_v.at[p], out_hbm.at[base + t], osem.at[p])

        fetch_rows(0, 0).start()
        fetch_rows(0, 1).start()
        fetch_aux(0, 0).start()

        @pl.loop(0, per_w)
        def _(t):
            p = lax.rem(t, 2)
            fetch_aux(t, p).wait()

            @pl.when(t + 1 < per_w)
            def _():
                fetch_aux(t + 1, 1 - p).start()

            @pl.when(t >= 2)
            def _():
                store_out(t - 2, p).wait()

            for c in range(2):
                fetch_rows(t, c).wait()
                chunk_fn(c, p, aux_v, rbuf[c][0], out_v)

                @pl.when(t + 1 < per_w)
                def _():
                    fetch_rows(t + 1, c).start()

            store_out(t, p).start()

        store_out(per_w - 2, 0).wait()
        store_out(per_w - 1, 1).wait()

    return per_token(tab, idx, aux)


def _sc_row_dots(tab, idx, hq_pairs):
    w = tab.shape[1]
    lanes, slab, ch = SC_LANES, SC_SLAB, SC_GATHER_ROWS

    rblk = 8

    def chunk(c, p, h_v, rows, part_v):
        @pl.loop(0, ch // rblk)
        def _(rb):
            r0 = rb * rblk
            acc = [jnp.zeros((lanes,), F32) for _ in range(rblk)]
            for sb in range(w // lanes // slab):
                col = sb * slab * lanes
                h = [h_v[p, pl.ds(col + j * lanes, lanes)] for j in range(slab)]
                for r in range(rblk):
                    for j in range(slab):
                        lo, hi = _sc_pair_products(rows[r0 + r, pl.ds(col + j * lanes, lanes)], h[j])
                        acc[r] += lo + hi
            for r in range(rblk):
                part_v[p, pl.ds(pl.multiple_of((c * ch + r0 + r) * lanes, lanes), lanes)] = acc[r]

    return _sc_per_token(tab, idx, hq_pairs, PEER_SEL * lanes, chunk)


def _sc_weighted_rows(tab, idx, w_rep):
    w = tab.shape[1]
    lanes, slab, ch = SC_LANES, SC_SLAB, SC_GATHER_ROWS

    def chunk(c, p, w_v, rows, acc_v):
        @pl.loop(0, w // lanes // slab)
        def _(sb):
            col = pl.multiple_of(sb * (slab * lanes), slab * lanes)
            if c == 0:
                acc = [jnp.zeros((lanes,), F32) for _ in range(2 * slab)]
            else:
                acc = [acc_v[p, pl.ds(part * w + col + j * lanes, lanes)] for j in range(slab) for part in range(2)]
            for r in range(ch):
                wv = w_v[p, pl.ds((c * ch + r) * lanes, lanes)]
                for j in range(slab):
                    lo, hi = _sc_pair_products(rows[r, pl.ds(col + j * lanes, lanes)], wv)
                    acc[2 * j] += lo
                    acc[2 * j + 1] += hi
            for j in range(slab):
                acc_v[p, pl.ds(col + j * lanes, lanes)] = acc[2 * j]
                acc_v[p, pl.ds(w + col + j * lanes, lanes)] = acc[2 * j + 1]

    return _sc_per_token(tab, idx, w_rep, 2 * w, chunk)


def _split3(x):
    hi = x.astype(BF16)
    r1 = x - hi.astype(F32)
    mid = r1.astype(BF16)
    lo = (r1 - mid.astype(F32)).astype(BF16)
    return jnp.concatenate([hi, mid, lo], axis=1)


def _peer_weights_kernel(part_ref, g_ref, fold_ref, rep_ref, w_ref):
    score = _bdot(_split3(part_ref[...]), fold_ref[...])
    wb = (g_ref[...] * jax.nn.gelu(score)).astype(BF16)
    word = pltpu.bitcast(_pair_words(wb, wb), F32)
    w_ref[...] = _bdot(_split3(word), rep_ref[...])


def _peer_weights(part, g):
    ntok, wide = part.shape
    tm = min(256, ntok)
    rep1 = jnp.repeat(jnp.eye(PEER_SEL, dtype=BF16), SC_LANES, axis=1)
    rep = jnp.tile(rep1, (3, 1))
    fold = jnp.tile(rep1.T, (3, 1))
    row = lambda w: pl.BlockSpec((tm, w), lambda i: (i, 0))
    full = lambda a: pl.BlockSpec(a.shape, lambda i: (0, 0))
    return pl.pallas_call(
        _peer_weights_kernel,
        grid=(ntok // tm,),
        in_specs=[row(wide), row(PEER_SEL), full(fold), full(rep)],
        out_specs=row(wide),
        out_shape=jax.ShapeDtypeStruct((ntok, wide), F32),
        compiler_params=_cparams("parallel"),
        name="peer_weights",
    )(part, g, fold, rep)


def _resid_kernel(x_ref, y_ref, gate_ref, o_ref):
    o_ref[...] = x_ref[...] + gate_ref[0] * y_ref[...]


def _resid(x, y, gate, tok0, seq):
    t, d = x.shape
    ntok = y.shape[0]
    tm = min(512, ntok, seq)
    t0 = tok0 // tm
    per_seq = seq // tm
    tok = pl.BlockSpec((tm, d), lambda i: (i + t0, 0))
    return pl.pallas_call(
        _resid_kernel,
        grid=(ntok // tm,),
        in_specs=[tok, pl.BlockSpec((tm, d), lambda i: (i, 0)),
                  pl.BlockSpec((1, 1, d), lambda i: ((i + t0) // per_seq, 0, 0))],
        out_specs=tok,
        out_shape=jax.ShapeDtypeStruct((t, d), F32),
        input_output_aliases={0: 0},
        compiler_params=_cparams("parallel"),
        name="peer_resid",
    )(x, y, gate)


def _peer_groups(t):
    return max(1, min(4, t // 2048))


def _peer_stage(x, peer):
    nw, shift, scale, _, w_q, keys, tab_u, _ = peer
    b, s, d = x.shape
    hq, e_t, g_t = _peer_idx(x.reshape(b * s, d), nw, shift, scale, w_q, keys, 0, b * s, s)
    idx = e_t.transpose(0, 2, 1).reshape(-1)
    g = g_t.transpose(0, 2, 1).reshape(b * s, PEER_SEL)
    return idx, g, _sc_row_dots(tab_u, idx, hq)


def _peer_reduce(peer, staged):
    idx, g, part = staged
    return _sc_weighted_rows(peer[7], idx, _peer_weights(part, g))


def _peer_resid(x, peer, y):
    b, s, d = x.shape
    return _resid(x.reshape(b * s, d), y, peer[3], 0, s).reshape(b, s, d)


def _rms_kernel(x_ref, w_ref, o_ref):
    x = x_ref[...]
    o_ref[...] = x * lax.rsqrt(jnp.mean(x * x, axis=-1, keepdims=True) + RMS_EPS) * w_ref[...]


def _final_norm(x, w):
    b, s, d = x.shape
    t = b * s
    tm = min(512, t)
    out = pl.pallas_call(
        _rms_kernel,
        grid=(t // tm,),
        in_specs=[pl.BlockSpec((tm, d), lambda i: (i, 0)), pl.BlockSpec((1, d), lambda i: (0, 0))],
        out_specs=pl.BlockSpec((tm, d), lambda i: (i, 0)),
        out_shape=jax.ShapeDtypeStruct((t, d), F32),
        compiler_params=_cparams("parallel"),
        name="final_norm",
    )(x.reshape(t, d), w.reshape(1, d))
    return out.reshape(b, s, d)


def _rope_tables(s):
    t = jnp.arange(s)
    row = (t // GRID_W).astype(F32)
    col = (t % GRID_W).astype(F32)
    half = HEAD_DIM // 2
    inv = ROPE_THETA ** (-jnp.arange(0, half, 2, dtype=F32) / half)
    ar = row[:, None] * inv
    ac = col[:, None] * inv
    ang = jnp.concatenate([ar, ar, ac, ac], axis=1)
    q = half // 2
    sign = jnp.concatenate([-jnp.ones(q), jnp.ones(q), -jnp.ones(q), jnp.ones(q)]).astype(F32)
    reps = QK_WIDTH // HEAD_DIM
    return jnp.tile(jnp.cos(ang), (1, reps)), jnp.tile(jnp.sin(ang) * sign, (1, reps))


def _rope_partner_cols():
    j = jnp.arange(QK_WIDTH)
    dd = j % HEAD_DIM
    q = HEAD_DIM // 4
    return j - dd + jnp.where((dd % (2 * q)) < q, dd + q, dd - q)


def _dft_tables(n, dtype):
    j = jnp.arange(n, dtype=I32)
    ang = ((j[:, None] * j[None, :]) % n).astype(F32) * (2.0 * math.pi / n)
    return jnp.cos(ang).astype(dtype), jnp.sin(ang).astype(dtype)


def _pack_table(tab):
    half = tab.shape[1] // 2
    bits = lax.bitcast_convert_type(tab.astype(BF16), jnp.uint16).astype(U32)
    return bits[:, :half] | (bits[:, half:] << 16)


def kernel(x, c, ctx, c_ctx, ada_w, ada_b, norm_mix_w, norm_ffn_w, ab_w_in, ab_w_out, attn_sink, gs_w_in, gs_norm_w,
           gs_w_s, gs_b_s, gs_w_out, peer_w_q, peer_keys, peer_u, peer_v, norm_out_w):
    bsz, seq, d = x.shape
    n_ctx = ctx.shape[1]
    depth = ada_w.shape[0]

    cond = jnp.concatenate([c, c_ctx[None], jnp.zeros((16 - bsz - 1, d), F32)], axis=0)
    mods = _ada_all(cond, ada_w, ada_b)

    cos, sin = _rope_tables(seq)
    partner = _rope_partner_cols()
    dft_cc, dft_cs = _dft_tables(FNET_GROUP_DIM, BF16)
    dft_c = jnp.concatenate([dft_cc, dft_cs], axis=1)
    pos_l = _dft_tables(seq, BF16)
    pos_c = _dft_tables(n_ctx, BF16)

    n_groups = _peer_groups(bsz * seq)
    assert bsz % n_groups == 0
    nb = bsz // n_groups
    xs = [x[g * nb:(g + 1) * nb] for g in range(n_groups)]
    cs = [ctx[g * nb:(g + 1) * nb] for g in range(n_groups)]
    last_attn = ((depth - 1) // 2) * 2

    layers = []
    for i in range(depth):
        j = i // 2
        p = dict(is_ab=i % 2 == 0, upd_ctx=i < last_attn, nw_m=norm_mix_w[i].reshape(1, d),
                 nw_f=norm_ffn_w[i].reshape(1, d), w_q=peer_w_q[i].astype(BF16),
                 keys=peer_keys[i].reshape(PEER_HEADS * 2, PEER_NKEYS, PEER_HALF_DIM).astype(BF16),
                 tabs=(_pack_table(peer_u[i]), _pack_table(peer_v[i])))
        if p["is_ab"]:
            w_in = ab_w_in[j]
            p.update(w_ext=jnp.concatenate([w_in, w_in[:, partner]], axis=1).astype(BF16),
                     w_out=ab_w_out[j].astype(BF16), sink=attn_sink[j].reshape(1, N_Q_HEADS))
        else:
            p.update(sgu_w=(gs_w_in[j].astype(BF16), gs_norm_w[j].reshape(1, -1), gs_w_s[j].astype(BF16),
                            gs_b_s[j].T, gs_w_out[j].astype(BF16)))
        layers.append(p)

    def group_mods(i, g):
        m_l = [m[:, None, :] for m in jnp.split(mods[i, g * nb:(g + 1) * nb], N_MOD, axis=-1)]
        m_c = [jnp.broadcast_to(m[None], (nb, 1, d)) for m in jnp.split(mods[i, bsz:bsz + 1], N_MOD, axis=-1)]
        return m_l, m_c

    def peer_params(i, m):
        p = layers[i]
        return (p["nw_f"], m[3], m[4], m[5], p["w_q"], p["keys"]) + p["tabs"]

    def mix_and_stage(i, g):
        p = layers[i]
        m_l, m_c = group_mods(i, g)
        xg, cg = xs[g], cs[g]
        if p["is_ab"]:
            q_c, k_c, v_c, fa_c, fb_c = _ab_in(cg, p["nw_m"], m_c[0], m_c[1], p["w_ext"], cos[:n_ctx], sin[:n_ctx],
                                               dft_c, False, 0, nb)
            q_l, k_l, v_l, fa_l, fb_l = _ab_in(xg, p["nw_m"], m_l[0], m_l[1], p["w_ext"], cos, sin, dft_c, True, 0, nb)
            a_l = _attn_local(q_l, k_l, v_l, k_c, v_c, p["sink"], 0)
            xg = _ab_out(a_l, _dft_pos(fa_l, fb_l, *pos_l), p["w_out"], xg, m_l[2], 0)
            if p["upd_ctx"]:
                a_c = _attn_ctx(q_c, k_c, v_c, p["sink"])
                cg = _ab_out(a_c, _dft_pos(fa_c, fb_c, *pos_c), p["w_out"], cg, m_c[2], 0)
        else:
            xg = _sgu(xg, p["nw_m"], m_l[0], m_l[1], m_l[2], *p["sgu_w"], 0, nb)
            if p["upd_ctx"]:
                cg = _sgu(cg, p["nw_m"], m_c[0], m_c[1], m_c[2], *p["sgu_w"], 0, nb)
        xs[g], cs[g] = xg, cg
        st_l = _peer_stage(xg, peer_params(i, m_l))
        st_c = _peer_stage(cg, peer_params(i, m_c)) if p["upd_ctx"] else None
        return st_l, st_c

    def reduce_rows(i, g, staged):
        m_l, m_c = group_mods(i, g)
        y_l = _peer_reduce(peer_params(i, m_l), staged[0])
        y_c = _peer_reduce(peer_params(i, m_c), staged[1]) if staged[1] is not None else None
        return y_l, y_c

    def add_residual(i, g, ys):
        m_l, m_c = group_mods(i, g)
        xs[g] = _peer_resid(xs[g], peer_params(i, m_l), ys[0])
        if ys[1] is not None:
            cs[g] = _peer_resid(cs[g], peer_params(i, m_c), ys[1])

    staged = [mix_and_stage(0, g) for g in range(n_groups)]
    for i in range(depth):
        ys = [reduce_rows(i, g, staged[g]) for g in range(n_groups)]
        for g in range(n_groups):
            add_residual(i, g, ys[g])
            if i + 1 < depth:
                staged[g] = mix_and_stage(i + 1, g)
    return jnp.concatenate([_final_norm(xg, norm_out_w) for xg in xs], axis=0)
```

```python
import functools
import math

import jax
import jax.numpy as jnp
from jax import lax
from jax.experimental import pallas as pl
from jax.experimental.pallas import tpu as pltpu
from jax.experimental.pallas import tpu_sc as plsc

F32 = jnp.float32
BF16 = jnp.bfloat16
I32 = jnp.int32
U32 = jnp.uint32

GRID_W = 64
N_Q_HEADS = 8
N_KV_HEADS = 2
Q_PER_KV = N_Q_HEADS // N_KV_HEADS
HEAD_DIM = 64
WINDOW = 128
ROPE_THETA = 10000.0
ATTN_WIDTH = N_Q_HEADS * HEAD_DIM
KV_WIDTH = N_KV_HEADS * HEAD_DIM
QK_WIDTH = ATTN_WIDTH + KV_WIDTH
FNET_GROUPS = 4
FNET_GROUP_DIM = 128
FNET_WIDTH = FNET_GROUPS * FNET_GROUP_DIM
AB_IN_WIDTH = ATTN_WIDTH + 2 * KV_WIDTH + FNET_WIDTH
SGU_GROUPS = 8
SGU_GROUP_DIM = 128
SGU_WIDTH = SGU_GROUPS * SGU_GROUP_DIM
SGU_CHUNK = 128
PEER_HEADS = 8
PEER_NKEYS = 128
PEER_TOPK = 16
PEER_HALF_DIM = 128
PEER_SEL = PEER_HEADS * PEER_TOPK
N_MOD = 6
RMS_EPS = 1e-6
NEG_INF = -1e30

SC_CORES = 2
SC_SUBCORES = 16
SC_WORKERS = SC_CORES * SC_SUBCORES
SC_LANES = 16
SC_GATHER_ROWS = 64
VMEM_LIMIT = 48 * 1024 * 1024


def _cparams(*sem):
    return pltpu.CompilerParams(dimension_semantics=sem, vmem_limit_bytes=VMEM_LIMIT)


def _norm_mod(x, nw, shift, scale):
    y = x * lax.rsqrt(jnp.mean(x * x, axis=-1, keepdims=True) + RMS_EPS) * nw
    return y * (1.0 + scale) + shift


def _bdot(a, b):
    return jnp.dot(a, b, preferred_element_type=F32)


def _ada_kernel(c_ref, w_ref, b_ref, o_ref):
    c = c_ref[...]
    s = (c * jax.nn.sigmoid(c)).astype(BF16)
    o_ref[0] = _bdot(s, w_ref[0].astype(BF16)) + b_ref[0]


def _ada_all(cond, ada_w, ada_b):
    depth, d, n = ada_w.shape
    r = cond.shape[0]
    tn = 1536
    return pl.pallas_call(
        _ada_kernel,
        grid=(depth, n // tn),
        in_specs=[pl.BlockSpec((r, d), lambda l, j: (0, 0)),
                  pl.BlockSpec((1, d, tn), lambda l, j: (l, 0, j)),
                  pl.BlockSpec((1, 1, tn), lambda l, j: (l, 0, j))],
        out_specs=pl.BlockSpec((1, r, tn), lambda l, j: (l, 0, j)),
        out_shape=jax.ShapeDtypeStruct((depth, r, n), F32),
        compiler_params=_cparams("parallel", "parallel"),
        name="ada",
    )(cond, ada_w, ada_b.reshape(depth, 1, n))


def _ab_in_kernel(x_ref, nw_ref, sh_ref, sc_ref, w_ref, cos_ref, sin_ref, dft_ref,
                  q_ref, k_ref, v_ref, fa_ref, fb_ref, *, rope):
    h = _norm_mod(x_ref[0], nw_ref[...], sh_ref[0], sc_ref[0]).astype(BF16)
    p = _bdot(h, w_ref[...])
    qk = p[:, :QK_WIDTH]
    if rope:
        qk = qk * cos_ref[...] + p[:, AB_IN_WIDTH:] * sin_ref[...]
    q_ref[0] = qk[:, :ATTN_WIDTH].astype(BF16)
    k_ref[0] = qk[:, ATTN_WIDTH:].astype(BF16)
    v_ref[0] = p[:, QK_WIDTH:QK_WIDTH + KV_WIDTH].astype(BF16)
    f0 = QK_WIDTH + KV_WIDTH
    for g in range(FNET_GROUPS):
        fg = p[:, f0 + g * FNET_GROUP_DIM:f0 + (g + 1) * FNET_GROUP_DIM].astype(BF16)
        ab = _bdot(fg, dft_ref[...])
        fa_ref[0, :, g * FNET_GROUP_DIM:(g + 1) * FNET_GROUP_DIM] = ab[:, :FNET_GROUP_DIM].astype(BF16)
        fb_ref[0, :, g * FNET_GROUP_DIM:(g + 1) * FNET_GROUP_DIM] = ab[:, FNET_GROUP_DIM:].astype(BF16)


def _ab_in(x, nw, shift, scale, w_ext, cos, sin, dft_c, rope, b0, nb):
    _, s, d = x.shape
    tm = min(512, s)
    n_ext = w_ext.shape[1]
    outs = [jax.ShapeDtypeStruct((nb, s, ATTN_WIDTH), BF16), jax.ShapeDtypeStruct((nb, s, KV_WIDTH), BF16),
            jax.ShapeDtypeStruct((nb, s, KV_WIDTH), BF16), jax.ShapeDtypeStruct((nb, s, FNET_WIDTH), BF16),
            jax.ShapeDtypeStruct((nb, s, FNET_WIDTH), BF16)]
    tok = lambda w: pl.BlockSpec((1, tm, w), lambda i, j: (i, j, 0))
    return pl.pallas_call(
        functools.partial(_ab_in_kernel, rope=rope),
        grid=(nb, s // tm),
        in_specs=[pl.BlockSpec((1, tm, d), lambda i, j: (i + b0, j, 0)),
                  pl.BlockSpec((1, d), lambda i, j: (0, 0)),
                  pl.BlockSpec((1, 1, d), lambda i, j: (i + b0, 0, 0)),
                  pl.BlockSpec((1, 1, d), lambda i, j: (i + b0, 0, 0)),
                  pl.BlockSpec((d, n_ext), lambda i, j: (0, 0)),
                  pl.BlockSpec((tm, QK_WIDTH), lambda i, j: (j, 0)),
                  pl.BlockSpec((tm, QK_WIDTH), lambda i, j: (j, 0)),
                  pl.BlockSpec(dft_c.shape, lambda i, j: (0, 0))],
        out_specs=[tok(ATTN_WIDTH), tok(KV_WIDTH), tok(KV_WIDTH), tok(FNET_WIDTH), tok(FNET_WIDTH)],
        out_shape=outs,
        compiler_params=_cparams("parallel", "parallel"),
        name="ab_in",
    )(x, nw, shift, scale, w_ext, cos, sin, dft_c)


def _softmax_pv(qh, kk, vv, valid, sk, scale):
    s = lax.dot_general(qh, kk, (((1,), (1,)), ((), ())), preferred_element_type=F32) * scale
    if valid is not None:
        s = jnp.where(valid, s, NEG_INF)
    m = jnp.maximum(jnp.max(s, axis=-1, keepdims=True), sk)
    p = jnp.exp(s - m)
    inv_den = 1.0 / (jnp.sum(p, axis=-1, keepdims=True) + jnp.exp(sk - m))
    return _bdot((p * inv_den).astype(BF16), vv)


def _gqa_group(q, k_all, v_all, valid, sink_ref, kvh, o_ref):
    n = q.shape[0]
    heads = range(kvh * Q_PER_KV, (kvh + 1) * Q_PER_KV)
    kk = k_all[:, kvh * HEAD_DIM:(kvh + 1) * HEAD_DIM]
    vv = v_all[:, kvh * HEAD_DIM:(kvh + 1) * HEAD_DIM]
    q4 = jnp.concatenate([q[:, h * HEAD_DIM:(h + 1) * HEAD_DIM] for h in heads], axis=0)
    sk4 = jnp.concatenate([jnp.broadcast_to(sink_ref[0:1, h:h + 1], (n, 1)) for h in heads], axis=0)
    valid4 = None if valid is None else jnp.concatenate([valid] * Q_PER_KV, axis=0)
    o4 = _softmax_pv(q4, kk, vv, valid4, sk4, HEAD_DIM ** -0.5)
    for g, h in enumerate(heads):
        o_ref[0, :, h * HEAD_DIM:(h + 1) * HEAD_DIM] = o4[g * n:(g + 1) * n].astype(BF16)


def _attn_local_kernel(q_ref, kp_ref, ko_ref, kn_ref, vp_ref, vo_ref, vn_ref, kc_ref, vc_ref, sink_ref, o_ref,
                       *, seq):
    j = pl.program_id(1)
    blk = q_ref.shape[1]
    n_ctx = kc_ref.shape[1]
    q = q_ref[0]
    kl = jnp.concatenate([kp_ref[0], ko_ref[0], kn_ref[0], kc_ref[0]], axis=0)
    vl = jnp.concatenate([vp_ref[0], vo_ref[0], vn_ref[0], vc_ref[0]], axis=0)
    nk = 3 * blk + n_ctx
    qi = lax.broadcasted_iota(I32, (blk, nk), 0)
    cj = lax.broadcasted_iota(I32, (blk, nk), 1)
    kj = cj - blk
    kpos = j * blk + kj
    valid = (cj >= 3 * blk) | ((jnp.abs(qi - kj) <= WINDOW) & (kpos >= 0) & (kpos < seq))
    for kvh in range(N_KV_HEADS):
        _gqa_group(q, kl, vl, valid, sink_ref, kvh, o_ref)


def _attn_local(q, k, v, kc, vc, sink, b0):
    b, s, _ = q.shape
    c = kc.shape[1]
    blk = WINDOW
    nb = s // blk
    qspec = pl.BlockSpec((1, blk, ATTN_WIDTH), lambda i, j: (i, j, 0))
    prev = pl.BlockSpec((1, blk, KV_WIDTH), lambda i, j: (i, jnp.maximum(j - 1, 0), 0))
    own = pl.BlockSpec((1, blk, KV_WIDTH), lambda i, j: (i, j, 0))
    nxt = pl.BlockSpec((1, blk, KV_WIDTH), lambda i, j: (i, jnp.minimum(j + 1, nb - 1), 0))
    cspec = pl.BlockSpec((1, c, KV_WIDTH), lambda i, j: (i + b0, 0, 0))
    return pl.pallas_call(
        functools.partial(_attn_local_kernel, seq=s),
        grid=(b, nb),
        in_specs=[qspec, prev, own, nxt, prev, own, nxt, cspec, cspec,
                  pl.BlockSpec((1, N_Q_HEADS), lambda i, j: (0, 0))],
        out_specs=qspec,
        out_shape=jax.ShapeDtypeStruct((b, s, ATTN_WIDTH), BF16),
        compiler_params=_cparams("parallel", "parallel"),
        name="attn_local",
    )(q, k, k, k, v, v, v, kc, vc, sink)


def _attn_ctx_kernel(q_ref, k_ref, v_ref, sink_ref, o_ref):
    for kvh in range(N_KV_HEADS):
        _gqa_group(q_ref[0], k_ref[0], v_ref[0], None, sink_ref, kvh, o_ref)


def _attn_ctx(q, k, v, sink):
    b, c, _ = q.shape
    return pl.pallas_call(
        _attn_ctx_kernel,
        grid=(b,),
        in_specs=[pl.BlockSpec((1, c, ATTN_WIDTH), lambda i: (i, 0, 0)),
                  pl.BlockSpec((1, c, KV_WIDTH), lambda i: (i, 0, 0)),
                  pl.BlockSpec((1, c, KV_WIDTH), lambda i: (i, 0, 0)),
                  pl.BlockSpec((1, N_Q_HEADS), lambda i: (0, 0))],
        out_specs=pl.BlockSpec((1, c, ATTN_WIDTH), lambda i: (i, 0, 0)),
        out_shape=jax.ShapeDtypeStruct((b, c, ATTN_WIDTH), BF16),
        compiler_params=_cparams("parallel"),
        name="attn_ctx",
    )(q, k, v, sink)


def _dft_pos_kernel(cs_ref, ss_ref, a_ref, b_ref, o_ref, acc_ref, *, scale):
    kk = pl.program_id(2)

    @pl.when(kk == 0)
    def _():
        acc_ref[...] = jnp.zeros_like(acc_ref)

    acc_ref[...] += _bdot(cs_ref[...], a_ref[0]) - _bdot(ss_ref[...], b_ref[0])

    @pl.when(kk == pl.num_programs(2) - 1)
    def _():
        o_ref[0] = (acc_ref[...] * scale).astype(BF16)


def _dft_pos(fa, fb, cs, ss):
    b, s, w = fa.shape
    t = min(1024, s)
    scale = 1.0 / math.sqrt(s * FNET_GROUP_DIM)
    return pl.pallas_call(
        functools.partial(_dft_pos_kernel, scale=scale),
        grid=(b, s // t, s // t),
        in_specs=[pl.BlockSpec((t, t), lambda i, m, k: (m, k)),
                  pl.BlockSpec((t, t), lambda i, m, k: (m, k)),
                  pl.BlockSpec((1, t, w), lambda i, m, k: (i, k, 0)),
                  pl.BlockSpec((1, t, w), lambda i, m, k: (i, k, 0))],
        out_specs=pl.BlockSpec((1, t, w), lambda i, m, k: (i, m, 0)),
        out_shape=jax.ShapeDtypeStruct((b, s, w), BF16),
        scratch_shapes=[pltpu.VMEM((t, w), F32)],
        compiler_params=_cparams("parallel", "parallel", "arbitrary"),
        name="dft_pos",
    )(cs, ss, fa, fb)


def _ab_out_kernel(a_ref, f_ref, w1_ref, w2_ref, x_ref, g_ref, o_ref):
    y = _bdot(a_ref[0], w1_ref[...]) + _bdot(f_ref[0], w2_ref[...])
    o_ref[0] = x_ref[0] + g_ref[0] * y


def _ab_out(a, fm, w_out, x, gate, b0):
    nb = a.shape[0]
    _, s, d = x.shape
    tm = min(512, s)
    tok = lambda w: pl.BlockSpec((1, tm, w), lambda i, j: (i, j, 0))
    xtok = pl.BlockSpec((1, tm, d), lambda i, j: (i + b0, j, 0))
    return pl.pallas_call(
        _ab_out_kernel,
        grid=(nb, s // tm),
        in_specs=[tok(ATTN_WIDTH), tok(FNET_WIDTH),
                  pl.BlockSpec((ATTN_WIDTH, d), lambda i, j: (0, 0)),
                  pl.BlockSpec((FNET_WIDTH, d), lambda i, j: (0, 0)),
                  xtok,
                  pl.BlockSpec((1, 1, d), lambda i, j: (i + b0, 0, 0))],
        out_specs=xtok,
        out_shape=jax.ShapeDtypeStruct(x.shape, F32),
        input_output_aliases={4: 0},
        compiler_params=_cparams("parallel", "parallel"),
        name="ab_out",
    )(a, fm, w_out[:ATTN_WIDTH], w_out[ATTN_WIDTH:], x, gate)


def _sgu_kernel(x_ref, nw_ref, sh_ref, sc_ref, g_ref, win_ref, gnw_ref, ws_ref, bs_ref, wout_ref, o_ref, uv_ref):
    x = x_ref[0]
    tm = x.shape[0]
    h = _norm_mod(x, nw_ref[...], sh_ref[0], sc_ref[0]).astype(BF16)
    z = jax.nn.gelu(_bdot(h, win_ref[...]))
    u = z[:, :SGU_WIDTH]
    v = z[:, SGU_WIDTH:]
    v = (v * lax.rsqrt(jnp.mean(v * v, axis=-1, keepdims=True) + RMS_EPS) * gnw_ref[...]).astype(BF16)
    for n in range(tm // SGU_CHUNK):
        r = slice(n * SGU_CHUNK, (n + 1) * SGU_CHUNK)
        for g in range(SGU_GROUPS):
            cs = slice(g * SGU_GROUP_DIM, (g + 1) * SGU_GROUP_DIM)
            sv = _bdot(ws_ref[g], v[r, cs]) + bs_ref[:, g:g + 1]
            uv_ref[r, cs] = (u[r, cs] * sv).astype(BF16)
    o_ref[0] = x + g_ref[0] * _bdot(uv_ref[...], wout_ref[...])


def _sgu(x, nw, shift, scale, gate, w_in, gnw, w_s, b_s_t, w_out, b0, nb):
    _, s, d = x.shape
    tm = min(256, s)
    tok = pl.BlockSpec((1, tm, d), lambda i, j: (i + b0, j, 0))
    mod = pl.BlockSpec((1, 1, d), lambda i, j: (i + b0, 0, 0))
    full = lambda a: pl.BlockSpec(a.shape, lambda i, j: (0,) * a.ndim)
    return pl.pallas_call(
        _sgu_kernel,
        grid=(nb, s // tm),
        in_specs=[tok, full(nw), mod, mod, mod, full(w_in), full(gnw), full(w_s), full(b_s_t), full(w_out)],
        out_specs=tok,
        out_shape=jax.ShapeDtypeStruct(x.shape, F32),
        input_output_aliases={0: 0},
        scratch_shapes=[pltpu.VMEM((tm, SGU_WIDTH), BF16)],
        compiler_params=_cparams("parallel", "parallel"),
        name="sgu",
    )(x, nw, shift, scale, gate, w_in, gnw, w_s, b_s_t, w_out)


def _topk_rows(s, rid, k):
    big = jnp.iinfo(jnp.int32).max
    vals, idxs = [], []
    for _ in range(k):
        m = jnp.max(s, axis=0, keepdims=True)
        am = jnp.min(jnp.where(s == m, rid, big), axis=0, keepdims=True)
        vals.append(m)
        idxs.append(am)
        s = jnp.where(rid == am, -jnp.inf, s)
    return jnp.concatenate(vals, axis=0), jnp.concatenate(idxs, axis=0)


_PAIR_PIECES = ((0, 1, 0, 16), (1, 1, 0, 8), (2, 1, 0, 8), (3, 1, 0, 8),
                (8, 8, 0, 1), (0, 8, 0, 1), (0, 8, 1, 1), (0, 8, 2, 1))
_PAIR_ROW_PIECES = 4


def _pair_candidates(t0, t1):
    k = PEER_TOPK
    n = t0.shape[1]
    vals, ids = [], []
    for a0, na, b0, nb in _PAIR_PIECES:
        io = lax.broadcasted_iota(I32, (max(na, nb), n), 0)
        a = a0 + io if na > 1 else jnp.full_like(io, a0)
        b = b0 + io if nb > 1 else jnp.full_like(io, b0)
        ok = (a + 1) * (b + 1) <= k
        if na > 1:
            ok = ok & (a >= _PAIR_ROW_PIECES)
        vals.append(jnp.where(ok, t0[a0:a0 + na] + t1[b0:b0 + nb], -jnp.inf))
        ids.append(a * k + b)
    return jnp.concatenate(vals, axis=0), jnp.concatenate(ids, axis=0)


def _select_rows(sel, table):
    out = jnp.zeros(sel.shape, table.dtype)
    for a in range(table.shape[0]):
        out = jnp.where(sel == a, table[a:a + 1], out)
    return out


def _pair_words(lo, hi):
    lo_bits = pltpu.bitcast(lo.astype(F32), U32) >> 16
    hi_bits = pltpu.bitcast(hi.astype(F32), U32) & jnp.uint32(0xFFFF0000)
    return lo_bits | hi_bits


def _peer_idx_kernel(x_ref, nw_ref, sh_ref, sc_ref, wq_ref, keys_ref, hq_ref, e_ref, g_ref, q_ref):
    hq = _norm_mod(x_ref[...], nw_ref[...], sh_ref[0], sc_ref[0])
    hb = hq.astype(BF16)
    half = hq.shape[1] // 2
    hq_ref[...] = _pair_words(hb[:, :half], hb[:, half:])
    q_ref[...] = _bdot(hb, wq_ref[...]).astype(BF16)

    def head(h, carry):
        tops, topi = [], []
        for s in range(2):
            col = pl.multiple_of((h * 2 + s) * PEER_HALF_DIM, PEER_HALF_DIM)
            qs = q_ref[:, pl.ds(col, PEER_HALF_DIM)]
            st = lax.dot_general(keys_ref[h * 2 + s], qs, (((1,), (1,)), ((), ())), preferred_element_type=F32)
            ts, ti = _topk_rows(st, lax.broadcasted_iota(I32, st.shape, 0), PEER_TOPK)
            tops.append(ts)
            topi.append(ti)
        best_s, best = _topk_rows(*_pair_candidates(tops[0], tops[1]), PEER_TOPK)
        i1 = _select_rows(best // PEER_TOPK, topi[0])
        i2 = _select_rows(best % PEER_TOPK, topi[1])
        p = jnp.exp(best_s - best_s[0:1])
        row = pl.multiple_of(h * PEER_TOPK, PEER_TOPK)
        e_ref[0, pl.ds(row, PEER_TOPK), :] = i1 * PEER_NKEYS + i2
        g_ref[0, pl.ds(row, PEER_TOPK), :] = p / jnp.sum(p, axis=0, keepdims=True)
        return carry

    lax.fori_loop(0, PEER_HEADS, head, 0)


def _peer_idx(x, nw, shift, scale, w_q, keys, tok0, ntok, seq):
    d = x.shape[1]
    tm = min(256, ntok, seq)
    nq = w_q.shape[1]
    t0 = tok0 // tm
    per_seq = seq // tm
    mod = pl.BlockSpec((1, 1, d), lambda i: ((i + t0) // per_seq, 0, 0))
    sel = pl.BlockSpec((1, PEER_SEL, tm), lambda i: (i, 0, 0))
    return pl.pallas_call(
        _peer_idx_kernel,
        grid=(ntok // tm,),
        in_specs=[pl.BlockSpec((tm, d), lambda i: (i + t0, 0)),
                  pl.BlockSpec((1, d), lambda i: (0, 0)),
                  mod, mod,
                  pl.BlockSpec((d, nq), lambda i: (0, 0)),
                  pl.BlockSpec(keys.shape, lambda i: (0, 0, 0))],
        out_specs=[pl.BlockSpec((tm, d // 2), lambda i: (i, 0)), sel, sel],
        out_shape=[jax.ShapeDtypeStruct((ntok, d // 2), U32),
                   jax.ShapeDtypeStruct((ntok // tm, PEER_SEL, tm), I32),
                   jax.ShapeDtypeStruct((ntok // tm, PEER_SEL, tm), F32)],
        scratch_shapes=[pltpu.VMEM((tm, nq), BF16)],
        compiler_params=_cparams("parallel"),
        name="peer_idx",
    )(x, nw, shift, scale, w_q, keys)


def _sc_worker_id():
    return lax.axis_index("s") * SC_CORES + lax.axis_index("c")


SC_SLAB = 4


def _sc_pair_products(x, y):
    prod = plsc.bitcast(x, BF16) * plsc.bitcast(y, BF16)
    return plsc.unpack(prod, format=plsc.PackFormat.INTERLEAVED, preferred_element_type=F32)


def _sc_per_token(tab, idx, aux, out_width, chunk_fn):
    k = PEER_SEL
    n_tok, a = aux.shape
    w = tab.shape[1]
    ch = SC_GATHER_ROWS
    per_w = n_tok // SC_WORKERS
    assert k == 2 * ch and per_w % 2 == 0
    mesh = plsc.VectorSubcoreMesh(core_axis_name="c", subcore_axis_name="s")

    @functools.partial(
        pl.kernel, mesh=mesh, out_type=jax.ShapeDtypeStruct((n_tok, out_width), F32),
        scratch_types=[pltpu.VMEM((per_w * k,), I32), pltpu.VMEM((2, a), aux.dtype),
                       pltpu.VMEM((ch, w), U32), pltpu.VMEM((ch, w), U32), pltpu.VMEM((2, out_width), F32),
                       pltpu.SemaphoreType.DMA, pltpu.SemaphoreType.DMA,
                       pltpu.SemaphoreType.DMA((2,)), pltpu.SemaphoreType.DMA((2,))],
        compiler_params=pltpu.CompilerParams(needs_layout_passes=False))
    def per_token(tab_hbm, idx_hbm, aux_hbm, out_hbm, idx_v, aux_v, rows0, rows1, out_v, gsem0, gsem1, asem, osem):
        base = _sc_worker_id() * per_w
        rbuf = ((rows0, gsem0), (rows1, gsem1))
        pltpu.sync_copy(idx_hbm.at[pl.ds(pl.multiple_of(base * k, k), per_w * k)], idx_v)

        def fetch_rows(t, c):
            sel = idx_v.at[pl.ds(pl.multiple_of(t * k + c * ch, ch), ch)]
            return pltpu.make_async_copy(tab_hbm.at[sel], rbuf[c][0], rbuf[c][1])

        def fetch_aux(t, p):
            return pltpu.make_async_copy(aux_hbm.at[base + t], aux_v.at[p], asem.at[p])

        def store_out(t, p):
            return pltpu.make_async_copy(out_v.at[p], out_hbm.at[base + t], osem.at[p])

        fetch_rows(0, 0).start()
        fetch_rows(0, 1).start()
        fetch_aux(0, 0).start()

        @pl.loop(0, per_w)
        def _(t):
            p = lax.rem(t, 2)
            fetch_aux(t, p).wait()

            @pl.when(t + 1 < per_w)
            def _():
                fetch_aux(t + 1, 1 - p).start()

            @pl.when(t >= 2)
            def _():
                store_out(t - 2, p).wait()

            for c in range(2):
                fetch_rows(t, c).wait()
                chunk_fn(c, p, aux_v, rbuf[c][0], out_v)

                @pl.when(t + 1 < per_w)
                def _():
                    fetch_rows(t + 1, c).start()

            store_out(t, p).start()

        store_out(per_w - 2, 0).wait()
        store_out(per_w - 1, 1).wait()

    return per_token(tab, idx, aux)


def _sc_row_dots(tab, idx, hq_pairs):
    w = tab.shape[1]
    lanes, slab, ch = SC_LANES, SC_SLAB, SC_GATHER_ROWS

    rblk = 8

    def chunk(c, p, h_v, rows, part_v):
        @pl.loop(0, ch // rblk)
        def _(rb):
            r0 = rb * rblk
            acc = [jnp.zeros((lanes,), F32) for _ in range(rblk)]
            for sb in range(w // lanes // slab):
                col = sb * slab * lanes
                h = [h_v[p, pl.ds(col + j * lanes, lanes)] for j in range(slab)]
                for r in range(rblk):
                    for j in range(slab):
                        lo, hi = _sc_pair_products(rows[r0 + r, pl.ds(col + j * lanes, lanes)], h[j])
                        acc[r] += lo + hi
            for r in range(rblk):
                part_v[p, pl.ds(pl.multiple_of((c * ch + r0 + r) * lanes, lanes), lanes)] = acc[r]

    return _sc_per_token(tab, idx, hq_pairs, PEER_SEL * lanes, chunk)


def _sc_weighted_rows(tab, idx, w_rep):
    w = tab.shape[1]
    lanes, slab, ch = SC_LANES, SC_SLAB, SC_GATHER_ROWS

    def chunk(c, p, w_v, rows, acc_v):
        @pl.loop(0, w // lanes // slab)
        def _(sb):
            col = pl.multiple_of(sb * (slab * lanes), slab * lanes)
            if c == 0:
                acc = [jnp.zeros((lanes,), F32) for _ in range(2 * slab)]
            else:
                acc = [acc_v[p, pl.ds(part * w + col + j * lanes, lanes)] for j in range(slab) for part in range(2)]
            for r in range(ch):
                wv = w_v[p, pl.ds((c * ch + r) * lanes, lanes)]
                for j in range(slab):
                    lo, hi = _sc_pair_products(rows[r, pl.ds(col + j * lanes, lanes)], wv)
                    acc[2 * j] += lo
                    acc[2 * j + 1] += hi
            for j in range(slab):
                acc_v[p, pl.ds(col + j * lanes, lanes)] = acc[2 * j]
                acc_v[p, pl.ds(w + col + j * lanes, lanes)] = acc[2 * j + 1]

    return _sc_per_token(tab, idx, w_rep, 2 * w, chunk)


def _split3(x):
    hi = x.astype(BF16)
    r1 = x - hi.astype(F32)
    mid = r1.astype(BF16)
    lo = (r1 - mid.astype(F32)).astype(BF16)
    return jnp.concatenate([hi, mid, lo], axis=1)


def _peer_weights_kernel(part_ref, g_ref, fold_ref, rep_ref, w_ref):
    score = _bdot(_split3(part_ref[...]), fold_ref[...])
    wb = (g_ref[...] * jax.nn.gelu(score)).astype(BF16)
    word = pltpu.bitcast(_pair_words(wb, wb), F32)
    w_ref[...] = _bdot(_split3(word), rep_ref[...])


def _peer_weights(part, g):
    ntok, wide = part.shape
    tm = min(256, ntok)
    rep1 = jnp.repeat(jnp.eye(PEER_SEL, dtype=BF16), SC_LANES, axis=1)
    rep = jnp.tile(rep1, (3, 1))
    fold = jnp.tile(rep1.T, (3, 1))
    row = lambda w: pl.BlockSpec((tm, w), lambda i: (i, 0))
    full = lambda a: pl.BlockSpec(a.shape, lambda i: (0, 0))
    return pl.pallas_call(
        _peer_weights_kernel,
        grid=(ntok // tm,),
        in_specs=[row(wide), row(PEER_SEL), full(fold), full(rep)],
        out_specs=row(wide),
        out_shape=jax.ShapeDtypeStruct((ntok, wide), F32),
        compiler_params=_cparams("parallel"),
        name="peer_weights",
    )(part, g, fold, rep)


def _resid_kernel(x_ref, y_ref, gate_ref, o_ref):
    o_ref[...] = x_ref[...] + gate_ref[0] * y_ref[...]


def _resid(x, y, gate, tok0, seq):
    t, d = x.shape
    ntok = y.shape[0]
    tm = min(512, ntok, seq)
    t0 = tok0 // tm
    per_seq = seq // tm
    tok = pl.BlockSpec((tm, d), lambda i: (i + t0, 0))
    return pl.pallas_call(
        _resid_kernel,
        grid=(ntok // tm,),
        in_specs=[tok, pl.BlockSpec((tm, d), lambda i: (i, 0)),
                  pl.BlockSpec((1, 1, d), lambda i: ((i + t0) // per_seq, 0, 0))],
        out_specs=tok,
        out_shape=jax.ShapeDtypeStruct((t, d), F32),
        input_output_aliases={0: 0},
        compiler_params=_cparams("parallel"),
        name="peer_resid",
    )(x, y, gate)


def _peer_groups(t):
    return max(1, min(4, t // 2048))


def _peer_stage(x, peer):
    nw, shift, scale, _, w_q, keys, tab_u, _ = peer
    b, s, d = x.shape
    hq, e_t, g_t = _peer_idx(x.reshape(b * s, d), nw, shift, scale, w_q, keys, 0, b * s, s)
    idx = e_t.transpose(0, 2, 1).reshape(-1)
    g = g_t.transpose(0, 2, 1).reshape(b * s, PEER_SEL)
    return idx, g, _sc_row_dots(tab_u, idx, hq)


def _peer_reduce(peer, staged):
    idx, g, part = staged
    return _sc_weighted_rows(peer[7], idx, _peer_weights(part, g))


def _peer_resid(x, peer, y):
    b, s, d = x.shape
    return _resid(x.reshape(b * s, d), y, peer[3], 0, s).reshape(b, s, d)


def _rms_kernel(x_ref, w_ref, o_ref):
    x = x_ref[...]
    o_ref[...] = x * lax.rsqrt(jnp.mean(x * x, axis=-1, keepdims=True) + RMS_EPS) * w_ref[...]


def _final_norm(x, w):
    b, s, d = x.shape
    t = b * s
    tm = min(512, t)
    out = pl.pallas_call(
        _rms_kernel,
        grid=(t // tm,),
        in_specs=[pl.BlockSpec((tm, d), lambda i: (i, 0)), pl.BlockSpec((1, d), lambda i: (0, 0))],
        out_specs=pl.BlockSpec((tm, d), lambda i: (i, 0)),
        out_shape=jax.ShapeDtypeStruct((t, d), F32),
        compiler_params=_cparams("parallel"),
        name="final_norm",
    )(x.reshape(t, d), w.reshape(1, d))
    return out.reshape(b, s, d)


def _rope_tables(s):
    t = jnp.arange(s)
    row = (t // GRID_W).astype(F32)
    col = (t % GRID_W).astype(F32)
    half = HEAD_DIM // 2
    inv = ROPE_THETA ** (-jnp.arange(0, half, 2, dtype=F32) / half)
    ar = row[:, None] * inv
    ac = col[:, None] * inv
    ang = jnp.concatenate([ar, ar, ac, ac], axis=1)
    q = half // 2
    sign = jnp.concatenate([-jnp.ones(q), jnp.ones(q), -jnp.ones(q), jnp.ones(q)]).astype(F32)
    reps = QK_WIDTH // HEAD_DIM
    return jnp.tile(jnp.cos(ang), (1, reps)), jnp.tile(jnp.sin(ang) * sign, (1, reps))


def _rope_partner_cols():
    j = jnp.arange(QK_WIDTH)
    dd = j % HEAD_DIM
    q = HEAD_DIM // 4
    return j - dd + jnp.where((dd % (2 * q)) < q, dd + q, dd - q)


def _dft_tables(n, dtype):
    j = jnp.arange(n, dtype=I32)
    ang = ((j[:, None] * j[None, :]) % n).astype(F32) * (2.0 * math.pi / n)
    return jnp.cos(ang).astype(dtype), jnp.sin(ang).astype(dtype)


def _pack_table(tab):
    half = tab.shape[1] // 2
    bits = lax.bitcast_convert_type(tab.astype(BF16), jnp.uint16).astype(U32)
    return bits[:, :half] | (bits[:, half:] << 16)


def kernel(x, c, ctx, c_ctx, ada_w, ada_b, norm_mix_w, norm_ffn_w, ab_w_in, ab_w_out, attn_sink, gs_w_in, gs_norm_w,
           gs_w_s, gs_b_s, gs_w_out, peer_w_q, peer_keys, peer_u, peer_v, norm_out_w):
    bsz, seq, d = x.shape
    n_ctx = ctx.shape[1]
    depth = ada_w.shape[0]

    cond = jnp.concatenate([c, c_ctx[None], jnp.zeros((16 - bsz - 1, d), F32)], axis=0)
    mods = _ada_all(cond, ada_w, ada_b)

    cos, sin = _rope_tables(seq)
    partner = _rope_partner_cols()
    dft_cc, dft_cs = _dft_tables(FNET_GROUP_DIM, BF16)
    dft_c = jnp.concatenate([dft_cc, dft_cs], axis=1)
    pos_l = _dft_tables(seq, BF16)
    pos_c = _dft_tables(n_ctx, BF16)

    n_groups = _peer_groups(bsz * seq)
    assert bsz % n_groups == 0
    nb = bsz // n_groups
    xs = [x[g * nb:(g + 1) * nb] for g in range(n_groups)]
    cs = [ctx[g * nb:(g + 1) * nb] for g in range(n_groups)]
    last_attn = ((depth - 1) // 2) * 2

    layers = []
    for i in range(depth):
        j = i // 2
        p = dict(is_ab=i % 2 == 0, upd_ctx=i < last_attn, nw_m=norm_mix_w[i].reshape(1, d),
                 nw_f=norm_ffn_w[i].reshape(1, d), w_q=peer_w_q[i].astype(BF16),
                 keys=peer_keys[i].reshape(PEER_HEADS * 2, PEER_NKEYS, PEER_HALF_DIM).astype(BF16),
                 tabs=(_pack_table(peer_u[i]), _pack_table(peer_v[i])))
        if p["is_ab"]:
            w_in = ab_w_in[j]
            p.update(w_ext=jnp.concatenate([w_in, w_in[:, partner]], axis=1).astype(BF16),
                     w_out=ab_w_out[j].astype(BF16), sink=attn_sink[j].reshape(1, N_Q_HEADS))
        else:
            p.update(sgu_w=(gs_w_in[j].astype(BF16), gs_norm_w[j].reshape(1, -1), gs_w_s[j].astype(BF16),
                            gs_b_s[j].T, gs_w_out[j].astype(BF16)))
        layers.append(p)

    def group_mods(i, g):
        m_l = [m[:, None, :] for m in jnp.split(mods[i, g * nb:(g + 1) * nb], N_MOD, axis=-1)]
        m_c = [jnp.broadcast_to(m[None], (nb, 1, d)) for m in jnp.split(mods[i, bsz:bsz + 1], N_MOD, axis=-1)]
        return m_l, m_c

    def peer_params(i, m):
        p = layers[i]
        return (p["nw_f"], m[3], m[4], m[5], p["w_q"], p["keys"]) + p["tabs"]

    def mix_and_stage(i, g):
        p = layers[i]
        m_l, m_c = group_mods(i, g)
        xg, cg = xs[g], cs[g]
        if p["is_ab"]:
            q_c, k_c, v_c, fa_c, fb_c = _ab_in(cg, p["nw_m"], m_c[0], m_c[1], p["w_ext"], cos[:n_ctx], sin[:n_ctx],
                                               dft_c, False, 0, nb)
            q_l, k_l, v_l, fa_l, fb_l = _ab_in(xg, p["nw_m"], m_l[0], m_l[1], p["w_ext"], cos, sin, dft_c, True, 0, nb)
            a_l = _attn_local(q_l, k_l, v_l, k_c, v_c, p["sink"], 0)
            xg = _ab_out(a_l, _dft_pos(fa_l, fb_l, *pos_l), p["w_out"], xg, m_l[2], 0)
            if p["upd_ctx"]:
                a_c = _attn_ctx(q_c, k_c, v_c, p["sink"])
                cg = _ab_out(a_c, _dft_pos(fa_c, fb_c, *pos_c), p["w_out"], cg, m_c[2], 0)
        else:
            xg = _sgu(xg, p["nw_m"], m_l[0], m_l[1], m_l[2], *p["sgu_w"], 0, nb)
            if p["upd_ctx"]:
                cg = _sgu(cg, p["nw_m"], m_c[0], m_c[1], m_c[2], *p["sgu_w"], 0, nb)
        xs[g], cs[g] = xg, cg
        st_l = _peer_stage(xg, peer_params(i, m_l))
        st_c = _peer_stage(cg, peer_params(i, m_c)) if p["upd_ctx"] else None
        return st_l, st_c

    def reduce_rows(i, g, staged):
        m_l, m_c = group_mods(i, g)
        y_l = _peer_reduce(peer_params(i, m_l), staged[0])
        y_c = _peer_reduce(peer_params(i, m_c), staged[1]) if staged[1] is not None else None
        return y_l, y_c

    def add_residual(i, g, ys):
        m_l, m_c = group_mods(i, g)
        xs[g] = _peer_resid(xs[g], peer_params(i, m_l), ys[0])
        if ys[1] is not None:
            cs[g] = _peer_resid(cs[g], peer_params(i, m_c), ys[1])

    staged = [mix_and_stage(0, g) for g in range(n_groups)]
    for i in range(depth):
        ys = [reduce_rows(i, g, staged[g]) for g in range(n_groups)]
        for g in range(n_groups):
            add_residual(i, g, ys[g])
            if i + 1 < depth:
                staged[g] = mix_and_stage(i + 1, g)
    return jnp.concatenate([_final_norm(xg, norm_out_w) for xg in xs], axis=0)
```

```python
import functools
import math

import jax
import jax.numpy as jnp
from jax import lax
from jax.experimental import pallas as pl
from jax.experimental.pallas import tpu as pltpu
from jax.experimental.pallas import tpu_sc as plsc

F32 = jnp.float32
BF16 = jnp.bfloat16
I32 = jnp.int32
U32 = jnp.uint32

GRID_W = 64
N_Q_HEADS = 8
N_KV_HEADS = 2
Q_PER_KV = N_Q_HEADS // N_KV_HEADS
HEAD_DIM = 64
WINDOW = 128
ROPE_THETA = 10000.0
ATTN_WIDTH = N_Q_HEADS * HEAD_DIM
KV_WIDTH = N_KV_HEADS * HEAD_DIM
QK_WIDTH = ATTN_WIDTH + KV_WIDTH
FNET_GROUPS = 4
FNET_GROUP_DIM = 128
FNET_WIDTH = FNET_GROUPS * FNET_GROUP_DIM
AB_IN_WIDTH = ATTN_WIDTH + 2 * KV_WIDTH + FNET_WIDTH
SGU_GROUPS = 8
SGU_GROUP_DIM = 128
SGU_WIDTH = SGU_GROUPS * SGU_GROUP_DIM
SGU_CHUNK = 128
PEER_HEADS = 8
PEER_NKEYS = 128
PEER_TOPK = 16
PEER_HALF_DIM = 128
PEER_SEL = PEER_HEADS * PEER_TOPK
N_MOD = 6
RMS_EPS = 1e-6
NEG_INF = -1e30

SC_CORES = 2
SC_SUBCORES = 16
SC_WORKERS = SC_CORES * SC_SUBCORES
SC_LANES = 16
SC_GATHER_ROWS = 64
VMEM_LIMIT = 48 * 1024 * 1024


def _cparams(*sem):
    return pltpu.CompilerParams(dimension_semantics=sem, vmem_limit_bytes=VMEM_LIMIT)


def _norm_mod(x, nw, shift, scale):
    y = x * lax.rsqrt(jnp.mean(x * x, axis=-1, keepdims=True) + RMS_EPS) * nw
    return y * (1.0 + scale) + shift


def _bdot(a, b):
    return jnp.dot(a, b, preferred_element_type=F32)


def _ada_kernel(c_ref, w_ref, b_ref, o_ref):
    c = c_ref[...]
    s = (c * jax.nn.sigmoid(c)).astype(BF16)
    o_ref[0] = _bdot(s, w_ref[0].astype(BF16)) + b_ref[0]


def _ada_all(cond, ada_w, ada_b):
    depth, d, n = ada_w.shape
    r = cond.shape[0]
    tn = 1536
    return pl.pallas_call(
        _ada_kernel,
        grid=(depth, n // tn),
        in_specs=[pl.BlockSpec((r, d), lambda l, j: (0, 0)),
                  pl.BlockSpec((1, d, tn), lambda l, j: (l, 0, j)),
                  pl.BlockSpec((1, 1, tn), lambda l, j: (l, 0, j))],
        out_specs=pl.BlockSpec((1, r, tn), lambda l, j: (l, 0, j)),
        out_shape=jax.ShapeDtypeStruct((depth, r, n), F32),
        compiler_params=_cparams("parallel", "parallel"),
        name="ada",
    )(cond, ada_w, ada_b.reshape(depth, 1, n))


def _ab_in_kernel(x_ref, nw_ref, sh_ref, sc_ref, w_ref, cos_ref, sin_ref, dft_ref,
                  q_ref, k_ref, v_ref, fa_ref, fb_ref, *, rope):
    h = _norm_mod(x_ref[0], nw_ref[...], sh_ref[0], sc_ref[0]).astype(BF16)
    p = _bdot(h, w_ref[...])
    qk = p[:, :QK_WIDTH]
    if rope:
        qk = qk * cos_ref[...] + p[:, AB_IN_WIDTH:] * sin_ref[...]
    q_ref[0] = qk[:, :ATTN_WIDTH].astype(BF16)
    k_ref[0] = qk[:, ATTN_WIDTH:].astype(BF16)
    v_ref[0] = p[:, QK_WIDTH:QK_WIDTH + KV_WIDTH].astype(BF16)
    f0 = QK_WIDTH + KV_WIDTH
    for g in range(FNET_GROUPS):
        fg = p[:, f0 + g * FNET_GROUP_DIM:f0 + (g + 1) * FNET_GROUP_DIM].astype(BF16)
        ab = _bdot(fg, dft_ref[...])
        fa_ref[0, :, g * FNET_GROUP_DIM:(g + 1) * FNET_GROUP_DIM] = ab[:, :FNET_GROUP_DIM].astype(BF16)
        fb_ref[0, :, g * FNET_GROUP_DIM:(g + 1) * FNET_GROUP_DIM] = ab[:, FNET_GROUP_DIM:].astype(BF16)


def _ab_in(x, nw, shift, scale, w_ext, cos, sin, dft_c, rope, b0, nb):
    _, s, d = x.shape
    tm = min(512, s)
    n_ext = w_ext.shape[1]
    outs = [jax.ShapeDtypeStruct((nb, s, ATTN_WIDTH), BF16), jax.ShapeDtypeStruct((nb, s, KV_WIDTH), BF16),
            jax.ShapeDtypeStruct((nb, s, KV_WIDTH), BF16), jax.ShapeDtypeStruct((nb, s, FNET_WIDTH), BF16),
            jax.ShapeDtypeStruct((nb, s, FNET_WIDTH), BF16)]
    tok = lambda w: pl.BlockSpec((1, tm, w), lambda i, j: (i, j, 0))
    return pl.pallas_call(
        functools.partial(_ab_in_kernel, rope=rope),
        grid=(nb, s // tm),
        in_specs=[pl.BlockSpec((1, tm, d), lambda i, j: (i + b0, j, 0)),
                  pl.BlockSpec((1, d), lambda i, j: (0, 0)),
                  pl.BlockSpec((1, 1, d), lambda i, j: (i + b0, 0, 0)),
                  pl.BlockSpec((1, 1, d), lambda i, j: (i + b0, 0, 0)),
                  pl.BlockSpec((d, n_ext), lambda i, j: (0, 0)),
                  pl.BlockSpec((tm, QK_WIDTH), lambda i, j: (j, 0)),
                  pl.BlockSpec((tm, QK_WIDTH), lambda i, j: (j, 0)),
                  pl.BlockSpec(dft_c.shape, lambda i, j: (0, 0))],
        out_specs=[tok(ATTN_WIDTH), tok(KV_WIDTH), tok(KV_WIDTH), tok(FNET_WIDTH), tok(FNET_WIDTH)],
        out_shape=outs,
        compiler_params=_cparams("parallel", "parallel"),
        name="ab_in",
    )(x, nw, shift, scale, w_ext, cos, sin, dft_c)


def _softmax_pv(qh, kk, vv, valid, sk, scale):
    s = lax.dot_general(qh, kk, (((1,), (1,)), ((), ())), preferred_element_type=F32) * scale
    if valid is not None:
        s = jnp.where(valid, s, NEG_INF)
    m = jnp.maximum(jnp.max(s, axis=-1, keepdims=True), sk)
    p = jnp.exp(s - m)
    inv_den = 1.0 / (jnp.sum(p, axis=-1, keepdims=True) + jnp.exp(sk - m))
    return _bdot((p * inv_den).astype(BF16), vv)


def _gqa_group(q, k_all, v_all, valid, sink_ref, kvh, o_ref):
    n = q.shape[0]
    heads = range(kvh * Q_PER_KV, (kvh + 1) * Q_PER_KV)
    kk = k_all[:, kvh * HEAD_DIM:(kvh + 1) * HEAD_DIM]
    vv = v_all[:, kvh * HEAD_DIM:(kvh + 1) * HEAD_DIM]
    q4 = jnp.concatenate([q[:, h * HEAD_DIM:(h + 1) * HEAD_DIM] for h in heads], axis=0)
    sk4 = jnp.concatenate([jnp.broadcast_to(sink_ref[0:1, h:h + 1], (n, 1)) for h in heads], axis=0)
    valid4 = None if valid is None else jnp.concatenate([valid] * Q_PER_KV, axis=0)
    o4 = _softmax_pv(q4, kk, vv, valid4, sk4, HEAD_DIM ** -0.5)
    for g, h in enumerate(heads):
        o_ref[0, :, h * HEAD_DIM:(h + 1) * HEAD_DIM] = o4[g * n:(g + 1) * n].astype(BF16)


def _attn_local_kernel(q_ref, kp_ref, ko_ref, kn_ref, vp_ref, vo_ref, vn_ref, kc_ref, vc_ref, sink_ref, o_ref,
                       *, seq):
    j = pl.program_id(1)
    blk = q_ref.shape[1]
    n_ctx = kc_ref.shape[1]
    q = q_ref[0]
    kl = jnp.concatenate([kp_ref[0], ko_ref[0], kn_ref[0], kc_ref[0]], axis=0)
    vl = jnp.concatenate([vp_ref[0], vo_ref[0], vn_ref[0], vc_ref[0]], axis=0)
    nk = 3 * blk + n_ctx
    qi = lax.broadcasted_iota(I32, (blk, nk), 0)
    cj = lax.broadcasted_iota(I32, (blk, nk), 1)
    kj = cj - blk
    kpos = j * blk + kj
    valid = (cj >= 3 * blk) | ((jnp.abs(qi - kj) <= WINDOW) & (kpos >= 0) & (kpos < seq))
    for kvh in range(N_KV_HEADS):
        _gqa_group(q, kl, vl, valid, sink_ref, kvh, o_ref)


def _attn_local(q, k, v, kc, vc, sink, b0):
    b, s, _ = q.shape
    c = kc.shape[1]
    blk = WINDOW
    nb = s // blk
    qspec = pl.BlockSpec((1, blk, ATTN_WIDTH), lambda i, j: (i, j, 0))
    prev = pl.BlockSpec((1, blk, KV_WIDTH), lambda i, j: (i, jnp.maximum(j - 1, 0), 0))
    own = pl.BlockSpec((1, blk, KV_WIDTH), lambda i, j: (i, j, 0))
    nxt = pl.BlockSpec((1, blk, KV_WIDTH), lambda i, j: (i, jnp.minimum(j + 1, nb - 1), 0))
    cspec = pl.BlockSpec((1, c, KV_WIDTH), lambda i, j: (i + b0, 0, 0))
    return pl.pallas_call(
        functools.partial(_attn_local_kernel, seq=s),
        grid=(b, nb),
        in_specs=[qspec, prev, own, nxt, prev, own, nxt, cspec, cspec,
                  pl.BlockSpec((1, N_Q_HEADS), lambda i, j: (0, 0))],
        out_specs=qspec,
        out_shape=jax.ShapeDtypeStruct((b, s, ATTN_WIDTH), BF16),
        compiler_params=_cparams("parallel", "parallel"),
        name="attn_local",
    )(q, k, k, k, v, v, v, kc, vc, sink)


def _attn_ctx_kernel(q_ref, k_ref, v_ref, sink_ref, o_ref):
    for kvh in range(N_KV_HEADS):
        _gqa_group(q_ref[0], k_ref[0], v_ref[0], None, sink_ref, kvh, o_ref)


def _attn_ctx(q, k, v, sink):
    b, c, _ = q.shape
    return pl.pallas_call(
        _attn_ctx_kernel,
        grid=(b,),
        in_specs=[pl.BlockSpec((1, c, ATTN_WIDTH), lambda i: (i, 0, 0)),
                  pl.BlockSpec((1, c, KV_WIDTH), lambda i: (i, 0, 0)),
                  pl.BlockSpec((1, c, KV_WIDTH), lambda i: (i, 0, 0)),
                  pl.BlockSpec((1, N_Q_HEADS), lambda i: (0, 0))],
        out_specs=pl.BlockSpec((1, c, ATTN_WIDTH), lambda i: (i, 0, 0)),
        out_shape=jax.ShapeDtypeStruct((b, c, ATTN_WIDTH), BF16),
        compiler_params=_cparams("parallel"),
        name="attn_ctx",
    )(q, k, v, sink)


def _dft_pos_kernel(cs_ref, ss_ref, a_ref, b_ref, o_ref, acc_ref, *, scale):
    kk = pl.program_id(2)

    @pl.when(kk == 0)
    def _():
        acc_ref[...] = jnp.zeros_like(acc_ref)

    acc_ref[...] += _bdot(cs_ref[...], a_ref[0]) - _bdot(ss_ref[...], b_ref[0])

    @pl.when(kk == pl.num_programs(2) - 1)
    def _():
        o_ref[0] = (acc_ref[...] * scale).astype(BF16)


def _dft_pos(fa, fb, cs, ss):
    b, s, w = fa.shape
    t = min(1024, s)
    scale = 1.0 / math.sqrt(s * FNET_GROUP_DIM)
    return pl.pallas_call(
        functools.partial(_dft_pos_kernel, scale=scale),
        grid=(b, s // t, s // t),
        in_specs=[pl.BlockSpec((t, t), lambda i, m, k: (m, k)),
                  pl.BlockSpec((t, t), lambda i, m, k: (m, k)),
                  pl.BlockSpec((1, t, w), lambda i, m, k: (i, k, 0)),
                  pl.BlockSpec((1, t, w), lambda i, m, k: (i, k, 0))],
        out_specs=pl.BlockSpec((1, t, w), lambda i, m, k: (i, m, 0)),
        out_shape=jax.ShapeDtypeStruct((b, s, w), BF16),
        scratch_shapes=[pltpu.VMEM((t, w), F32)],
        compiler_params=_cparams("parallel", "parallel", "arbitrary"),
        name="dft_pos",
    )(cs, ss, fa, fb)


def _ab_out_kernel(a_ref, f_ref, w1_ref, w2_ref, x_ref, g_ref, o_ref):
    y = _bdot(a_ref[0], w1_ref[...]) + _bdot(f_ref[0], w2_ref[...])
    o_ref[0] = x_ref[0] + g_ref[0] * y


def _ab_out(a, fm, w_out, x, gate, b0):
    nb = a.shape[0]
    _, s, d = x.shape
    tm = min(512, s)
    tok = lambda w: pl.BlockSpec((1, tm, w), lambda i, j: (i, j, 0))
    xtok = pl.BlockSpec((1, tm, d), lambda i, j: (i + b0, j, 0))
    return pl.pallas_call(
        _ab_out_kernel,
        grid=(nb, s // tm),
        in_specs=[tok(ATTN_WIDTH), tok(FNET_WIDTH),
                  pl.BlockSpec((ATTN_WIDTH, d), lambda i, j: (0, 0)),
                  pl.BlockSpec((FNET_WIDTH, d), lambda i, j: (0, 0)),
                  xtok,
                  pl.BlockSpec((1, 1, d), lambda i, j: (i + b0, 0, 0))],
        out_specs=xtok,
        out_shape=jax.ShapeDtypeStruct(x.shape, F32),
        input_output_aliases={4: 0},
        compiler_params=_cparams("parallel", "parallel"),
        name="ab_out",
    )(a, fm, w_out[:ATTN_WIDTH], w_out[ATTN_WIDTH:], x, gate)


def _sgu_kernel(x_ref, nw_ref, sh_ref, sc_ref, g_ref, win_ref, gnw_ref, ws_ref, bs_ref, wout_ref, o_ref, uv_ref):
    x = x_ref[0]
    tm = x.shape[0]
    h = _norm_mod(x, nw_ref[...], sh_ref[0], sc_ref[0]).astype(BF16)
    z = jax.nn.gelu(_bdot(h, win_ref[...]))
    u = z[:, :SGU_WIDTH]
    v = z[:, SGU_WIDTH:]
    v = (v * lax.rsqrt(jnp.mean(v * v, axis=-1, keepdims=True) + RMS_EPS) * gnw_ref[...]).astype(BF16)
    for n in range(tm // SGU_CHUNK):
        r = slice(n * SGU_CHUNK, (n + 1) * SGU_CHUNK)
        for g in range(SGU_GROUPS):
            cs = slice(g * SGU_GROUP_DIM, (g + 1) * SGU_GROUP_DIM)
            sv = _bdot(ws_ref[g], v[r, cs]) + bs_ref[:, g:g + 1]
            uv_ref[r, cs] = (u[r, cs] * sv).astype(BF16)
    o_ref[0] = x + g_ref[0] * _bdot(uv_ref[...], wout_ref[...])


def _sgu(x, nw, shift, scale, gate, w_in, gnw, w_s, b_s_t, w_out, b0, nb):
    _, s, d = x.shape
    tm = min(256, s)
    tok = pl.BlockSpec((1, tm, d), lambda i, j: (i + b0, j, 0))
    mod = pl.BlockSpec((1, 1, d), lambda i, j: (i + b0, 0, 0))
    full = lambda a: pl.BlockSpec(a.shape, lambda i, j: (0,) * a.ndim)
    return pl.pallas_call(
        _sgu_kernel,
        grid=(nb, s // tm),
        in_specs=[tok, full(nw), mod, mod, mod, full(w_in), full(gnw), full(w_s), full(b_s_t), full(w_out)],
        out_specs=tok,
        out_shape=jax.ShapeDtypeStruct(x.shape, F32),
        input_output_aliases={0: 0},
        scratch_shapes=[pltpu.VMEM((tm, SGU_WIDTH), BF16)],
        compiler_params=_cparams("parallel", "parallel"),
        name="sgu",
    )(x, nw, shift, scale, gate, w_in, gnw, w_s, b_s_t, w_out)


def _topk_rows(s, rid, k):
    big = jnp.iinfo(jnp.int32).max
    vals, idxs = [], []
    for _ in range(k):
        m = jnp.max(s, axis=0, keepdims=True)
        am = jnp.min(jnp.where(s == m, rid, big), axis=0, keepdims=True)
        vals.append(m)
        idxs.append(am)
        s = jnp.where(rid == am, -jnp.inf, s)
    return jnp.concatenate(vals, axis=0), jnp.concatenate(idxs, axis=0)


_PAIR_PIECES = ((0, 1, 0, 16), (1, 1, 0, 8), (2, 1, 0, 8), (3, 1, 0, 8),
                (8, 8, 0, 1), (0, 8, 0, 1), (0, 8, 1, 1), (0, 8, 2, 1))
_PAIR_ROW_PIECES = 4


def _pair_candidates(t0, t1):
    k = PEER_TOPK
    n = t0.shape[1]
    vals, ids = [], []
    for a0, na, b0, nb in _PAIR_PIECES:
        io = lax.broadcasted_iota(I32, (max(na, nb), n), 0)
        a = a0 + io if na > 1 else jnp.full_like(io, a0)
        b = b0 + io if nb > 1 else jnp.full_like(io, b0)
        ok = (a + 1) * (b + 1) <= k
        if na > 1:
            ok = ok & (a >= _PAIR_ROW_PIECES)
        vals.append(jnp.where(ok, t0[a0:a0 + na] + t1[b0:b0 + nb], -jnp.inf))
        ids.append(a * k + b)
    return jnp.concatenate(vals, axis=0), jnp.concatenate(ids, axis=0)


def _select_rows(sel, table):
    out = jnp.zeros(sel.shape, table.dtype)
    for a in range(table.shape[0]):
        out = jnp.where(sel == a, table[a:a + 1], out)
    return out


def _pair_words(lo, hi):
    lo_bits = pltpu.bitcast(lo.astype(F32), U32) >> 16
    hi_bits = pltpu.bitcast(hi.astype(F32), U32) & jnp.uint32(0xFFFF0000)
    return lo_bits | hi_bits


def _peer_idx_kernel(x_ref, nw_ref, sh_ref, sc_ref, wq_ref, keys_ref, hq_ref, e_ref, g_ref, q_ref):
    hq = _norm_mod(x_ref[...], nw_ref[...], sh_ref[0], sc_ref[0])
    hb = hq.astype(BF16)
    half = hq.shape[1] // 2
    hq_ref[...] = _pair_words(hb[:, :half], hb[:, half:])
    q_ref[...] = _bdot(hb, wq_ref[...]).astype(BF16)

    def head(h, carry):
        tops, topi = [], []
        for s in range(2):
            col = pl.multiple_of((h * 2 + s) * PEER_HALF_DIM, PEER_HALF_DIM)
            qs = q_ref[:, pl.ds(col, PEER_HALF_DIM)]
            st = lax.dot_general(keys_ref[h * 2 + s], qs, (((1,), (1,)), ((), ())), preferred_element_type=F32)
            ts, ti = _topk_rows(st, lax.broadcasted_iota(I32, st.shape, 0), PEER_TOPK)
            tops.append(ts)
            topi.append(ti)
        best_s, best = _topk_rows(*_pair_candidates(tops[0], tops[1]), PEER_TOPK)
        i1 = _select_rows(best // PEER_TOPK, topi[0])
        i2 = _select_rows(best % PEER_TOPK, topi[1])
        p = jnp.exp(best_s - best_s[0:1])
        row = pl.multiple_of(h * PEER_TOPK, PEER_TOPK)
        e_ref[0, pl.ds(row, PEER_TOPK), :] = i1 * PEER_NKEYS + i2
        g_ref[0, pl.ds(row, PEER_TOPK), :] = p / jnp.sum(p, axis=0, keepdims=True)
        return carry

    lax.fori_loop(0, PEER_HEADS, head, 0)


def _peer_idx(x, nw, shift, scale, w_q, keys, tok0, ntok, seq):
    d = x.shape[1]
    tm = min(256, ntok, seq)
    nq = w_q.shape[1]
    t0 = tok0 // tm
    per_seq = seq // tm
    mod = pl.BlockSpec((1, 1, d), lambda i: ((i + t0) // per_seq, 0, 0))
    sel = pl.BlockSpec((1, PEER_SEL, tm), lambda i: (i, 0, 0))
    return pl.pallas_call(
        _peer_idx_kernel,
        grid=(ntok // tm,),
        in_specs=[pl.BlockSpec((tm, d), lambda i: (i + t0, 0)),
                  pl.BlockSpec((1, d), lambda i: (0, 0)),
                  mod, mod,
                  pl.BlockSpec((d, nq), lambda i: (0, 0)),
                  pl.BlockSpec(keys.shape, lambda i: (0, 0, 0))],
        out_specs=[pl.BlockSpec((tm, d // 2), lambda i: (i, 0)), sel, sel],
        out_shape=[jax.ShapeDtypeStruct((ntok, d // 2), U32),
                   jax.ShapeDtypeStruct((ntok // tm, PEER_SEL, tm), I32),
                   jax.ShapeDtypeStruct((ntok // tm, PEER_SEL, tm), F32)],
        scratch_shapes=[pltpu.VMEM((tm, nq), BF16)],
        compiler_params=_cparams("parallel"),
        name="peer_idx",
    )(x, nw, shift, scale, w_q, keys)


def _sc_worker_id():
    return lax.axis_index("s") * SC_CORES + lax.axis_index("c")


SC_SLAB = 4


def _sc_pair_products(x, y):
    prod = plsc.bitcast(x, BF16) * plsc.bitcast(y, BF16)
    return plsc.unpack(prod, format=plsc.PackFormat.INTERLEAVED, preferred_element_type=F32)


def _sc_per_token(tab, idx, aux, out_width, chunk_fn):
    k = PEER_SEL
    n_tok, a = aux.shape
    w = tab.shape[1]
    ch = SC_GATHER_ROWS
    per_w = n_tok // SC_WORKERS
    assert k == 2 * ch and per_w % 2 == 0
    mesh = plsc.VectorSubcoreMesh(core_axis_name="c", subcore_axis_name="s")

    @functools.partial(
        pl.kernel, mesh=mesh, out_type=jax.ShapeDtypeStruct((n_tok, out_width), F32),
        scratch_types=[pltpu.VMEM((per_w * k,), I32), pltpu.VMEM((2, a), aux.dtype),
                       pltpu.VMEM((ch, w), U32), pltpu.VMEM((ch, w), U32), pltpu.VMEM((2, out_width), F32),
                       pltpu.SemaphoreType.DMA, pltpu.SemaphoreType.DMA,
                       pltpu.SemaphoreType.DMA((2,)), pltpu.SemaphoreType.DMA((2,))],
        compiler_params=pltpu.CompilerParams(needs_layout_passes=False))
    def per_token(tab_hbm, idx_hbm, aux_hbm, out_hbm, idx_v, aux_v, rows0, rows1, out_v, gsem0, gsem1, asem, osem):
        base = _sc_worker_id() * per_w
        rbuf = ((rows0, gsem0), (rows1, gsem1))
        pltpu.sync_copy(idx_hbm.at[pl.ds(pl.multiple_of(base * k, k), per_w * k)], idx_v)

        def fetch_rows(t, c):
            sel = idx_v.at[pl.ds(pl.multiple_of(t * k + c * ch, ch), ch)]
            return pltpu.make_async_copy(tab_hbm.at[sel], rbuf[c][0], rbuf[c][1])

        def fetch_aux(t, p):
            return pltpu.make_async_copy(aux_hbm.at[base + t], aux_v.at[p], asem.at[p])

        def store_out(t, p):
            return pltpu.make_async_copy(out_v.at[p], out_hbm.at[base + t], osem.at[p])

        fetch_rows(0, 0).start()
        fetch_rows(0, 1).start()
        fetch_aux(0, 0).start()

        @pl.loop(0, per_w)
        def _(t):
            p = lax.rem(t, 2)
            fetch_aux(t, p).wait()

            @pl.when(t + 1 < per_w)
            def _():
                fetch_aux(t + 1, 1 - p).start()

            @pl.when(t >= 2)
            def _():
                store_out(t - 2, p).wait()

            for c in range(2):
                fetch_rows(t, c).wait()
                chunk_fn(c, p, aux_v, rbuf[c][0], out_v)

                @pl.when(t + 1 < per_w)
                def _():
                    fetch_rows(t + 1, c).start()

            store_out(t, p).start()

        store_out(per_w - 2, 0).wait()
        store_out(per_w - 1, 1).wait()

    return per_token(tab, idx, aux)


def _sc_row_dots(tab, idx, hq_pairs):
    w = tab.shape[1]
    lanes, slab, ch = SC_LANES, SC_SLAB, SC_GATHER_ROWS

    rblk = 8

    def chunk(c, p, h_v, rows, part_v):
        @pl.loop(0, ch // rblk)
        def _(rb):
            r0 = rb * rblk
            acc = [jnp.zeros((lanes,), F32) for _ in range(rblk)]
            for sb in range(w // lanes // slab):
                col = sb * slab * lanes
                h = [h_v[p, pl.ds(col + j * lanes, lanes)] for j in range(slab)]
                for r in range(rblk):
                    for j in range(slab):
                        lo, hi = _sc_pair_products(rows[r0 + r, pl.ds(col + j * lanes, lanes)], h[j])
                        acc[r] += lo + hi
            for r in range(rblk):
                part_v[p, pl.ds(pl.multiple_of((c * ch + r0 + r) * lanes, lanes), lanes)] = acc[r]

    return _sc_per_token(tab, idx, hq_pairs, PEER_SEL * lanes, chunk)


def _sc_weighted_rows(tab, idx, w_rep):
    w = tab.shape[1]
    lanes, slab, ch = SC_LANES, SC_SLAB, SC_GATHER_ROWS

    def chunk(c, p, w_v, rows, acc_v):
        @pl.loop(0, w // lanes // slab)
        def _(sb):
            col = pl.multiple_of(sb * (slab * lanes), slab * lanes)
            if c == 0:
                acc = [jnp.zeros((lanes,), F32) for _ in range(2 * slab)]
            else:
                acc = [acc_v[p, pl.ds(part * w + col + j * lanes, lanes)] for j in range(slab) for part in range(2)]
            for r in range(ch):
                wv = w_v[p, pl.ds((c * ch + r) * lanes, lanes)]
                for j in range(slab):
                    lo, hi = _sc_pair_products(rows[r, pl.ds(col + j * lanes, lanes)], wv)
                    acc[2 * j] += lo
                    acc[2 * j + 1] += hi
            for j in range(slab):
                acc_v[p, pl.ds(col + j * lanes, lanes)] = acc[2 * j]
                acc_v[p, pl.ds(w + col + j * lanes, lanes)] = acc[2 * j + 1]

    return _sc_per_token(tab, idx, w_rep, 2 * w, chunk)


def _split3(x):
    hi = x.astype(BF16)
    r1 = x - hi.astype(F32)
    mid = r1.astype(BF16)
    lo = (r1 - mid.astype(F32)).astype(BF16)
    return jnp.concatenate([hi, mid, lo], axis=1)


def _peer_weights_kernel(part_ref, g_ref, fold_ref, rep_ref, w_ref):
    score = _bdot(_split3(part_ref[...]), fold_ref[...])
    wb = (g_ref[...] * jax.nn.gelu(score)).astype(BF16)
    word = pltpu.bitcast(_pair_words(wb, wb), F32)
    w_ref[...] = _bdot(_split3(word), rep_ref[...])


def _peer_weights(part, g):
    ntok, wide = part.shape
    tm = min(256, ntok)
    rep1 = jnp.repeat(jnp.eye(PEER_SEL, dtype=BF16), SC_LANES, axis=1)
    rep = jnp.tile(rep1, (3, 1))
    fold = jnp.tile(rep1.T, (3, 1))
    row = lambda w: pl.BlockSpec((tm, w), lambda i: (i, 0))
    full = lambda a: pl.BlockSpec(a.shape, lambda i: (0, 0))
    return pl.pallas_call(
        _peer_weights_kernel,
        grid=(ntok // tm,),
        in_specs=[row(wide), row(PEER_SEL), full(fold), full(rep)],
        out_specs=row(wide),
        out_shape=jax.ShapeDtypeStruct((ntok, wide), F32),
        compiler_params=_cparams("parallel"),
        name="peer_weights",
    )(part, g, fold, rep)


def _resid_kernel(x_ref, y_ref, gate_ref, o_ref):
    o_ref[...] = x_ref[...] + gate_ref[0] * y_ref[...]


def _resid(x, y, gate, tok0, seq):
    t, d = x.shape
    ntok = y.shape[0]
    tm = min(512, ntok, seq)
    t0 = tok0 // tm
    per_seq = seq // tm
    tok = pl.BlockSpec((tm, d), lambda i: (i + t0, 0))
    return pl.pallas_call(
        _resid_kernel,
        grid=(ntok // tm,),
        in_specs=[tok, pl.BlockSpec((tm, d), lambda i: (i, 0)),
                  pl.BlockSpec((1, 1, d), lambda i: ((i + t0) // per_seq, 0, 0))],
        out_specs=tok,
        out_shape=jax.ShapeDtypeStruct((t, d), F32),
        input_output_aliases={0: 0},
        compiler_params=_cparams("parallel"),
        name="peer_resid",
    )(x, y, gate)


def _peer_groups(t):
    return max(1, min(8, t // 2048))


def _peer_stage(x, peer):
    nw, shift, scale, _, w_q, keys, tab_u, _ = peer
    b, s, d = x.shape
    hq, e_t, g_t = _peer_idx(x.reshape(b * s, d), nw, shift, scale, w_q, keys, 0, b * s, s)
    idx = e_t.transpose(0, 2, 1).reshape(-1)
    g = g_t.transpose(0, 2, 1).reshape(b * s, PEER_SEL)
    return idx, g, _sc_row_dots(tab_u, idx, hq)


def _peer_reduce(peer, staged):
    idx, g, part = staged
    return _sc_weighted_rows(peer[7], idx, _peer_weights(part, g))


def _peer_resid(x, peer, y):
    b, s, d = x.shape
    return _resid(x.reshape(b * s, d), y, peer[3], 0, s).reshape(b, s, d)


def _rms_kernel(x_ref, w_ref, o_ref):
    x = x_ref[...]
    o_ref[...] = x * lax.rsqrt(jnp.mean(x * x, axis=-1, keepdims=True) + RMS_EPS) * w_ref[...]


def _final_norm(x, w):
    b, s, d = x.shape
    t = b * s
    tm = min(512, t)
    out = pl.pallas_call(
        _rms_kernel,
        grid=(t // tm,),
        in_specs=[pl.BlockSpec((tm, d), lambda i: (i, 0)), pl.BlockSpec((1, d), lambda i: (0, 0))],
        out_specs=pl.BlockSpec((tm, d), lambda i: (i, 0)),
        out_shape=jax.ShapeDtypeStruct((t, d), F32),
        compiler_params=_cparams("parallel"),
        name="final_norm",
    )(x.reshape(t, d), w.reshape(1, d))
    return out.reshape(b, s, d)


def _rope_tables(s):
    t = jnp.arange(s)
    row = (t // GRID_W).astype(F32)
    col = (t % GRID_W).astype(F32)
    half = HEAD_DIM // 2
    inv = ROPE_THETA ** (-jnp.arange(0, half, 2, dtype=F32) / half)
    ar = row[:, None] * inv
    ac = col[:, None] * inv
    ang = jnp.concatenate([ar, ar, ac, ac], axis=1)
    q = half // 2
    sign = jnp.concatenate([-jnp.ones(q), jnp.ones(q), -jnp.ones(q), jnp.ones(q)]).astype(F32)
    reps = QK_WIDTH // HEAD_DIM
    return jnp.tile(jnp.cos(ang), (1, reps)), jnp.tile(jnp.sin(ang) * sign, (1, reps))


def _rope_partner_cols():
    j = jnp.arange(QK_WIDTH)
    dd = j % HEAD_DIM
    q = HEAD_DIM // 4
    return j - dd + jnp.where((dd % (2 * q)) < q, dd + q, dd - q)


def _dft_tables(n, dtype):
    j = jnp.arange(n, dtype=I32)
    ang = ((j[:, None] * j[None, :]) % n).astype(F32) * (2.0 * math.pi / n)
    return jnp.cos(ang).astype(dtype), jnp.sin(ang).astype(dtype)


def _pack_table(tab):
    half = tab.shape[1] // 2
    bits = lax.bitcast_convert_type(tab.astype(BF16), jnp.uint16).astype(U32)
    return bits[:, :half] | (bits[:, half:] << 16)


def kernel(x, c, ctx, c_ctx, ada_w, ada_b, norm_mix_w, norm_ffn_w, ab_w_in, ab_w_out, attn_sink, gs_w_in, gs_norm_w,
           gs_w_s, gs_b_s, gs_w_out, peer_w_q, peer_keys, peer_u, peer_v, norm_out_w):
    bsz, seq, d = x.shape
    n_ctx = ctx.shape[1]
    depth = ada_w.shape[0]

    cond = jnp.concatenate([c, c_ctx[None], jnp.zeros((16 - bsz - 1, d), F32)], axis=0)
    mods = _ada_all(cond, ada_w, ada_b)

    cos, sin = _rope_tables(seq)
    partner = _rope_partner_cols()
    dft_cc, dft_cs = _dft_tables(FNET_GROUP_DIM, BF16)
    dft_c = jnp.concatenate([dft_cc, dft_cs], axis=1)
    pos_l = _dft_tables(seq, BF16)
    pos_c = _dft_tables(n_ctx, BF16)

    n_groups = _peer_groups(bsz * seq)
    assert bsz % n_groups == 0
    nb = bsz // n_groups
    xs = [x[g * nb:(g + 1) * nb] for g in range(n_groups)]
    cs = [ctx[g * nb:(g + 1) * nb] for g in range(n_groups)]
    last_attn = ((depth - 1) // 2) * 2

    layers = []
    for i in range(depth):
        j = i // 2
        p = dict(is_ab=i % 2 == 0, upd_ctx=i < last_attn, nw_m=norm_mix_w[i].reshape(1, d),
                 nw_f=norm_ffn_w[i].reshape(1, d), w_q=peer_w_q[i].astype(BF16),
                 keys=peer_keys[i].reshape(PEER_HEADS * 2, PEER_NKEYS, PEER_HALF_DIM).astype(BF16),
                 tabs=(_pack_table(peer_u[i]), _pack_table(peer_v[i])))
        if p["is_ab"]:
            w_in = ab_w_in[j]
            p.update(w_ext=jnp.concatenate([w_in, w_in[:, partner]], axis=1).astype(BF16),
                     w_out=ab_w_out[j].astype(BF16), sink=attn_sink[j].reshape(1, N_Q_HEADS))
        else:
            p.update(sgu_w=(gs_w_in[j].astype(BF16), gs_norm_w[j].reshape(1, -1), gs_w_s[j].astype(BF16),
                            gs_b_s[j].T, gs_w_out[j].astype(BF16)))
        layers.append(p)

    def group_mods(i, g):
        m_l = [m[:, None, :] for m in jnp.split(mods[i, g * nb:(g + 1) * nb], N_MOD, axis=-1)]
        m_c = [jnp.broadcast_to(m[None], (nb, 1, d)) for m in jnp.split(mods[i, bsz:bsz + 1], N_MOD, axis=-1)]
        return m_l, m_c

    def peer_params(i, m):
        p = layers[i]
        return (p["nw_f"], m[3], m[4], m[5], p["w_q"], p["keys"]) + p["tabs"]

    def mix_and_stage(i, g):
        p = layers[i]
        m_l, m_c = group_mods(i, g)
        xg, cg = xs[g], cs[g]
        if p["is_ab"]:
            q_c, k_c, v_c, fa_c, fb_c = _ab_in(cg, p["nw_m"], m_c[0], m_c[1], p["w_ext"], cos[:n_ctx], sin[:n_ctx],
                                               dft_c, False, 0, nb)
            q_l, k_l, v_l, fa_l, fb_l = _ab_in(xg, p["nw_m"], m_l[0], m_l[1], p["w_ext"], cos, sin, dft_c, True, 0, nb)
            a_l = _attn_local(q_l, k_l, v_l, k_c, v_c, p["sink"], 0)
            xg = _ab_out(a_l, _dft_pos(fa_l, fb_l, *pos_l), p["w_out"], xg, m_l[2], 0)
            if p["upd_ctx"]:
                a_c = _attn_ctx(q_c, k_c, v_c, p["sink"])
                cg = _ab_out(a_c, _dft_pos(fa_c, fb_c, *pos_c), p["w_out"], cg, m_c[2], 0)
        else:
            xg = _sgu(xg, p["nw_m"], m_l[0], m_l[1], m_l[2], *p["sgu_w"], 0, nb)
            if p["upd_ctx"]:
                cg = _sgu(cg, p["nw_m"], m_c[0], m_c[1], m_c[2], *p["sgu_w"], 0, nb)
        xs[g], cs[g] = xg, cg
        st_l = _peer_stage(xg, peer_params(i, m_l))
        st_c = _peer_stage(cg, peer_params(i, m_c)) if p["upd_ctx"] else None
        return st_l, st_c

    def reduce_rows(i, g, staged):
        m_l, m_c = group_mods(i, g)
        y_l = _peer_reduce(peer_params(i, m_l), staged[0])
        y_c = _peer_reduce(peer_params(i, m_c), staged[1]) if staged[1] is not None else None
        return y_l, y_c

    def add_residual(i, g, ys):
        m_l, m_c = group_mods(i, g)
        xs[g] = _peer_resid(xs[g], peer_params(i, m_l), ys[0])
        if ys[1] is not None:
            cs[g] = _peer_resid(cs[g], peer_params(i, m_c), ys[1])

    staged = [mix_and_stage(0, g) for g in range(n_groups)]
    for i in range(depth):
        ys = [reduce_rows(i, g, staged[g]) for g in range(n_groups)]
        for g in range(n_groups):
            add_residual(i, g, ys[g])
            if i + 1 < depth:
                staged[g] = mix_and_stage(i + 1, g)
    return jnp.concatenate([_final_norm(xg, norm_out_w) for xg in xs], axis=0)
```
